```python
import math
import jax
import jax.numpy as jnp
from jax import lax
import numpy as np

D_MODEL = 1024
BATCH = 32
SEQ = 2048
DEPTH = 1

HEAD_DIM = 64
N_HEADS = D_MODEL // HEAD_DIM
NSA_HEADS = N_HEADS // 2
SB_HEADS = N_HEADS - NSA_HEADS
NSA_KV_GROUPS = 2
NSA_REP = NSA_HEADS // NSA_KV_GROUPS
CMP_BLOCK = 32
CMP_STRIDE = 16
CMP_HIDDEN = 256
SEL_BLOCK = 64
SEL_TOP = 8
SEL_QBLK = 32
WINDOW = 512
QBLK = 128
N_BUCKETS = 32
MAX_DISTANCE = 128
D_FF = 2816
CONV_WIDTH = 3
EPS = 1e-6
NEG_INF = -1e30
FORCED_BONUS = 1e6

NSA_WIDTH = NSA_HEADS * HEAD_DIM
SB_WIDTH = SB_HEADS * HEAD_DIM
KV_WIDTH = NSA_KV_GROUPS * HEAD_DIM
GATE_WIDTH = NSA_HEADS * 3
IN_SPLIT_SIZES = (NSA_WIDTH, KV_WIDTH, KV_WIDTH, KV_WIDTH, KV_WIDTH, KV_WIDTH, KV_WIDTH,
                  GATE_WIDTH, SB_WIDTH, SB_WIDTH, SB_WIDTH)
IN_COLS = sum(IN_SPLIT_SIZES)

kernel_name = 'hybrid_nsa_stickbreaking_convffn'


def _rms_norm(x, w):
    xf = x.astype(jnp.float32)
    y = xf * lax.rsqrt(jnp.mean(xf * xf, axis=-1, keepdims=True) + EPS)
    return (y * w.astype(jnp.float32)).astype(x.dtype)


def _head_rms_norm(o, w):
    B, T, H, Dh = o.shape
    of = o.astype(jnp.float32)
    y = of * lax.rsqrt(jnp.mean(of * of, axis=-1, keepdims=True) + EPS)
    return (y.reshape(B, T, H * Dh) * w.astype(jnp.float32)).astype(o.dtype)


def _split_cols(h):
    outs = []
    start = 0
    for size in IN_SPLIT_SIZES:
        outs.append(h[..., start:start + size])
        start += size
    return outs


def _t5_bucket(dist):
    n = jnp.maximum(dist, 0)
    max_exact = N_BUCKETS // 2
    nf = jnp.maximum(n, 1).astype(jnp.float32)
    log_b = max_exact + (jnp.log(nf / max_exact) / math.log(MAX_DISTANCE / max_exact)
                         * (N_BUCKETS - max_exact)).astype(jnp.int32)
    log_b = jnp.minimum(log_b, N_BUCKETS - 1)
    return jnp.where(n < max_exact, n, log_b)


def _masked_softmax(logits, mask):
    p = jax.nn.softmax(jnp.where(mask, logits, NEG_INF), axis=-1)
    return jnp.where(mask, p, 0.0)


def _compress(kv, pos, w1, w2):
    n_cmp = (kv.shape[2] - CMP_BLOCK) // CMP_STRIDE + 1
    idx = CMP_STRIDE * jnp.arange(n_cmp)[:, None] + jnp.arange(CMP_BLOCK)[None, :]
    blk = kv[:, :, idx] + pos
    blk = blk.reshape(blk.shape[0], blk.shape[1], n_cmp, CMP_BLOCK * HEAD_DIM)
    return jax.nn.gelu(blk @ w1) @ w2


def _gather_blocks(blocks, ix):
    return blocks[ix]


def _nsa(q, kc, vc, ks, vs, kw, vw, gates, pos_k, pos_v, k_w1, k_w2, v_w1, v_w2, rel_bias):
    B, G, R, T, Dh = q.shape
    scale = HEAD_DIM ** -0.5
    t_pos = jnp.arange(T)

    k_cmp = _compress(kc, pos_k, k_w1, k_w2)
    v_cmp = _compress(vc, pos_v, v_w1, v_w2)
    n_cmp = k_cmp.shape[2]
    cmp_end = CMP_STRIDE * jnp.arange(n_cmp) + CMP_BLOCK - 1
    dist_c = t_pos[:, None] - cmp_end[None, :]
    bias_c = rel_bias[_t5_bucket(dist_c)].transpose(2, 0, 1).reshape(G, R, T, n_cmp).astype(jnp.float32)
    logit_c = jnp.einsum('bgrtd,bgnd->bgrtn', q, k_cmp).astype(jnp.float32) * scale + bias_c
    p_cmp = _masked_softmax(logit_c, dist_c >= 0)
    o_cmp = jnp.einsum('bgrtn,bgnd->bgrtd', p_cmp.astype(v_cmp.dtype), v_cmp)

    n_sel = T // SEL_BLOCK
    sel_top = min(SEL_TOP, n_sel)
    sel_j = jnp.arange(n_sel)
    overlap = ((CMP_STRIDE * jnp.arange(n_cmp)[:, None] < SEL_BLOCK * (sel_j[None, :] + 1))
               & (cmp_end[:, None] + 1 > SEL_BLOCK * sel_j[None, :])).astype(jnp.float32)
    imp = jnp.einsum('bgrtn,nj->bgtj', p_cmp, overlap)
    cur = (t_pos // SEL_BLOCK)[:, None]
    causal_blk = sel_j[None, :] <= cur
    forced = (sel_j[None, :] == 0) | (sel_j[None, :] == cur) | (sel_j[None, :] == cur - 1)
    score = jnp.where(causal_blk, imp + jnp.where(forced, FORCED_BONUS, 0.0), NEG_INF)
    top_val, sel_idx = lax.top_k(score, sel_top)
    sel_ok = top_val > 0.5 * NEG_INF
    k_blocks = ks.reshape(B, G, n_sel, SEL_BLOCK, Dh)
    v_blocks = vs.reshape(B, G, n_sel, SEL_BLOCK, Dh)
    rel_bias_g = rel_bias.reshape(N_BUCKETS, G, R)
    g_idx = jnp.arange(G)[None, :, None, None]
    gather = jax.vmap(jax.vmap(_gather_blocks))
    n_kv = sel_top * SEL_BLOCK

    def sel_step(i):
        t0 = i * SEL_QBLK
        qb = lax.dynamic_slice_in_dim(q, t0, SEL_QBLK, axis=3)
        ib = lax.dynamic_slice_in_dim(sel_idx, t0, SEL_QBLK, axis=2)
        okb = lax.dynamic_slice_in_dim(sel_ok, t0, SEL_QBLK, axis=2)
        kg = gather(k_blocks, ib).reshape(B, G, SEL_QBLK, n_kv, Dh)
        vg = gather(v_blocks, ib).reshape(B, G, SEL_QBLK, n_kv, Dh)
        pos = (ib[..., None] * SEL_BLOCK + jnp.arange(SEL_BLOCK)).reshape(B, G, SEL_QBLK, n_kv)
        dist = (t0 + jnp.arange(SEL_QBLK))[:, None] - pos
        ok = jnp.broadcast_to(okb[..., None], okb.shape + (SEL_BLOCK,)).reshape(B, G, SEL_QBLK, n_kv) & (dist >= 0)
        bias = rel_bias_g[_t5_bucket(dist), g_idx].transpose(0, 1, 4, 2, 3).astype(jnp.float32)
        logits = jnp.einsum('bgrqd,bgqkd->bgrqk', qb, kg).astype(jnp.float32) * scale + bias
        p = _masked_softmax(logits, ok[:, :, None])
        return jnp.einsum('bgrqk,bgqkd->bgrqd', p.astype(vg.dtype), vg)

    o_sel = lax.map(sel_step, jnp.arange(T // SEL_QBLK))
    o_sel = o_sel.transpose(1, 2, 3, 0, 4, 5).reshape(B, G, R, T, Dh)

    span = QBLK + WINDOW
    k_pad = jnp.pad(kw, ((0, 0), (0, 0), (WINDOW, 0), (0, 0)))
    v_pad = jnp.pad(vw, ((0, 0), (0, 0), (WINDOW, 0), (0, 0)))
    band = jnp.arange(QBLK)[:, None] + WINDOW - jnp.arange(span)[None, :]
    band_ok = (band >= 0) & (band < WINDOW)
    bias_w = rel_bias[_t5_bucket(band)].transpose(2, 0, 1).reshape(G, R, QBLK, span).astype(jnp.float32)

    def win_step(i):
        t0 = i * QBLK
        qb = lax.dynamic_slice_in_dim(q, t0, QBLK, axis=3)
        kb = lax.dynamic_slice_in_dim(k_pad, t0, span, axis=2)
        vb = lax.dynamic_slice_in_dim(v_pad, t0, span, axis=2)
        s_real = t0 - WINDOW + jnp.arange(span)
        ok = band_ok & (s_real >= 0)[None, :]
        logits = jnp.einsum('bgrqd,bgkd->bgrqk', qb, kb).astype(jnp.float32) * scale + bias_w
        p = _masked_softmax(logits, ok)
        return jnp.einsum('bgrqk,bgkd->bgrqd', p.astype(vb.dtype), vb)

    o_win = lax.map(win_step, jnp.arange(T // QBLK))
    o_win = o_win.transpose(1, 2, 3, 0, 4, 5).reshape(B, G, R, T, Dh)

    o = gates[0] * o_cmp + gates[1] * o_sel + gates[2] * o_win
    return o.transpose(0, 3, 1, 2, 4).reshape(B, T, G * R, Dh)


def _stick_breaking(q, k, v):
    T = q.shape[2]
    scale = HEAD_DIM ** -0.5
    outs = []
    for nb in range(T // QBLK):
        t0 = nb * QBLK
        L = t0 + QBLK
        qb = q[:, :, t0:L]
        kb = k[:, :, :L]
        vb = v[:, :, :L]
        z = jnp.einsum('bhqd,bhkd->bhqk', qb, kb).astype(jnp.float32) * scale
        mask = jnp.arange(L)[None, :] < (t0 + jnp.arange(QBLK))[:, None]
        log_rest = jnp.where(mask, jax.nn.log_sigmoid(-z), 0.0)
        after = lax.cumsum(log_rest, axis=3, reverse=True) - log_rest
        a = jnp.where(mask, jnp.exp(jax.nn.log_sigmoid(z) + after), 0.0)
        outs.append(jnp.einsum('bhqk,bhkd->bhqd', a.astype(vb.dtype), vb))
    return jnp.concatenate(outs, axis=2)


def setup_inputs(seed: int = 0) -> dict:
    key = jax.random.key(seed)
    ks = jax.random.split(key, 24)
    f32 = jnp.float32
    L = DEPTH

    def normal(k, shape, scale):
        return jax.random.normal(k, shape, f32) * scale

    def gain(k, shape):
        return 1.0 + 0.02 * jax.random.normal(k, shape, f32)

    return {
        'x': normal(ks[0], (BATCH, SEQ, D_MODEL), 1.0),
        'norm1_w': gain(ks[1], (L, D_MODEL)),
        'w_in': normal(ks[2], (L, D_MODEL, IN_COLS), D_MODEL ** -0.5),
        'cmp_pos_k': normal(ks[3], (L, CMP_BLOCK, HEAD_DIM), 0.1),
        'cmp_pos_v': normal(ks[4], (L, CMP_BLOCK, HEAD_DIM), 0.1),
        'cmp_k_w1': normal(ks[5], (L, CMP_BLOCK * HEAD_DIM, CMP_HIDDEN), (CMP_BLOCK * HEAD_DIM) ** -0.5),
        'cmp_k_w2': normal(ks[6], (L, CMP_HIDDEN, HEAD_DIM), CMP_HIDDEN ** -0.5),
        'cmp_v_w1': normal(ks[7], (L, CMP_BLOCK * HEAD_DIM, CMP_HIDDEN), (CMP_BLOCK * HEAD_DIM) ** -0.5),
        'cmp_v_w2': normal(ks[8], (L, CMP_HIDDEN, HEAD_DIM), CMP_HIDDEN ** -0.5),
        'gate_b': normal(ks[9], (L, GATE_WIDTH), 0.01),
        'nsa_out_norm_w': gain(ks[10], (L, NSA_WIDTH)),
        'sb_out_norm_w': gain(ks[11], (L, SB_WIDTH)),
        'w_out': normal(ks[12], (L, D_MODEL, D_MODEL), D_MODEL ** -0.5),
        'norm2_w': gain(ks[13], (L, D_MODEL)),
        'w_up': normal(ks[14], (L, D_MODEL, 2 * D_FF), D_MODEL ** -0.5),
        'conv_w': normal(ks[15], (L, CONV_WIDTH, 2 * D_FF), CONV_WIDTH ** -0.5),
        'conv_b': normal(ks[16], (L, 2 * D_FF), 0.01),
        'w_down': normal(ks[17], (L, D_FF, D_MODEL), D_FF ** -0.5),
        'rel_bias': normal(ks[18], (N_BUCKETS, NSA_HEADS), 0.1),
        'final_norm_w': gain(ks[19], (D_MODEL,)),
    }


def _to_q_groups(a, B, T):
    return a.reshape(B, T, NSA_KV_GROUPS, NSA_REP, HEAD_DIM).transpose(0, 2, 3, 1, 4)


def _to_kv_groups(a, B, T):
    return a.reshape(B, T, NSA_KV_GROUPS, HEAD_DIM).transpose(0, 2, 1, 3)


def _to_heads(a, B, T):
    return a.reshape(B, T, SB_HEADS, HEAD_DIM).transpose(0, 2, 1, 3)


def reference(x, norm1_w, w_in, cmp_pos_k, cmp_pos_v, cmp_k_w1, cmp_k_w2, cmp_v_w1, cmp_v_w2,
              gate_b, nsa_out_norm_w, sb_out_norm_w, w_out, norm2_w, w_up, conv_w, conv_b,
              w_down, rel_bias, final_norm_w):
    B, T, _ = x.shape
    h = x
    for layer in range(DEPTH):
        u = _rms_norm(h, norm1_w[layer])
        proj = u @ w_in[layer]
        q_n, kc, vc, ks, vs, kw, vw, g_lin, q_s, k_s, v_s = _split_cols(proj)
        gates = jax.nn.sigmoid((g_lin + gate_b[layer]).astype(jnp.float32)).astype(x.dtype)
        gates = gates.reshape(B, T, NSA_KV_GROUPS, NSA_REP, 3).transpose(4, 0, 2, 3, 1)[..., None]
        o_nsa = _nsa(_to_q_groups(q_n, B, T),
                     _to_kv_groups(kc, B, T), _to_kv_groups(vc, B, T),
                     _to_kv_groups(ks, B, T), _to_kv_groups(vs, B, T),
                     _to_kv_groups(kw, B, T), _to_kv_groups(vw, B, T),
                     gates, cmp_pos_k[layer], cmp_pos_v[layer],
                     cmp_k_w1[layer], cmp_k_w2[layer], cmp_v_w1[layer], cmp_v_w2[layer],
                     rel_bias)
        o_sb = _stick_breaking(_to_heads(q_s, B, T), _to_heads(k_s, B, T),
                               _to_heads(v_s, B, T)).transpose(0, 2, 1, 3)
        mixed = jnp.concatenate([_head_rms_norm(o_nsa, nsa_out_norm_w[layer]),
                                 _head_rms_norm(o_sb, sb_out_norm_w[layer])], axis=-1)
        h = h + mixed @ w_out[layer]
        u = _rms_norm(h, norm2_w[layer])
        up = u @ w_up[layer]
        up = lax.conv_general_dilated(up, conv_w[layer][:, None, :].astype(up.dtype),
                                      window_strides=(1,), padding=[(CONV_WIDTH - 1, 0)],
                                      dimension_numbers=('NWC', 'WIO', 'NWC'),
                                      feature_group_count=2 * D_FF) + conv_b[layer]
        gate, val = up[..., :D_FF], up[..., D_FF:]
        h = h + (jax.nn.silu(gate) * val) @ w_down[layer]
    return _rms_norm(h, final_norm_w)
```

```python
import functools
import math

import jax
import jax.numpy as jnp
import numpy as np
from jax import lax
from jax.experimental import pallas as pl
from jax.experimental.pallas import tpu as pltpu

F32 = jnp.float32
BF16 = jnp.bfloat16

HEAD_DIM = 64
NSA_HEADS = 8
SB_HEADS = 8
NSA_GROUPS = 2
NSA_REP = NSA_HEADS // NSA_GROUPS
CMP_BLOCK = 32
CMP_STRIDE = 16
SEL_BLOCK = 64
SEL_TOP = 8
WINDOW = 512
N_BUCKETS = 32
MAX_DISTANCE = 128
CONV_WIDTH = 3
EPS = 1e-6
NEG_INF = -1e30
FORCED_BONUS = 1e6

LANES = 128
TQ = 128
TK = 128
ROW_TILE = 512
FFN_CHUNK = 256
HALO = 16
VMEM_LIMIT = 56 * 1024 * 1024
WIN_TILES = WINDOW // TK + 1


def _dot(a, b):
    return jnp.dot(a, b, preferred_element_type=F32)


def _dot_nt(a, b):
    return lax.dot_general(a, b, (((1,), (1,)), ((), ())), preferred_element_type=F32)


def _params(*sem):
    return pltpu.CompilerParams(dimension_semantics=sem, vmem_limit_bytes=VMEM_LIMIT)


def _t5_bucket(dist):
    n = jnp.maximum(dist, 0)
    max_exact = N_BUCKETS // 2
    nf = jnp.maximum(n, 1).astype(F32)
    log_b = max_exact + (jnp.log(nf / max_exact) / math.log(MAX_DISTANCE / max_exact)
                         * (N_BUCKETS - max_exact)).astype(jnp.int32)
    log_b = jnp.minimum(log_b, N_BUCKETS - 1)
    return jnp.where(n < max_exact, n, log_b)


def _inproj_kernel(x_ref, nw_ref, w_ref, gb_ref, qn_ref, kcvc_ref, kv_ref, gates_ref,
                   sbq_ref, sbk_ref, sbv_ref, *, splits):
    x = x_ref[...]
    u = x * lax.rsqrt(jnp.mean(x * x, axis=-1, keepdims=True) + EPS) * nw_ref[...]
    u = u.astype(BF16)
    outs = (qn_ref, kcvc_ref, kv_ref, gates_ref, sbq_ref, sbk_ref, sbv_ref)
    start = 0
    for ref, size in zip(outs, splits):
        r = _dot(u, w_ref[:, start:start + size])
        if ref is gates_ref:
            r = jax.nn.sigmoid(r + gb_ref[...])
        ref[...] = r.astype(ref.dtype)
        start += size


def _inproj(x2, norm_w, w_all, gate_b_pad, splits):
    n, d = x2.shape
    dts = (BF16, F32, BF16, F32, BF16, BF16, BF16)
    return pl.pallas_call(
        functools.partial(_inproj_kernel, splits=splits),
        grid=(n // ROW_TILE,),
        in_specs=[
            pl.BlockSpec((ROW_TILE, d), lambda i: (i, 0)),
            pl.BlockSpec((1, d), lambda i: (0, 0)),
            pl.BlockSpec(w_all.shape, lambda i: (0, 0)),
            pl.BlockSpec((1, LANES), lambda i: (0, 0)),
        ],
        out_specs=[pl.BlockSpec((ROW_TILE, s), lambda i: (i, 0)) for s in splits],
        out_shape=[jax.ShapeDtypeStruct((n, s), dt) for s, dt in zip(splits, dts)],
        compiler_params=_params("parallel"),
        name="inproj",
    )(x2, norm_w, w_all, gate_b_pad)


def _gelu_tanh(x):
    return 0.5 * x * (1.0 + jnp.tanh(math.sqrt(2.0 / math.pi) * (x + 0.044715 * (x * x * x))))


def _compress_kernel(c_ref, pos_ref, w1_ref, w2_ref, kc_ref, vc_ref):
    n_chunk = c_ref.shape[2]
    row = lax.broadcasted_iota(jnp.int32, (n_chunk, LANES), 0)
    for kv, out_ref in ((0, kc_ref), (1, vc_ref)):
        acc = jnp.zeros((n_chunk, LANES), F32)
        for g in range(NSA_GROUPS):
            c = c_ref[0, 2 * kv + g]
            top = _dot((c + pos_ref[kv, 0:1, :]).astype(BF16), w1_ref[kv, 0])
            bot = _dot((c + pos_ref[kv, 1:2, :]).astype(BF16), w1_ref[kv, 1])
            hidden = top + pltpu.roll(bot, n_chunk - 1, axis=0)
            acc = acc + _dot(_gelu_tanh(hidden).astype(BF16), w2_ref[kv, g])
        out_ref[0] = jnp.where(row < n_chunk - 1, acc, 0.0).astype(out_ref.dtype)


def _compress(chunks, pos, w1, w2):
    b, _, n_chunk, width = chunks.shape
    hid = w1.shape[-1]
    out = jax.ShapeDtypeStruct((b, n_chunk, LANES), BF16)
    return pl.pallas_call(
        _compress_kernel,
        grid=(b,),
        in_specs=[
            pl.BlockSpec((1, 4, n_chunk, width), lambda i: (i, 0, 0, 0)),
            pl.BlockSpec((2, 2, width), lambda i: (0, 0, 0)),
            pl.BlockSpec((2, 2, width, hid), lambda i: (0, 0, 0, 0)),
            pl.BlockSpec((2, 2, hid, LANES), lambda i: (0, 0, 0, 0)),
        ],
        out_specs=[pl.BlockSpec((1, n_chunk, LANES), lambda i: (i, 0, 0))] * 2,
        out_shape=[out, out],
        compiler_params=_params("parallel"),
        name="compress",
    )(chunks, pos, w1, w2)


def _nsa_kernel(q_ref, ks_ref, vs_ref, kw_ref, vw_ref, kc_ref, vc_ref, gates_ref,
                biasc_ref, biass_ref, biasw_ref, ovl_ref, expand_ref, normw_ref,
                o_ref, s_ref, maskb_ref):
    i = pl.program_id(1)
    t0 = i * TQ
    rows = NSA_REP * TQ
    lane = lax.broadcasted_iota(jnp.int32, (rows, LANES), 1)
    gates = gates_ref[0]

    n_selp = ovl_ref.shape[0]
    blk = lax.broadcasted_iota(jnp.int32, (n_selp, TQ), 0)
    cur = (t0 + lax.broadcasted_iota(jnp.int32, (n_selp, TQ), 1)) // SEL_BLOCK
    causal_blk = blk <= cur
    forced = (blk == 0) | (blk == cur) | (blk == cur - 1)

    for g in range(NSA_GROUPS):
        qg = jnp.concatenate(
            [q_ref[0, :, (g * NSA_REP + r) * LANES:(g * NSA_REP + r + 1) * LANES]
             for r in range(NSA_REP)], axis=0)

        lc = _dot_nt(qg, kc_ref[0]) + biasc_ref[g, 0]
        mc = jnp.max(lc, axis=-1, keepdims=True)
        ec = jnp.exp(lc - mc)
        inv = jnp.where(mc > 0.5 * NEG_INF, 1.0 / jnp.sum(ec, axis=-1, keepdims=True), 0.0)
        pc = (ec * inv).astype(BF16)
        o_cmp = _dot(pc, vc_ref[0])

        imp4 = _dot_nt(ovl_ref[...], pc)
        imp = imp4[:, 0:TQ]
        for r in range(1, NSA_REP):
            imp = imp + imp4[:, r * TQ:(r + 1) * TQ]
        score = jnp.where(causal_blk, imp + jnp.where(forced, FORCED_BONUS, 0.0), NEG_INF)
        rank = jnp.zeros((n_selp, TQ), F32)
        for j in range(n_selp):
            other = score[j:j + 1, :]
            ahead = (other > score) | ((other == score) & (blk > j))
            rank = rank + jnp.where(ahead, 1.0, 0.0)
        sel_t = jnp.where((rank < SEL_TOP) & causal_blk, 1.0, 0.0)
        sel_t = jnp.concatenate([sel_t, jnp.zeros((LANES - n_selp, TQ), F32)], axis=0)
        sel = sel_t.T.astype(BF16)
        maskb_ref[...] = (_dot(sel, expand_ref[...]) - 1.0) * (-NEG_INF)

        def sel_logits(j, m_run):
            off = pl.multiple_of(j * TK, TK)
            boff = pl.multiple_of((2 - jnp.minimum(i - j, 2)) * TK, TK)
            s = _dot_nt(qg, ks_ref[0, pl.ds(off, TK), :]) + biass_ref[g, :, pl.ds(boff, TK)]
            mb = maskb_ref[:, pl.ds(off, TK)]
            s = s + jnp.concatenate([mb] * NSA_REP, axis=0)
            s_ref[:, pl.ds(off, TK)] = s
            return jnp.maximum(m_run, s)

        m_run = lax.fori_loop(0, i + 1, sel_logits, jnp.full((rows, TK), NEG_INF, F32))
        m_sel = jnp.max(m_run, axis=-1, keepdims=True)

        def sel_pv(j, carry):
            acc, l_run = carry
            off = pl.multiple_of(j * TK, TK)
            p = jnp.exp(s_ref[:, pl.ds(off, TK)] - m_sel)
            return acc + _dot(p.astype(BF16), vs_ref[0, pl.ds(off, TK), :]), l_run + p

        acc, l_run = lax.fori_loop(0, i + 1, sel_pv,
                                   (jnp.zeros((rows, LANES), F32), jnp.zeros((rows, TK), F32)))
        o_sel = acc * (1.0 / jnp.sum(l_run, axis=-1, keepdims=True))

        j_lo = jnp.maximum(i - (WIN_TILES - 1), 0)

        def win_logits(j, m_run):
            off = pl.multiple_of(j * TK, TK)
            boff = pl.multiple_of((j - i + WIN_TILES - 1) * TK, TK)
            s = _dot_nt(qg, kw_ref[0, pl.ds(off, TK), :]) + biasw_ref[g, :, pl.ds(boff, TK)]
            s_ref[:, pl.ds(off, TK)] = s
            return jnp.maximum(m_run, s)

        m_run = lax.fori_loop(j_lo, i + 1, win_logits, jnp.full((rows, TK), NEG_INF, F32))
        m_win = jnp.max(m_run, axis=-1, keepdims=True)

        def win_pv(j, carry):
            acc, l_run = carry
            off = pl.multiple_of(j * TK, TK)
            p = jnp.exp(s_ref[:, pl.ds(off, TK)] - m_win)
            return acc + _dot(p.astype(BF16), vw_ref[0, pl.ds(off, TK), :]), l_run + p

        acc, l_run = lax.fori_loop(j_lo, i + 1, win_pv,
                                   (jnp.zeros((rows, LANES), F32), jnp.zeros((rows, TK), F32)))
        o_win = acc * (1.0 / jnp.sum(l_run, axis=-1, keepdims=True))

        def gate(branch):
            return jnp.concatenate(
                [gates[:, (g * NSA_REP + r) * 3 + branch:(g * NSA_REP + r) * 3 + branch + 1]
                 for r in range(NSA_REP)], axis=0)

        o = gate(0) * o_cmp + gate(1) * o_sel + gate(2) * o_win
        mine = (lane // HEAD_DIM) == g
        ssq = jnp.sum(jnp.where(mine, o * o, 0.0), axis=-1, keepdims=True)
        y = o * lax.rsqrt(ssq * (1.0 / HEAD_DIM) + EPS)
        left = lax.broadcasted_iota(jnp.int32, (TQ, LANES), 1) < HEAD_DIM
        for pair in range(NSA_REP // 2):
            even = y[(2 * pair) * TQ:(2 * pair + 1) * TQ]
            odd = y[(2 * pair + 1) * TQ:(2 * pair + 2) * TQ]
            if g == 0:
                odd = pltpu.roll(odd, HEAD_DIM, axis=1)
            else:
                even = pltpu.roll(even, HEAD_DIM, axis=1)
            cb = g * (NSA_REP // 2) + pair
            packed = jnp.where(left, even, odd) * normw_ref[:, cb * LANES:(cb + 1) * LANES]
            o_ref[0, :, cb * LANES:(cb + 1) * LANES] = packed.astype(o_ref.dtype)


def _nsa(qn, kv, kcmp, vcmp, gates, biasc, biass, biasw, ovl, expand, normw, b, t):
    nq = t // TQ
    rows = NSA_REP * TQ
    qn3 = qn.reshape(b, t, qn.shape[-1])
    kv3 = kv.reshape(b, t, kv.shape[-1])
    gates3 = gates.reshape(b, t, LANES)
    kv_spec = lambda c: pl.BlockSpec((1, t, LANES), lambda bi, qi, c=c: (bi, 0, c))
    return pl.pallas_call(
        _nsa_kernel,
        grid=(b, nq),
        in_specs=[
            pl.BlockSpec((1, TQ, qn3.shape[-1]), lambda bi, qi: (bi, qi, 0)),
            kv_spec(0), kv_spec(1), kv_spec(2), kv_spec(3),
            pl.BlockSpec((1,) + kcmp.shape[1:], lambda bi, qi: (bi, 0, 0)),
            pl.BlockSpec((1,) + vcmp.shape[1:], lambda bi, qi: (bi, 0, 0)),
            pl.BlockSpec((1, TQ, LANES), lambda bi, qi: (bi, qi, 0)),
            pl.BlockSpec((NSA_GROUPS, 1, rows, LANES), lambda bi, qi: (0, qi, 0, 0)),
            pl.BlockSpec(biass.shape, lambda bi, qi: (0, 0, 0)),
            pl.BlockSpec(biasw.shape, lambda bi, qi: (0, 0, 0)),
            pl.BlockSpec(ovl.shape, lambda bi, qi: (0, 0)),
            pl.BlockSpec(expand.shape, lambda bi, qi: (0, 0)),
            pl.BlockSpec(normw.shape, lambda bi, qi: (0, 0)),
        ],
        out_specs=pl.BlockSpec((1, TQ, NSA_HEADS * HEAD_DIM), lambda bi, qi: (bi, qi, 0)),
        out_shape=jax.ShapeDtypeStruct((b, t, NSA_HEADS * HEAD_DIM), BF16),
        scratch_shapes=[pltpu.VMEM((rows, t), F32), pltpu.VMEM((TQ, t), F32)],
        compiler_params=_params("parallel", "arbitrary"),
        name="nsa",
    )(qn3, kv3, kv3, kv3, kv3, kcmp, vcmp, gates3, biasc, biass, biasw, ovl, expand, normw)


def _sb_kernel(q_ref, k_ref, v_ref, tri_ref, normw_ref, o_ref):
    i = pl.program_id(1)
    rows = 2 * TQ
    lane = lax.broadcasted_iota(jnp.int32, (rows, LANES), 1)
    tl = lax.broadcasted_iota(jnp.int32, (rows, LANES), 0) & (TQ - 1)
    strict = lane < tl
    mine = (lane // HEAD_DIM) == (lax.broadcasted_iota(jnp.int32, (rows, LANES), 0) // TQ)
    left = lax.broadcasted_iota(jnp.int32, (TQ, LANES), 1) < HEAD_DIM

    for pair in range(SB_HEADS // 2):
        cols = slice(pair * LANES, (pair + 1) * LANES)
        qp = jnp.concatenate(
            [q_ref[0, :, (2 * pair + h) * LANES:(2 * pair + h + 1) * LANES] for h in range(2)],
            axis=0)

        def tile(j, carry, diag):
            acc, after_later = carry
            off = pl.multiple_of(j * TK, TK)
            z = _dot_nt(qp, k_ref[0, pl.ds(off, TK), cols])
            log_beta = jnp.minimum(z, 0.0) - jnp.log(1.0 + jnp.exp(-jnp.abs(z)))
            log_rest = log_beta - z
            if diag:
                log_rest = jnp.where(strict, log_rest, 0.0)
            hi = log_rest.astype(BF16)
            lo = (log_rest - hi.astype(F32)).astype(BF16)
            csum = _dot(hi, tri_ref[...]) + _dot(lo, tri_ref[...])
            a = jnp.exp(log_beta + csum[:, :TK] + after_later)
            if diag:
                a = jnp.where(strict, a, 0.0)
            acc = acc + _dot(a.astype(BF16), v_ref[0, pl.ds(off, TK), cols])
            return acc, after_later + csum[:, TK:]

        zero = jnp.zeros((rows, LANES), F32)
        carry = tile(i, (zero, zero), True)
        acc, _ = lax.fori_loop(0, i, lambda s, c: tile(i - 1 - s, c, False), carry)

        ssq = jnp.sum(jnp.where(mine, acc * acc, 0.0), axis=-1, keepdims=True)
        y = acc * lax.rsqrt(ssq * (1.0 / HEAD_DIM) + EPS)
        packed = jnp.where(left, y[:TQ], y[TQ:]) * normw_ref[:, cols]
        o_ref[0, :, cols] = packed.astype(o_ref.dtype)


def _sb(q, k, v, tri, normw, b, t):
    q3 = q.reshape(b, t, q.shape[-1])
    k3 = k.reshape(b, t, k.shape[-1])
    v3 = v.reshape(b, t, v.shape[-1])
    width = SB_HEADS * HEAD_DIM
    return pl.pallas_call(
        _sb_kernel,
        grid=(b, t // TQ),
        in_specs=[
            pl.BlockSpec((1, TQ, q3.shape[-1]), lambda bi, qi: (bi, qi, 0)),
            pl.BlockSpec((1, t, width), lambda bi, qi: (bi, 0, 0)),
            pl.BlockSpec((1, t, width), lambda bi, qi: (bi, 0, 0)),
            pl.BlockSpec(tri.shape, lambda bi, qi: (0, 0)),
            pl.BlockSpec(normw.shape, lambda bi, qi: (0, 0)),
        ],
        out_specs=pl.BlockSpec((1, TQ, width), lambda bi, qi: (bi, qi, 0)),
        out_shape=jax.ShapeDtypeStruct((b, t, width), BF16),
        compiler_params=_params("parallel", "arbitrary"),
        name="stickbreak",
    )(q3, k3, v3, tri, normw)


def _outproj_kernel(x_ref, on_ref, os_ref, wn_ref, ws_ref, nw_ref, h_ref, u_ref):
    h = x_ref[...] + _dot(on_ref[...], wn_ref[...]) + _dot(os_ref[...], ws_ref[...])
    h_ref[...] = h
    u = h * lax.rsqrt(jnp.mean(h * h, axis=-1, keepdims=True) + EPS) * nw_ref[...]
    u_ref[...] = u.astype(u_ref.dtype)


def _outproj(x2, o_nsa, o_sb, w_n, w_s, norm_w):
    n, d = x2.shape
    row = lambda w: pl.BlockSpec((ROW_TILE, w), lambda i: (i, 0))
    full = lambda a: pl.BlockSpec(a.shape, lambda i: (0, 0))
    return pl.pallas_call(
        _outproj_kernel,
        grid=(n // ROW_TILE,),
        in_specs=[row(d), row(o_nsa.shape[-1]), row(o_sb.shape[-1]), full(w_n), full(w_s), full(norm_w)],
        out_specs=[row(d), row(d)],
        out_shape=[jax.ShapeDtypeStruct((n, d), F32), jax.ShapeDtypeStruct((n, d), BF16)],
        compiler_params=_params("parallel"),
        name="outproj",
    )(x2, o_nsa, o_sb, w_n, w_s, norm_w)


def _ffn_kernel(h_ref, u_ref, halo_ref, wup_ref, cw_ref, cb_ref, wdown_ref, fw_ref, o_ref, *, d_ff):
    i = pl.program_id(1)
    halo = jnp.where(i > 0, halo_ref[0], jnp.zeros_like(halo_ref[0]))
    ue = jnp.concatenate([halo, u_ref[0]], axis=0)
    tile = u_ref.shape[1]

    def conv(up, c0):
        w = cw_ref[:, c0:c0 + FFN_CHUNK]
        return (w[2:3] * up[HALO:] + w[1:2] * up[HALO - 1:HALO - 1 + tile]
                + w[0:1] * up[HALO - 2:HALO - 2 + tile] + cb_ref[:, c0:c0 + FFN_CHUNK])

    acc = h_ref[0]
    for c in range(d_ff // FFN_CHUNK):
        g0, v0 = c * FFN_CHUNK, d_ff + c * FFN_CHUNK
        gate = conv(_dot(ue, wup_ref[:, g0:g0 + FFN_CHUNK]), g0)
        val = conv(_dot(ue, wup_ref[:, v0:v0 + FFN_CHUNK]), v0)
        act = gate * (1.0 / (1.0 + jnp.exp(-gate))) * val
        acc = acc + _dot(act.astype(BF16), wdown_ref[g0:g0 + FFN_CHUNK, :])
    y = acc * lax.rsqrt(jnp.mean(acc * acc, axis=-1, keepdims=True) + EPS) * fw_ref[...]
    o_ref[0] = y.astype(o_ref.dtype)


def _ffn(h, u, w_up, conv_w, conv_b, w_down, final_w, b, t):
    d = h.shape[-1]
    d_ff = w_down.shape[0]
    h3 = h.reshape(b, t, d)
    u3 = u.reshape(b, t, d)
    per = ROW_TILE // HALO
    full = lambda a: pl.BlockSpec(a.shape, lambda bi, ti: (0, 0))
    return pl.pallas_call(
        functools.partial(_ffn_kernel, d_ff=d_ff),
        grid=(b, t // ROW_TILE),
        in_specs=[
            pl.BlockSpec((1, ROW_TILE, d), lambda bi, ti: (bi, ti, 0)),
            pl.BlockSpec((1, ROW_TILE, d), lambda bi, ti: (bi, ti, 0)),
            pl.BlockSpec((1, HALO, d), lambda bi, ti: (bi, jnp.maximum(ti * per - 1, 0), 0)),
            full(w_up), full(conv_w), full(conv_b), full(w_down), full(final_w),
        ],
        out_specs=pl.BlockSpec((1, ROW_TILE, d), lambda bi, ti: (bi, ti, 0)),
        out_shape=jax.ShapeDtypeStruct((b, t, d), F32),
        compiler_params=_params("parallel", "arbitrary"),
        name="convffn",
    )(h3, u3, u3, w_up, conv_w, conv_b, w_down, final_w)


def _pad_half(w, half):
    z = jnp.zeros_like(w)
    return jnp.concatenate([w, z] if half == 0 else [z, w], axis=-1)


def _input_weights(w):
    d = w.shape[0]
    nq, kvw = NSA_HEADS * HEAD_DIM, NSA_GROUPS * HEAD_DIM
    scale = HEAD_DIM ** -0.5
    o = 0
    q_n = w[:, o:o + nq] * scale; o += nq
    kc = w[:, o:o + kvw]; o += kvw
    vc = w[:, o:o + kvw]; o += kvw
    ks = w[:, o:o + kvw]; o += kvw
    vs = w[:, o:o + kvw]; o += kvw
    kw = w[:, o:o + kvw]; o += kvw
    vw = w[:, o:o + kvw]; o += kvw
    gl = w[:, o:o + NSA_HEADS * 3]; o += NSA_HEADS * 3
    sbw = SB_HEADS * HEAD_DIM
    q_s = w[:, o:o + sbw] * scale; o += sbw
    k_s = w[:, o:o + sbw]; o += sbw
    v_s = w[:, o:o + sbw]; o += sbw
    q_n = q_n.reshape(d, NSA_GROUPS, NSA_REP, HEAD_DIM)
    q_n = jnp.concatenate([_pad_half(q_n[:, g], g) for g in range(NSA_GROUPS)], axis=1).reshape(d, -1)
    q_s = q_s.reshape(d, SB_HEADS // 2, 2, HEAD_DIM)
    q_s = jnp.stack([_pad_half(q_s[:, :, h], h) for h in range(2)], axis=2).reshape(d, -1)
    gl = jnp.pad(gl, ((0, 0), (0, LANES - gl.shape[1])))
    pieces = (q_n, jnp.concatenate([kc, vc], axis=1), jnp.concatenate([ks, vs, kw, vw], axis=1),
              gl, q_s, k_s, v_s)
    return jnp.concatenate(pieces, axis=1).astype(BF16), tuple(p.shape[1] for p in pieces)


def _bias_tables(rel_bias, t):
    nq = t // TQ
    by_dist = rel_bias[_t5_bucket(jnp.arange(t + WINDOW))]
    by_dist = by_dist.T.reshape(NSA_GROUPS, NSA_REP, -1).astype(F32)
    tl = np.arange(TQ)[:, None]

    def table(dist, ok):
        b = by_dist[:, :, np.maximum(dist, 0)]
        b = jnp.where(jnp.asarray(ok), b, NEG_INF)
        return b.reshape(NSA_GROUPS, NSA_REP * TQ, dist.shape[1])

    dw = tl + WINDOW - np.arange(WINDOW + TQ)[None, :]
    biasw = table(dw, (dw >= 0) & (dw < WINDOW))
    ds = tl + 2 * TK - np.arange(3 * TK)[None, :]
    biass = table(ds, ds >= 0)
    n_cmp = (t - CMP_BLOCK) // CMP_STRIDE + 1
    n = np.arange(LANES)[None, :]
    dc = np.arange(t)[:, None] - (CMP_STRIDE * n + CMP_BLOCK - 1)
    bc = by_dist[:, :, np.maximum(dc, 0)]
    bc = jnp.where(jnp.asarray((dc >= 0) & (n < n_cmp)), bc, NEG_INF)
    biasc = bc.reshape(NSA_GROUPS, NSA_REP, nq, TQ, LANES).transpose(0, 2, 1, 3, 4)
    biasc = biasc.reshape(NSA_GROUPS, nq, NSA_REP * TQ, LANES)
    return biasc, biass, biasw


def _selection_constants(t):
    n_cmp = (t - CMP_BLOCK) // CMP_STRIDE + 1
    n_sel = t // SEL_BLOCK
    n_selp = -(-n_sel // 8) * 8
    n = np.arange(LANES)[None, :]
    j = np.arange(n_selp)[:, None]
    ovl = ((CMP_STRIDE * n < SEL_BLOCK * (j + 1)) & (CMP_STRIDE * n + CMP_BLOCK > SEL_BLOCK * j)
           & (n < n_cmp) & (j < n_sel))
    jj = np.arange(LANES)[:, None]
    expand = (np.arange(t)[None, :] // SEL_BLOCK) == jj
    return jnp.asarray(ovl, BF16), jnp.asarray(expand, BF16)


def kernel(x, norm1_w, w_in, cmp_pos_k, cmp_pos_v, cmp_k_w1, cmp_k_w2, cmp_v_w1, cmp_v_w2, gate_b,
           nsa_out_norm_w, sb_out_norm_w, w_out, norm2_w, w_up, conv_w, conv_b, w_down, rel_bias,
           final_norm_w):
    b, t, d = x.shape
    assert t % ROW_TILE == 0 and t >= WINDOW + TQ and (b * t) % ROW_TILE == 0
    assert (t - CMP_BLOCK) // CMP_STRIDE + 1 < LANES + 1 and t // CMP_STRIDE == LANES
    assert w_in.shape[0] == 1, "single-layer block: the closing norm is fused into the FFN kernel"
    half = CMP_STRIDE * HEAD_DIM
    nsa_w = NSA_HEADS * HEAD_DIM
    l = 0

    biasc, biass, biasw = _bias_tables(rel_bias, t)
    ovl, expand = _selection_constants(t)
    r_ = np.arange(TK)[:, None]
    c_ = np.arange(TK)[None, :]
    tri = jnp.asarray(np.concatenate([r_ > c_, np.ones((TK, TK), bool)], axis=1), BF16)

    h = x.reshape(b * t, d)
    w_all, splits = _input_weights(w_in[l])
    gb = jnp.pad(gate_b[l], (0, LANES - gate_b.shape[1])).reshape(1, LANES)
    qn, kcvc, kv, gates, sbq, sbk, sbv = _inproj(h, norm1_w[l].reshape(1, d), w_all, gb, splits)

    chunks = kcvc.reshape(b, t // CMP_STRIDE, CMP_STRIDE, 4, HEAD_DIM)
    chunks = chunks.transpose(0, 3, 1, 2, 4).reshape(b, 4, t // CMP_STRIDE, half)
    pos = jnp.stack([cmp_pos_k[l].reshape(2, half), cmp_pos_v[l].reshape(2, half)])
    w1 = jnp.stack([cmp_k_w1[l].reshape(2, half, -1), cmp_v_w1[l].reshape(2, half, -1)]).astype(BF16)
    w2 = jnp.stack([jnp.stack([_pad_half(w, g) for g in range(NSA_GROUPS)])
                    for w in (cmp_k_w2[l], cmp_v_w2[l])]).astype(BF16)
    kcmp, vcmp = _compress(chunks, pos, w1, w2)

    o_nsa = _nsa(qn, kv, kcmp, vcmp, gates, biasc, biass, biasw, ovl, expand,
                 nsa_out_norm_w[l].reshape(1, nsa_w), b, t)
    o_sb = _sb(sbq, sbk, sbv, tri, sb_out_norm_w[l].reshape(1, -1), b, t)

    wo = w_out[l].astype(BF16)
    h1, u2 = _outproj(h, o_nsa.reshape(b * t, -1), o_sb.reshape(b * t, -1),
                      wo[:nsa_w], wo[nsa_w:], norm2_w[l].reshape(1, d))
    return _ffn(h1, u2, w_up[l].astype(BF16), conv_w[l], conv_b[l].reshape(1, -1),
                w_down[l].astype(BF16), final_norm_w.reshape(1, d), b, t)
```

```python
import functools
import math

import jax
import jax.numpy as jnp
import numpy as np
from jax import lax
from jax.experimental import pallas as pl
from jax.experimental.pallas import tpu as pltpu

F32 = jnp.float32
BF16 = jnp.bfloat16

HEAD_DIM = 64
NSA_HEADS = 8
SB_HEADS = 8
NSA_GROUPS = 2
NSA_REP = NSA_HEADS // NSA_GROUPS
CMP_BLOCK = 32
CMP_STRIDE = 16
SEL_BLOCK = 64
SEL_TOP = 8
WINDOW = 512
N_BUCKETS = 32
MAX_DISTANCE = 128
CONV_WIDTH = 3
EPS = 1e-6
NEG_INF = -1e30
FORCED_BONUS = 1e6

LANES = 128
TQ = 128
TK = 128
NTQ = 256
NTK = 256
ROW_TILE = 512
FFN_CHUNK = 256
HALO = 16
VMEM_LIMIT = 56 * 1024 * 1024
WIN_TILES = WINDOW // NTK + 1


def _dot(a, b):
    return jnp.dot(a, b, preferred_element_type=F32)


def _dot_nt(a, b):
    return lax.dot_general(a, b, (((1,), (1,)), ((), ())), preferred_element_type=F32)


def _params(*sem):
    return pltpu.CompilerParams(dimension_semantics=sem, vmem_limit_bytes=VMEM_LIMIT)


def _t5_bucket(dist):
    n = jnp.maximum(dist, 0)
    max_exact = N_BUCKETS // 2
    nf = jnp.maximum(n, 1).astype(F32)
    log_b = max_exact + (jnp.log(nf / max_exact) / math.log(MAX_DISTANCE / max_exact)
                         * (N_BUCKETS - max_exact)).astype(jnp.int32)
    log_b = jnp.minimum(log_b, N_BUCKETS - 1)
    return jnp.where(n < max_exact, n, log_b)


def _inproj_kernel(x_ref, nw_ref, w_ref, gb_ref, qn_ref, kcvc_ref, kv_ref, gates_ref,
                   sbq_ref, sbk_ref, sbv_ref, *, splits):
    x = x_ref[...]
    u = x * lax.rsqrt(jnp.mean(x * x, axis=-1, keepdims=True) + EPS) * nw_ref[...]
    u = u.astype(BF16)
    outs = (qn_ref, kcvc_ref, kv_ref, gates_ref, sbq_ref, sbk_ref, sbv_ref)
    start = 0
    for ref, size in zip(outs, splits):
        r = _dot(u, w_ref[:, start:start + size])
        if ref is gates_ref:
            r = jax.nn.sigmoid(r + gb_ref[...])
        ref[...] = r.astype(ref.dtype)
        start += size


def _inproj(x2, norm_w, w_all, gate_b_pad, splits):
    n, d = x2.shape
    dts = (BF16, F32, BF16, F32, BF16, BF16, BF16)
    return pl.pallas_call(
        functools.partial(_inproj_kernel, splits=splits),
        grid=(n // ROW_TILE,),
        in_specs=[
            pl.BlockSpec((ROW_TILE, d), lambda i: (i, 0)),
            pl.BlockSpec((1, d), lambda i: (0, 0)),
            pl.BlockSpec(w_all.shape, lambda i: (0, 0)),
            pl.BlockSpec((1, LANES), lambda i: (0, 0)),
        ],
        out_specs=[pl.BlockSpec((ROW_TILE, s), lambda i: (i, 0)) for s in splits],
        out_shape=[jax.ShapeDtypeStruct((n, s), dt) for s, dt in zip(splits, dts)],
        compiler_params=_params("parallel"),
        name="inproj",
    )(x2, norm_w, w_all, gate_b_pad)


def _gelu_tanh(x):
    return 0.5 * x * (1.0 + jnp.tanh(math.sqrt(2.0 / math.pi) * (x + 0.044715 * (x * x * x))))


def _compress_kernel(c_ref, pos_ref, w1_ref, w2_ref, kc_ref, vc_ref):
    n_chunk = c_ref.shape[2]
    row = lax.broadcasted_iota(jnp.int32, (n_chunk, LANES), 0)
    for kv, out_ref in ((0, kc_ref), (1, vc_ref)):
        acc = jnp.zeros((n_chunk, LANES), F32)
        for g in range(NSA_GROUPS):
            c = c_ref[0, 2 * kv + g]
            top = _dot((c + pos_ref[kv, 0:1, :]).astype(BF16), w1_ref[kv, 0])
            bot = _dot((c + pos_ref[kv, 1:2, :]).astype(BF16), w1_ref[kv, 1])
            hidden = top + pltpu.roll(bot, n_chunk - 1, axis=0)
            acc = acc + _dot(_gelu_tanh(hidden).astype(BF16), w2_ref[kv, g])
        out_ref[0] = jnp.where(row < n_chunk - 1, acc, 0.0).astype(out_ref.dtype)


def _compress(chunks, pos, w1, w2):
    b, _, n_chunk, width = chunks.shape
    hid = w1.shape[-1]
    out = jax.ShapeDtypeStruct((b, n_chunk, LANES), BF16)
    return pl.pallas_call(
        _compress_kernel,
        grid=(b,),
        in_specs=[
            pl.BlockSpec((1, 4, n_chunk, width), lambda i: (i, 0, 0, 0)),
            pl.BlockSpec((2, 2, width), lambda i: (0, 0, 0)),
            pl.BlockSpec((2, 2, width, hid), lambda i: (0, 0, 0, 0)),
            pl.BlockSpec((2, 2, hid, LANES), lambda i: (0, 0, 0, 0)),
        ],
        out_specs=[pl.BlockSpec((1, n_chunk, LANES), lambda i: (i, 0, 0))] * 2,
        out_shape=[out, out],
        compiler_params=_params("parallel"),
        name="compress",
    )(chunks, pos, w1, w2)


def _nsa_kernel(q_ref, ks_ref, vs_ref, kw_ref, vw_ref, kc_ref, vc_ref, gates_ref,
                biasc_ref, biass_ref, biasw_ref, ovl_ref, blockneg_ref, normw_ref,
                o_ref, s_ref, m_ref, acc_ref):
    i = pl.program_id(1)
    t0 = i * NTQ
    rows = NSA_REP * NTQ
    lane = lax.broadcasted_iota(jnp.int32, (rows, LANES), 1)
    v_half = lax.broadcasted_iota(jnp.int32, (NTK, LANES), 1) // HEAD_DIM
    gates = gates_ref[0]

    n_selp = ovl_ref.shape[0]
    blk = lax.broadcasted_iota(jnp.int32, (n_selp, NTQ), 0)
    cur = (t0 + lax.broadcasted_iota(jnp.int32, (n_selp, NTQ), 1)) // SEL_BLOCK
    causal_blk = blk <= cur
    forced = (blk == 0) | (blk == cur) | (blk == cur - 1)

    def attend(g, j_lo, q, key_tile, bias_tile, v_ref):
        m_ref[...] = jnp.full(m_ref.shape, NEG_INF, F32)

        def logits(j, c):
            off = pl.multiple_of(j * NTK, NTK)
            s = _dot_nt(q, key_tile(off)) + bias_tile(j)
            s_ref[:, pl.ds(off, NTK)] = s
            m_ref[...] = jnp.maximum(m_ref[...], jnp.maximum(s[:, :LANES], s[:, LANES:]))
            return c

        lax.fori_loop(j_lo, i + 1, logits, 0)
        m_ref[...] = jnp.broadcast_to(jnp.max(m_ref[...], axis=-1, keepdims=True), m_ref.shape)
        acc_ref[...] = jnp.zeros(acc_ref.shape, F32)

        def pv(j, c):
            off = pl.multiple_of(j * NTK, NTK)
            m = m_ref[...]
            p = jnp.exp(s_ref[:, pl.ds(off, NTK)] - jnp.concatenate([m, m], axis=1))
            v = v_ref[0, pl.ds(off, NTK), :]
            v = jnp.where(v_half == g, v, jnp.ones_like(v))
            acc_ref[...] += _dot(p.astype(BF16), v)
            return c

        lax.fori_loop(j_lo, i + 1, pv, 0)
        acc = acc_ref[...]
        return acc * (1.0 / pltpu.roll(acc, HEAD_DIM, axis=1))

    for g in range(NSA_GROUPS):
        qg = jnp.concatenate(
            [q_ref[0, :, (g * NSA_REP + r) * LANES:(g * NSA_REP + r + 1) * LANES]
             for r in range(NSA_REP)], axis=0)

        lc = _dot_nt(qg, kc_ref[0]) + biasc_ref[g, 0]
        mc = jnp.max(lc, axis=-1, keepdims=True)
        ec = jnp.exp(lc - mc)
        inv = jnp.where(mc > 0.5 * NEG_INF, 1.0 / jnp.sum(ec, axis=-1, keepdims=True), 0.0)
        pc = (ec * inv).astype(BF16)
        o_cmp = _dot(pc, vc_ref[0])

        imp4 = _dot_nt(ovl_ref[...], pc)
        imp = imp4[:, 0:NTQ]
        for r in range(1, NSA_REP):
            imp = imp + imp4[:, r * NTQ:(r + 1) * NTQ]
        score = jnp.where(causal_blk, imp + jnp.where(forced, FORCED_BONUS, 0.0), NEG_INF)
        rank = jnp.zeros((n_selp, NTQ), F32)
        for j in range(n_selp):
            other = score[j:j + 1, :]
            ahead = (other > score) | ((other == score) & (blk > j))
            rank = rank + jnp.where(ahead, 1.0, 0.0)
        unsel_t = jnp.where((rank < SEL_TOP) & causal_blk, 0.0, 1.0)
        unsel_t = jnp.concatenate([unsel_t, jnp.zeros((LANES - n_selp, NTQ), F32)], axis=0)
        unsel = unsel_t.T.astype(BF16)

        q_sel = jnp.concatenate([qg, jnp.concatenate([unsel] * NSA_REP, axis=0)], axis=1)
        o_sel = attend(
            g, 0, q_sel,
            lambda off: jnp.concatenate([ks_ref[0, pl.ds(off, NTK), :], blockneg_ref[pl.ds(off, NTK), :]], axis=1),
            lambda j: biass_ref[g, :, pl.ds(pl.multiple_of((2 - jnp.minimum(i - j, 2)) * NTK, NTK), NTK)],
            vs_ref)

        o_win = attend(
            g, jnp.maximum(i - (WIN_TILES - 1), 0), qg,
            lambda off: kw_ref[0, pl.ds(off, NTK), :],
            lambda j: biasw_ref[g, :, pl.ds(pl.multiple_of((j - i + WIN_TILES - 1) * NTK, NTK), NTK)],
            vw_ref)

        def gate(branch):
            return jnp.concatenate(
                [gates[:, (g * NSA_REP + r) * 3 + branch:(g * NSA_REP + r) * 3 + branch + 1]
                 for r in range(NSA_REP)], axis=0)

        o = gate(0) * o_cmp + gate(1) * o_sel + gate(2) * o_win
        mine = (lane // HEAD_DIM) == g
        ssq = jnp.sum(jnp.where(mine, o * o, 0.0), axis=-1, keepdims=True)
        y = o * lax.rsqrt(ssq * (1.0 / HEAD_DIM) + EPS)
        left = lax.broadcasted_iota(jnp.int32, (NTQ, LANES), 1) < HEAD_DIM
        for pair in range(NSA_REP // 2):
            even = y[(2 * pair) * NTQ:(2 * pair + 1) * NTQ]
            odd = y[(2 * pair + 1) * NTQ:(2 * pair + 2) * NTQ]
            if g == 0:
                odd = pltpu.roll(odd, HEAD_DIM, axis=1)
            else:
                even = pltpu.roll(even, HEAD_DIM, axis=1)
            cb = g * (NSA_REP // 2) + pair
            packed = jnp.where(left, even, odd) * normw_ref[:, cb * LANES:(cb + 1) * LANES]
            o_ref[0, :, cb * LANES:(cb + 1) * LANES] = packed.astype(o_ref.dtype)


def _nsa(qn, kv, kcmp, vcmp, gates, biasc, biass, biasw, ovl, blockneg, normw, b, t):
    nq = t // NTQ
    rows = NSA_REP * NTQ
    qn3 = qn.reshape(b, t, qn.shape[-1])
    kv3 = kv.reshape(b, t, kv.shape[-1])
    gates3 = gates.reshape(b, t, LANES)
    kv_spec = lambda c: pl.BlockSpec((1, t, LANES), lambda bi, qi, c=c: (bi, 0, c))
    const = lambda a: pl.BlockSpec(a.shape, lambda bi, qi: (0,) * a.ndim, pipeline_mode=pl.Buffered(1))
    return pl.pallas_call(
        _nsa_kernel,
        grid=(b, nq),
        in_specs=[
            pl.BlockSpec((1, NTQ, qn3.shape[-1]), lambda bi, qi: (bi, qi, 0)),
            kv_spec(0), kv_spec(1), kv_spec(2), kv_spec(3),
            pl.BlockSpec((1,) + kcmp.shape[1:], lambda bi, qi: (bi, 0, 0)),
            pl.BlockSpec((1,) + vcmp.shape[1:], lambda bi, qi: (bi, 0, 0)),
            pl.BlockSpec((1, NTQ, LANES), lambda bi, qi: (bi, qi, 0)),
            pl.BlockSpec((NSA_GROUPS, 1, rows, LANES), lambda bi, qi: (0, qi, 0, 0)),
            const(biass), const(biasw), const(ovl), const(blockneg), const(normw),
        ],
        out_specs=pl.BlockSpec((1, NTQ, NSA_HEADS * HEAD_DIM), lambda bi, qi: (bi, qi, 0)),
        out_shape=jax.ShapeDtypeStruct((b, t, NSA_HEADS * HEAD_DIM), BF16),
        scratch_shapes=[pltpu.VMEM((rows, t), F32), pltpu.VMEM((rows, LANES), F32),
                        pltpu.VMEM((rows, LANES), F32)],
        compiler_params=_params("parallel", "arbitrary"),
        name="nsa",
    )(qn3, kv3, kv3, kv3, kv3, kcmp, vcmp, gates3, biasc, biass, biasw, ovl, blockneg, normw)


def _sb_kernel(q_ref, k_ref, v_ref, tri_ref, normw_ref, o_ref, acc_ref, run_ref):
    i = pl.program_id(1)
    rows = 2 * TQ
    n_pairs = SB_HEADS // 2
    lane = lax.broadcasted_iota(jnp.int32, (rows, LANES), 1)
    tl = lax.broadcasted_iota(jnp.int32, (rows, LANES), 0) & (TQ - 1)
    strict = lane < tl
    mine = (lane // HEAD_DIM) == (lax.broadcasted_iota(jnp.int32, (rows, LANES), 0) // TQ)
    left = lax.broadcasted_iota(jnp.int32, (TQ, LANES), 1) < HEAD_DIM

    def tile(j, diag):
        off = pl.multiple_of(j * TK, TK)
        pairs = range(n_pairs)
        cols = [slice(p * LANES, (p + 1) * LANES) for p in pairs]
        zs = []
        for p in pairs:
            qp = jnp.concatenate(
                [q_ref[0, :, (2 * p + h) * LANES:(2 * p + h + 1) * LANES] for h in range(2)],
                axis=0)
            zs.append(_dot_nt(qp, k_ref[0, pl.ds(off, TK), cols[p]]))
        csums = []
        for p in pairs:
            z = zs[p]
            sp = jnp.maximum(z, 0.0) + jnp.log(1.0 + jnp.exp(-jnp.abs(z)))
            if diag:
                sp = jnp.where(strict, sp, 0.0)
            hi = sp.astype(BF16)
            lo = (sp - hi.astype(F32)).astype(BF16)
            csums.append(_dot(jnp.concatenate([hi, lo], axis=1), tri_ref[...]))
        for p in pairs:
            v = v_ref[0, pl.ds(off, TK), cols[p]]
            if diag:
                a = jnp.where(strict, jnp.exp(zs[p] - csums[p][:, :TK]), 0.0)
                acc_ref[p] = _dot(a.astype(BF16), v)
                run_ref[p] = csums[p][:, TK:]
            else:
                a = jnp.exp(zs[p] - csums[p][:, :TK] - run_ref[p])
                acc_ref[p] += _dot(a.astype(BF16), v)
                run_ref[p] += csums[p][:, TK:]

    tile(i, True)

    def body(step, carry):
        tile(i - 1 - step, False)
        return carry

    lax.fori_loop(0, i, body, 0)

    for pair in range(n_pairs):
        cols = slice(pair * LANES, (pair + 1) * LANES)
        acc = acc_ref[pair]
        ssq = jnp.sum(jnp.where(mine, acc * acc, 0.0), axis=-1, keepdims=True)
        y = acc * lax.rsqrt(ssq * (1.0 / HEAD_DIM) + EPS)
        packed = jnp.where(left, y[:TQ], y[TQ:]) * normw_ref[:, cols]
        o_ref[0, :, cols] = packed.astype(o_ref.dtype)


def _sb(q, k, v, tri, normw, b, t):
    q3 = q.reshape(b, t, q.shape[-1])
    k3 = k.reshape(b, t, k.shape[-1])
    v3 = v.reshape(b, t, v.shape[-1])
    width = SB_HEADS * HEAD_DIM
    return pl.pallas_call(
        _sb_kernel,
        grid=(b, t // TQ),
        in_specs=[
            pl.BlockSpec((1, TQ, q3.shape[-1]), lambda bi, qi: (bi, qi, 0)),
            pl.BlockSpec((1, t, width), lambda bi, qi: (bi, 0, 0)),
            pl.BlockSpec((1, t, width), lambda bi, qi: (bi, 0, 0)),
            pl.BlockSpec(tri.shape, lambda bi, qi: (0, 0)),
            pl.BlockSpec(normw.shape, lambda bi, qi: (0, 0)),
        ],
        out_specs=pl.BlockSpec((1, TQ, width), lambda bi, qi: (bi, qi, 0)),
        out_shape=jax.ShapeDtypeStruct((b, t, width), BF16),
        scratch_shapes=[pltpu.VMEM((SB_HEADS // 2, 2 * TQ, LANES), F32)] * 2,
        compiler_params=_params("parallel", "arbitrary"),
        name="stickbreak",
    )(q3, k3, v3, tri, normw)


def _outproj_kernel(x_ref, on_ref, os_ref, wn_ref, ws_ref, nw_ref, h_ref, u_ref):
    h = x_ref[...] + _dot(on_ref[...], wn_ref[...]) + _dot(os_ref[...], ws_ref[...])
    h_ref[...] = h
    u = h * lax.rsqrt(jnp.mean(h * h, axis=-1, keepdims=True) + EPS) * nw_ref[...]
    u_ref[...] = u.astype(u_ref.dtype)


def _outproj(x2, o_nsa, o_sb, w_n, w_s, norm_w):
    n, d = x2.shape
    row = lambda w: pl.BlockSpec((ROW_TILE, w), lambda i: (i, 0))
    full = lambda a: pl.BlockSpec(a.shape, lambda i: (0, 0))
    return pl.pallas_call(
        _outproj_kernel,
        grid=(n // ROW_TILE,),
        in_specs=[row(d), row(o_nsa.shape[-1]), row(o_sb.shape[-1]), full(w_n), full(w_s), full(norm_w)],
        out_specs=[row(d), row(d)],
        out_shape=[jax.ShapeDtypeStruct((n, d), F32), jax.ShapeDtypeStruct((n, d), BF16)],
        compiler_params=_params("parallel"),
        name="outproj",
    )(x2, o_nsa, o_sb, w_n, w_s, norm_w)


def _ffn_kernel(h_ref, u_ref, halo_ref, wup_ref, cw_ref, cb_ref, wdown_ref, fw_ref, o_ref, *, d_ff):
    i = pl.program_id(1)
    halo = jnp.where(i > 0, halo_ref[0], jnp.zeros_like(halo_ref[0]))
    ue = jnp.concatenate([halo, u_ref[0]], axis=0)
    tile = u_ref.shape[1]

    def conv(up, c0):
        w = cw_ref[:, c0:c0 + FFN_CHUNK]
        return (w[2:3] * up[HALO:] + w[1:2] * up[HALO - 1:HALO - 1 + tile]
                + w[0:1] * up[HALO - 2:HALO - 2 + tile] + cb_ref[:, c0:c0 + FFN_CHUNK])

    acc = h_ref[0]
    for c in range(d_ff // FFN_CHUNK):
        g0, v0 = c * FFN_CHUNK, d_ff + c * FFN_CHUNK
        gate = conv(_dot(ue, wup_ref[:, g0:g0 + FFN_CHUNK]), g0)
        val = conv(_dot(ue, wup_ref[:, v0:v0 + FFN_CHUNK]), v0)
        act = gate * (1.0 / (1.0 + jnp.exp(-gate))) * val
        acc = acc + _dot(act.astype(BF16), wdown_ref[g0:g0 + FFN_CHUNK, :])
    y = acc * lax.rsqrt(jnp.mean(acc * acc, axis=-1, keepdims=True) + EPS) * fw_ref[...]
    o_ref[0] = y.astype(o_ref.dtype)


def _ffn(h, u, w_up, conv_w, conv_b, w_down, final_w, b, t):
    d = h.shape[-1]
    d_ff = w_down.shape[0]
    h3 = h.reshape(b, t, d)
    u3 = u.reshape(b, t, d)
    per = ROW_TILE // HALO
    full = lambda a: pl.BlockSpec(a.shape, lambda bi, ti: (0, 0))
    return pl.pallas_call(
        functools.partial(_ffn_kernel, d_ff=d_ff),
        grid=(b, t // ROW_TILE),
        in_specs=[
            pl.BlockSpec((1, ROW_TILE, d), lambda bi, ti: (bi, ti, 0)),
            pl.BlockSpec((1, ROW_TILE, d), lambda bi, ti: (bi, ti, 0)),
            pl.BlockSpec((1, HALO, d), lambda bi, ti: (bi, jnp.maximum(ti * per - 1, 0), 0)),
            full(w_up), full(conv_w), full(conv_b), full(w_down), full(final_w),
        ],
        out_specs=pl.BlockSpec((1, ROW_TILE, d), lambda bi, ti: (bi, ti, 0)),
        out_shape=jax.ShapeDtypeStruct((b, t, d), F32),
        compiler_params=_params("parallel", "arbitrary"),
        name="convffn",
    )(h3, u3, u3, w_up, conv_w, conv_b, w_down, final_w)


def _pad_half(w, half):
    z = jnp.zeros_like(w)
    return jnp.concatenate([w, z] if half == 0 else [z, w], axis=-1)


def _input_weights(w):
    d = w.shape[0]
    nq, kvw = NSA_HEADS * HEAD_DIM, NSA_GROUPS * HEAD_DIM
    scale = HEAD_DIM ** -0.5
    o = 0
    q_n = w[:, o:o + nq] * scale; o += nq
    kc = w[:, o:o + kvw]; o += kvw
    vc = w[:, o:o + kvw]; o += kvw
    ks = w[:, o:o + kvw]; o += kvw
    vs = w[:, o:o + kvw]; o += kvw
    kw = w[:, o:o + kvw]; o += kvw
    vw = w[:, o:o + kvw]; o += kvw
    gl = w[:, o:o + NSA_HEADS * 3]; o += NSA_HEADS * 3
    sbw = SB_HEADS * HEAD_DIM
    q_s = w[:, o:o + sbw] * scale; o += sbw
    k_s = w[:, o:o + sbw]; o += sbw
    v_s = w[:, o:o + sbw]; o += sbw
    q_n = q_n.reshape(d, NSA_GROUPS, NSA_REP, HEAD_DIM)
    q_n = jnp.concatenate([_pad_half(q_n[:, g], g) for g in range(NSA_GROUPS)], axis=1).reshape(d, -1)
    q_s = q_s.reshape(d, SB_HEADS // 2, 2, HEAD_DIM)
    q_s = jnp.stack([_pad_half(q_s[:, :, h], h) for h in range(2)], axis=2).reshape(d, -1)
    gl = jnp.pad(gl, ((0, 0), (0, LANES - gl.shape[1])))
    pieces = (q_n, jnp.concatenate([kc, vc], axis=1), jnp.concatenate([ks, vs, kw, vw], axis=1),
              gl, q_s, k_s, v_s)
    return jnp.concatenate(pieces, axis=1).astype(BF16), tuple(p.shape[1] for p in pieces)


def _bias_tables(rel_bias, t):
    nq = t // NTQ
    by_dist = rel_bias[_t5_bucket(jnp.arange(t + WINDOW))]
    by_dist = by_dist.T.astype(F32)

    def by_signed(d, ok):
        return jnp.where(jnp.asarray(ok), by_dist[:, np.maximum(d, 0)], NEG_INF)

    def toeplitz(w, n_rows, n_cols):
        p = w.shape[-1]
        assert p >= n_rows + n_cols - 1
        reps = (1,) * (w.ndim - 1) + (n_rows + 1,)
        h = jnp.tile(w, reps)[..., :n_rows * (p + 1)].reshape(w.shape[:-1] + (n_rows, p + 1))
        return jnp.flip(h[..., :n_cols], axis=-1)

    def stacked(tb):
        return tb.reshape(NSA_GROUPS, NSA_REP * NTQ, tb.shape[-1])

    span = WINDOW + NTQ
    dw = np.arange(NTQ + span - 1) - (span - 1) + WINDOW
    biasw = stacked(toeplitz(by_signed(dw, (dw >= 0) & (dw < WINDOW)), NTQ, span))
    assert NTK > MAX_DISTANCE
    ds = np.arange(NTQ + 3 * NTK - 1) - (3 * NTK - 1) + 2 * NTK
    biass = stacked(toeplitz(by_signed(ds, ds >= 0), NTQ, 3 * NTK))
    n_cmp = (t - CMP_BLOCK) // CMP_STRIDE + 1
    n_a = t // CMP_STRIDE
    dc = (CMP_STRIDE * (np.arange(n_a + LANES - 1)[None, :] - (LANES - 1))
          + np.arange(CMP_STRIDE)[:, None] - (CMP_BLOCK - 1))
    bc = toeplitz(by_signed(dc, dc >= 0), n_a, LANES)
    bc = jnp.where(jnp.asarray(np.arange(LANES) < n_cmp), bc, NEG_INF)
    bc = bc.transpose(0, 2, 1, 3).reshape(NSA_GROUPS, NSA_REP, nq, NTQ, LANES)
    biasc = bc.transpose(0, 2, 1, 3, 4).reshape(NSA_GROUPS, nq, NSA_REP * NTQ, LANES)
    return biasc, biass, biasw


def _suffix_sum_matrix():
    r = np.arange(TK)[:, None]
    c = np.arange(TK)[None, :]
    m = np.concatenate([r >= c, np.ones((TK, TK), bool)], axis=1)
    return jnp.asarray(np.concatenate([m, m], axis=0), BF16)


def _selection_constants(t):
    n_cmp = (t - CMP_BLOCK) // CMP_STRIDE + 1
    n_sel = t // SEL_BLOCK
    n_selp = -(-n_sel // 8) * 8
    n = np.arange(LANES)[None, :]
    j = np.arange(n_selp)[:, None]
    ovl = ((CMP_STRIDE * n < SEL_BLOCK * (j + 1)) & (CMP_STRIDE * n + CMP_BLOCK > SEL_BLOCK * j)
           & (n < n_cmp) & (j < n_sel))
    own = (np.arange(t)[:, None] // SEL_BLOCK) == np.arange(LANES)[None, :]
    blockneg = jnp.where(jnp.asarray(own), NEG_INF, 0.0).astype(BF16)
    return jnp.asarray(ovl, BF16), blockneg


def kernel(x, norm1_w, w_in, cmp_pos_k, cmp_pos_v, cmp_k_w1, cmp_k_w2, cmp_v_w1, cmp_v_w2, gate_b,
           nsa_out_norm_w, sb_out_norm_w, w_out, norm2_w, w_up, conv_w, conv_b, w_down, rel_bias,
           final_norm_w):
    b, t, d = x.shape
    assert t % ROW_TILE == 0 and t % NTQ == 0 and t >= WINDOW + NTQ and (b * t) % ROW_TILE == 0
    assert (t - CMP_BLOCK) // CMP_STRIDE + 1 < LANES + 1 and t // CMP_STRIDE == LANES
    assert w_in.shape[0] == 1, "single-layer block: the closing norm is fused into the FFN kernel"
    half = CMP_STRIDE * HEAD_DIM
    nsa_w = NSA_HEADS * HEAD_DIM
    l = 0

    biasc, biass, biasw = _bias_tables(rel_bias, t)
    ovl, blockneg = _selection_constants(t)
    tri = _suffix_sum_matrix()

    h = x.reshape(b * t, d)
    w_all, splits = _input_weights(w_in[l])
    gb = jnp.pad(gate_b[l], (0, LANES - gate_b.shape[1])).reshape(1, LANES)
    qn, kcvc, kv, gates, sbq, sbk, sbv = _inproj(h, norm1_w[l].reshape(1, d), w_all, gb, splits)

    chunks = kcvc.reshape(b, t // CMP_STRIDE, CMP_STRIDE, 4, HEAD_DIM)
    chunks = chunks.transpose(0, 3, 1, 2, 4).reshape(b, 4, t // CMP_STRIDE, half)
    pos = jnp.stack([cmp_pos_k[l].reshape(2, half), cmp_pos_v[l].reshape(2, half)])
    w1 = jnp.stack([cmp_k_w1[l].reshape(2, half, -1), cmp_v_w1[l].reshape(2, half, -1)]).astype(BF16)
    w2 = jnp.stack([jnp.stack([_pad_half(w, g) for g in range(NSA_GROUPS)])
                    for w in (cmp_k_w2[l], cmp_v_w2[l])]).astype(BF16)
    kcmp, vcmp = _compress(chunks, pos, w1, w2)

    o_nsa = _nsa(qn, kv, kcmp, vcmp, gates, biasc, biass, biasw, ovl, blockneg,
                 nsa_out_norm_w[l].reshape(1, nsa_w), b, t)
    o_sb = _sb(sbq, sbk, sbv, tri, sb_out_norm_w[l].reshape(1, -1), b, t)

    wo = w_out[l].astype(BF16)
    h1, u2 = _outproj(h, o_nsa.reshape(b * t, -1), o_sb.reshape(b * t, -1),
                      wo[:nsa_w], wo[nsa_w:], norm2_w[l].reshape(1, d))
    return _ffn(h1, u2, w_up[l].astype(BF16), conv_w[l], conv_b[l].reshape(1, -1),
                w_down[l].astype(BF16), final_norm_w.reshape(1, d), b, t)
```

```python
import functools
import math

import jax
import jax.numpy as jnp
import numpy as np
from jax import lax
from jax.experimental import pallas as pl
from jax.experimental.pallas import tpu as pltpu

F32 = jnp.float32
BF16 = jnp.bfloat16

HEAD_DIM = 64
NSA_HEADS = 8
SB_HEADS = 8
NSA_GROUPS = 2
NSA_REP = NSA_HEADS // NSA_GROUPS
CMP_BLOCK = 32
CMP_STRIDE = 16
SEL_BLOCK = 64
SEL_TOP = 8
WINDOW = 512
N_BUCKETS = 32
MAX_DISTANCE = 128
CONV_WIDTH = 3
EPS = 1e-6
NEG_INF = -1e30
FORCED_BONUS = 1e6
TINY = 1e-30

LANES = 128
TQ = 256
TK = 256
NTQ = 256
NTK = 256
ROW_TILE = 512
FFN_CHUNK = 256
HALO = 16
VMEM_LIMIT = 56 * 1024 * 1024
WIN_TILES = WINDOW // NTK + 1


def _dot(a, b):
    return jnp.dot(a, b, preferred_element_type=F32)


def _dot_nt(a, b):
    return lax.dot_general(a, b, (((1,), (1,)), ((), ())), preferred_element_type=F32)


def _params(*sem):
    return pltpu.CompilerParams(dimension_semantics=sem, vmem_limit_bytes=VMEM_LIMIT)


def _t5_bucket(dist):
    n = jnp.maximum(dist, 0)
    max_exact = N_BUCKETS // 2
    nf = jnp.maximum(n, 1).astype(F32)
    log_b = max_exact + (jnp.log(nf / max_exact) / math.log(MAX_DISTANCE / max_exact)
                         * (N_BUCKETS - max_exact)).astype(jnp.int32)
    log_b = jnp.minimum(log_b, N_BUCKETS - 1)
    return jnp.where(n < max_exact, n, log_b)


def _inproj_kernel(x_ref, nw_ref, w_ref, gb_ref, qn_ref, kcvc_ref, kv_ref, gates_ref,
                   sbq_ref, sbk_ref, sbv_ref, *, splits):
    x = x_ref[...]
    u = x * lax.rsqrt(jnp.mean(x * x, axis=-1, keepdims=True) + EPS) * nw_ref[...]
    u = u.astype(BF16)
    outs = (qn_ref, kcvc_ref, kv_ref, gates_ref, sbq_ref, sbk_ref, sbv_ref)
    start = 0
    for ref, size in zip(outs, splits):
        r = _dot(u, w_ref[:, start:start + size])
        if ref is gates_ref:
            r = jax.nn.sigmoid(r + gb_ref[...])
        ref[...] = r.astype(ref.dtype)
        start += size


def _inproj(x2, norm_w, w_all, gate_b_pad, splits):
    n, d = x2.shape
    dts = (BF16, F32, BF16, F32, BF16, BF16, BF16)
    return pl.pallas_call(
        functools.partial(_inproj_kernel, splits=splits),
        grid=(n // ROW_TILE,),
        in_specs=[
            pl.BlockSpec((ROW_TILE, d), lambda i: (i, 0)),
            pl.BlockSpec((1, d), lambda i: (0, 0)),
            pl.BlockSpec(w_all.shape, lambda i: (0, 0)),
            pl.BlockSpec((1, LANES), lambda i: (0, 0)),
        ],
        out_specs=[pl.BlockSpec((ROW_TILE, s), lambda i: (i, 0)) for s in splits],
        out_shape=[jax.ShapeDtypeStruct((n, s), dt) for s, dt in zip(splits, dts)],
        compiler_params=_params("parallel"),
        name="inproj",
    )(x2, norm_w, w_all, gate_b_pad)


def _gelu_tanh(x):
    return 0.5 * x * (1.0 + jnp.tanh(math.sqrt(2.0 / math.pi) * (x + 0.044715 * (x * x * x))))


def _compress_kernel(c_ref, pos_ref, w1_ref, w2_ref, kc_ref, vc_ref):
    n_chunk = c_ref.shape[2]
    row = lax.broadcasted_iota(jnp.int32, (n_chunk, LANES), 0)
    for kv, out_ref in ((0, kc_ref), (1, vc_ref)):
        acc = jnp.zeros((n_chunk, LANES), F32)
        for g in range(NSA_GROUPS):
            c = c_ref[0, 2 * kv + g]
            top = _dot((c + pos_ref[kv, 0:1, :]).astype(BF16), w1_ref[kv, 0])
            bot = _dot((c + pos_ref[kv, 1:2, :]).astype(BF16), w1_ref[kv, 1])
            hidden = top + pltpu.roll(bot, n_chunk - 1, axis=0)
            acc = acc + _dot(_gelu_tanh(hidden).astype(BF16), w2_ref[kv, g])
        out_ref[0] = jnp.where(row < n_chunk - 1, acc, 0.0).astype(out_ref.dtype)


def _compress(chunks, pos, w1, w2):
    b, _, n_chunk, width = chunks.shape
    hid = w1.shape[-1]
    out = jax.ShapeDtypeStruct((b, n_chunk, LANES), BF16)
    return pl.pallas_call(
        _compress_kernel,
        grid=(b,),
        in_specs=[
            pl.BlockSpec((1, 4, n_chunk, width), lambda i: (i, 0, 0, 0)),
            pl.BlockSpec((2, 2, width), lambda i: (0, 0, 0)),
            pl.BlockSpec((2, 2, width, hid), lambda i: (0, 0, 0, 0)),
            pl.BlockSpec((2, 2, hid, LANES), lambda i: (0, 0, 0, 0)),
        ],
        out_specs=[pl.BlockSpec((1, n_chunk, LANES), lambda i: (i, 0, 0))] * 2,
        out_shape=[out, out],
        compiler_params=_params("parallel"),
        name="compress",
    )(chunks, pos, w1, w2)


def _nsa_kernel(q_ref, ks_ref, vs_ref, kw_ref, vw_ref, kc_ref, vc_ref, gates_ref,
                biasc_ref, biass_ref, biasw_ref, ovl_ref, blockneg_ref, normw_ref,
                o_ref, s_ref, m_ref, acc_ref):
    i = pl.program_id(1)
    t0 = i * NTQ
    rows = NSA_REP * NTQ
    lane = lax.broadcasted_iota(jnp.int32, (rows, LANES), 1)
    ones_v = jnp.ones((NTK, LANES), BF16)
    gates = gates_ref[0]

    n_selp = ovl_ref.shape[0] - 8
    blk = lax.broadcasted_iota(jnp.int32, (n_selp, NTQ), 0)
    cur = (t0 + lax.broadcasted_iota(jnp.int32, (n_selp, NTQ), 1)) // SEL_BLOCK
    causal_blk = blk <= cur
    forced = (blk == 0) | (blk == cur) | (blk == cur - 1)

    def attend(g, j_lo, q, key_tile, bias_tile, v_ref):
        m_ref[...] = jnp.full(m_ref.shape, NEG_INF, F32)

        def logits(j, c):
            off = pl.multiple_of(j * NTK, NTK)
            s = _dot_nt(q, key_tile(off)) + bias_tile(j)
            s_ref[:, pl.ds(off, NTK)] = s
            m_ref[...] = jnp.maximum(m_ref[...], jnp.maximum(s[:, :LANES], s[:, LANES:]))
            return c

        lax.fori_loop(j_lo, i + 1, logits, 0)
        m_ref[...] = jnp.broadcast_to(jnp.max(m_ref[...], axis=-1, keepdims=True), m_ref.shape)
        acc_ref[...] = jnp.zeros(acc_ref.shape, F32)

        def pv(j, c):
            off = pl.multiple_of(j * NTK, NTK)
            m = m_ref[...]
            p = jnp.exp(s_ref[:, pl.ds(off, NTK)] - jnp.concatenate([m, m], axis=1))
            v = jnp.concatenate([v_ref[0, pl.ds(off, NTK), :], ones_v], axis=1)
            acc_ref[...] += _dot(p.astype(BF16), v)
            return c

        lax.fori_loop(j_lo, i + 1, pv, 0)
        return acc_ref[:, :LANES] * (1.0 / acc_ref[:, LANES:])

    for g in range(NSA_GROUPS):
        qg = jnp.concatenate(
            [q_ref[0, :, (g * NSA_REP + r) * LANES:(g * NSA_REP + r + 1) * LANES]
             for r in range(NSA_REP)], axis=0)

        lc = _dot_nt(qg, kc_ref[0]) + biasc_ref[g, 0]
        mc = jnp.maximum(jnp.max(lc, axis=-1, keepdims=True), 0.1 * NEG_INF)
        ec = jnp.exp(lc - mc).astype(BF16)
        oc = _dot(ec, jnp.concatenate([vc_ref[0], ones_v[:LANES]], axis=1))
        o_cmp = oc[:, :LANES] * (1.0 / jnp.maximum(oc[:, LANES:], TINY))

        imp4 = _dot_nt(ovl_ref[...], ec)
        imp4 = imp4[:n_selp] * (1.0 / jnp.maximum(imp4[n_selp:n_selp + 1], TINY))
        imp = imp4[:, 0:NTQ]
        for r in range(1, NSA_REP):
            imp = imp + imp4[:, r * NTQ:(r + 1) * NTQ]
        score = jnp.where(causal_blk, imp + jnp.where(forced, FORCED_BONUS, 0.0), NEG_INF)
        rank = jnp.zeros((n_selp, NTQ), F32)
        for j in range(n_selp):
            other = score[j:j + 1, :]
            ahead = (other > score) | ((other == score) & (blk > j))
            rank = rank + jnp.where(ahead, 1.0, 0.0)
        unsel_t = jnp.where((rank < SEL_TOP) & causal_blk, 0.0, 1.0)
        unsel_t = jnp.concatenate([unsel_t, jnp.zeros((LANES - n_selp, NTQ), F32)], axis=0)
        unsel = unsel_t.T.astype(BF16)

        q_sel = jnp.concatenate([qg, jnp.concatenate([unsel] * NSA_REP, axis=0)], axis=1)
        o_sel = attend(
            g, 0, q_sel,
            lambda off: jnp.concatenate([ks_ref[0, pl.ds(off, NTK), :], blockneg_ref[pl.ds(off, NTK), :]], axis=1),
            lambda j: biass_ref[g, :, pl.ds(pl.multiple_of((2 - jnp.minimum(i - j, 2)) * NTK, NTK), NTK)],
            vs_ref)

        o_win = attend(
            g, jnp.maximum(i - (WIN_TILES - 1), 0), qg,
            lambda off: kw_ref[0, pl.ds(off, NTK), :],
            lambda j: biasw_ref[g, :, pl.ds(pl.multiple_of((j - i + WIN_TILES - 1) * NTK, NTK), NTK)],
            vw_ref)

        def gate(branch):
            return jnp.concatenate(
                [gates[:, (g * NSA_REP + r) * 3 + branch:(g * NSA_REP + r) * 3 + branch + 1]
                 for r in range(NSA_REP)], axis=0)

        o = gate(0) * o_cmp + gate(1) * o_sel + gate(2) * o_win
        mine = (lane // HEAD_DIM) == g
        ssq = jnp.sum(jnp.where(mine, o * o, 0.0), axis=-1, keepdims=True)
        y = o * lax.rsqrt(ssq * (1.0 / HEAD_DIM) + EPS)
        left = lax.broadcasted_iota(jnp.int32, (NTQ, LANES), 1) < HEAD_DIM
        for pair in range(NSA_REP // 2):
            even = y[(2 * pair) * NTQ:(2 * pair + 1) * NTQ]
            odd = y[(2 * pair + 1) * NTQ:(2 * pair + 2) * NTQ]
            if g == 0:
                odd = pltpu.roll(odd, HEAD_DIM, axis=1)
            else:
                even = pltpu.roll(even, HEAD_DIM, axis=1)
            cb = g * (NSA_REP // 2) + pair
            packed = jnp.where(left, even, odd) * normw_ref[:, cb * LANES:(cb + 1) * LANES]
            o_ref[0, :, cb * LANES:(cb + 1) * LANES] = packed.astype(o_ref.dtype)


def _nsa(qn, kv, kcmp, vcmp, gates, biasc, biass, biasw, ovl, blockneg, normw, b, t):
    nq = t // NTQ
    rows = NSA_REP * NTQ
    qn3 = qn.reshape(b, t, qn.shape[-1])
    kv3 = kv.reshape(b, t, kv.shape[-1])
    gates3 = gates.reshape(b, t, LANES)
    kv_spec = lambda c: pl.BlockSpec((1, t, LANES), lambda bi, qi, c=c: (bi, 0, c))
    const = lambda a: pl.BlockSpec(a.shape, lambda bi, qi: (0,) * a.ndim, pipeline_mode=pl.Buffered(1))
    return pl.pallas_call(
        _nsa_kernel,
        grid=(b, nq),
        in_specs=[
            pl.BlockSpec((1, NTQ, qn3.shape[-1]), lambda bi, qi: (bi, qi, 0)),
            kv_spec(0), kv_spec(1), kv_spec(2), kv_spec(3),
            pl.BlockSpec((1,) + kcmp.shape[1:], lambda bi, qi: (bi, 0, 0)),
            pl.BlockSpec((1,) + vcmp.shape[1:], lambda bi, qi: (bi, 0, 0)),
            pl.BlockSpec((1, NTQ, LANES), lambda bi, qi: (bi, qi, 0)),
            pl.BlockSpec((NSA_GROUPS, 1, rows, LANES), lambda bi, qi: (0, qi, 0, 0)),
            const(biass), const(biasw), const(ovl), const(blockneg), const(normw),
        ],
        out_specs=pl.BlockSpec((1, NTQ, NSA_HEADS * HEAD_DIM), lambda bi, qi: (bi, qi, 0)),
        out_shape=jax.ShapeDtypeStruct((b, t, NSA_HEADS * HEAD_DIM), BF16),
        scratch_shapes=[pltpu.VMEM((rows, t), F32), pltpu.VMEM((rows, LANES), F32),
                        pltpu.VMEM((rows, 2 * LANES), F32)],
        compiler_params=_params("parallel", "arbitrary"),
        name="nsa",
    )(qn3, kv3, kv3, kv3, kv3, kcmp, vcmp, gates3, biasc, biass, biasw, ovl, blockneg, normw)


def _sb_kernel(q_ref, k_ref, v_ref, tri_ref, normw_ref, o_ref, acc_ref, run_ref):
    i = pl.program_id(1)
    rows = 2 * TQ
    n_pairs = SB_HEADS // 2
    lane = lax.broadcasted_iota(jnp.int32, (rows, LANES), 1)
    row = lax.broadcasted_iota(jnp.int32, (rows, LANES), 0)
    strict = lax.broadcasted_iota(jnp.int32, (rows, TK), 1) < (
        lax.broadcasted_iota(jnp.int32, (rows, TK), 0) & (TQ - 1))
    mine = (lane // HEAD_DIM) == (row // TQ)
    left = lax.broadcasted_iota(jnp.int32, (TQ, LANES), 1) < HEAD_DIM

    def tile(j, diag):
        off = pl.multiple_of(j * TK, TK)
        pairs = range(n_pairs)
        cols = [slice(p * LANES, (p + 1) * LANES) for p in pairs]
        zs = []
        for p in pairs:
            qp = jnp.concatenate(
                [q_ref[0, :, (2 * p + h) * LANES:(2 * p + h + 1) * LANES] for h in range(2)],
                axis=0)
            zs.append(_dot_nt(qp, k_ref[0, pl.ds(off, TK), cols[p]]))
        csums = []
        for p in pairs:
            z = zs[p]
            sp = jnp.maximum(z, 0.0) + jnp.log(1.0 + jnp.exp(-jnp.abs(z)))
            if diag:
                sp = jnp.where(strict, sp, 0.0)
            hi = sp.astype(BF16)
            lo = (sp - hi.astype(F32)).astype(BF16)
            csums.append(_dot(jnp.concatenate([hi, lo], axis=1), tri_ref[...]))
        for p in pairs:
            v = v_ref[0, pl.ds(off, TK), cols[p]]
            tile_sum = jnp.broadcast_to(csums[p][:, 0:1], (rows, LANES))
            if diag:
                a = jnp.where(strict, jnp.exp(zs[p] - csums[p]), 0.0)
                acc_ref[p] = _dot(a.astype(BF16), v)
                run_ref[p] = tile_sum
            else:
                run = run_ref[p]
                a = jnp.exp(zs[p] - csums[p] - jnp.concatenate([run] * (TK // LANES), axis=1))
                acc_ref[p] += _dot(a.astype(BF16), v)
                run_ref[p] = run + tile_sum

    tile(i, True)

    def body(step, carry):
        tile(i - 1 - step, False)
        return carry

    lax.fori_loop(0, i, body, 0)

    for pair in range(n_pairs):
        cols = slice(pair * LANES, (pair + 1) * LANES)
        acc = acc_ref[pair]
        ssq = jnp.sum(jnp.where(mine, acc * acc, 0.0), axis=-1, keepdims=True)
        y = acc * lax.rsqrt(ssq * (1.0 / HEAD_DIM) + EPS)
        packed = jnp.where(left, y[:TQ], y[TQ:]) * normw_ref[:, cols]
        o_ref[0, :, cols] = packed.astype(o_ref.dtype)


def _sb(q, k, v, tri, normw, b, t):
    q3 = q.reshape(b, t, q.shape[-1])
    k3 = k.reshape(b, t, k.shape[-1])
    v3 = v.reshape(b, t, v.shape[-1])
    width = SB_HEADS * HEAD_DIM
    return pl.pallas_call(
        _sb_kernel,
        grid=(b, t // TQ),
        in_specs=[
            pl.BlockSpec((1, TQ, q3.shape[-1]), lambda bi, qi: (bi, qi, 0)),
            pl.BlockSpec((1, t, width), lambda bi, qi: (bi, 0, 0)),
            pl.BlockSpec((1, t, width), lambda bi, qi: (bi, 0, 0)),
            pl.BlockSpec(tri.shape, lambda bi, qi: (0, 0)),
            pl.BlockSpec(normw.shape, lambda bi, qi: (0, 0)),
        ],
        out_specs=pl.BlockSpec((1, TQ, width), lambda bi, qi: (bi, qi, 0)),
        out_shape=jax.ShapeDtypeStruct((b, t, width), BF16),
        scratch_shapes=[pltpu.VMEM((SB_HEADS // 2, 2 * TQ, LANES), F32)] * 2,
        compiler_params=_params("parallel", "arbitrary"),
        name="stickbreak",
    )(q3, k3, v3, tri, normw)


def _outproj_kernel(x_ref, on_ref, os_ref, wn_ref, ws_ref, nw_ref, h_ref, u_ref):
    h = x_ref[...] + _dot(on_ref[...], wn_ref[...]) + _dot(os_ref[...], ws_ref[...])
    h_ref[...] = h
    u = h * lax.rsqrt(jnp.mean(h * h, axis=-1, keepdims=True) + EPS) * nw_ref[...]
    u_ref[...] = u.astype(u_ref.dtype)


def _outproj(x2, o_nsa, o_sb, w_n, w_s, norm_w):
    n, d = x2.shape
    row = lambda w: pl.BlockSpec((ROW_TILE, w), lambda i: (i, 0))
    full = lambda a: pl.BlockSpec(a.shape, lambda i: (0, 0))
    return pl.pallas_call(
        _outproj_kernel,
        grid=(n // ROW_TILE,),
        in_specs=[row(d), row(o_nsa.shape[-1]), row(o_sb.shape[-1]), full(w_n), full(w_s), full(norm_w)],
        out_specs=[row(d), row(d)],
        out_shape=[jax.ShapeDtypeStruct((n, d), F32), jax.ShapeDtypeStruct((n, d), BF16)],
        compiler_params=_params("parallel"),
        name="outproj",
    )(x2, o_nsa, o_sb, w_n, w_s, norm_w)


def _ffn_kernel(h_ref, u_ref, halo_ref, wup_ref, cw_ref, cb_ref, wdown_ref, fw_ref, o_ref, up_ref, *, d_ff):
    i = pl.program_id(1)
    halo = jnp.where(i > 0, halo_ref[0], jnp.zeros_like(halo_ref[0]))
    ue = jnp.concatenate([halo, u_ref[0]], axis=0)
    tile = u_ref.shape[1]

    def conv(slot, c0):
        w = cw_ref[:, c0:c0 + FFN_CHUNK]
        taps = [up_ref[slot, HALO - k:HALO - k + tile] for k in range(CONV_WIDTH)]
        return (w[2:3] * taps[0] + w[1:2] * taps[1] + w[0:1] * taps[2]
                + cb_ref[:, c0:c0 + FFN_CHUNK])

    def up_project(c):
        slot = 2 * (c % 2)
        up_ref[slot] = _dot(ue, wup_ref[:, c * FFN_CHUNK:(c + 1) * FFN_CHUNK])
        up_ref[slot + 1] = _dot(ue, wup_ref[:, d_ff + c * FFN_CHUNK:d_ff + (c + 1) * FFN_CHUNK])

    n_chunks = d_ff // FFN_CHUNK
    acc = h_ref[0]
    up_project(0)
    for c in range(n_chunks):
        g0, v0 = c * FFN_CHUNK, d_ff + c * FFN_CHUNK
        slot = 2 * (c % 2)
        if c + 1 < n_chunks:
            up_project(c + 1)
        gate = conv(slot, g0)
        val = conv(slot + 1, v0)
        act = gate * (1.0 / (1.0 + jnp.exp(-gate))) * val
        acc = acc + _dot(act.astype(BF16), wdown_ref[g0:g0 + FFN_CHUNK, :])
    y = acc * lax.rsqrt(jnp.mean(acc * acc, axis=-1, keepdims=True) + EPS) * fw_ref[...]
    o_ref[0] = y.astype(o_ref.dtype)


def _ffn(h, u, w_up, conv_w, conv_b, w_down, final_w, b, t):
    d = h.shape[-1]
    d_ff = w_down.shape[0]
    h3 = h.reshape(b, t, d)
    u3 = u.reshape(b, t, d)
    per = ROW_TILE // HALO
    full = lambda a: pl.BlockSpec(a.shape, lambda bi, ti: (0, 0))
    return pl.pallas_call(
        functools.partial(_ffn_kernel, d_ff=d_ff),
        grid=(b, t // ROW_TILE),
        in_specs=[
            pl.BlockSpec((1, ROW_TILE, d), lambda bi, ti: (bi, ti, 0)),
            pl.BlockSpec((1, ROW_TILE, d), lambda bi, ti: (bi, ti, 0)),
            pl.BlockSpec((1, HALO, d), lambda bi, ti: (bi, jnp.maximum(ti * per - 1, 0), 0)),
            full(w_up), full(conv_w), full(conv_b), full(w_down), full(final_w),
        ],
        out_specs=pl.BlockSpec((1, ROW_TILE, d), lambda bi, ti: (bi, ti, 0)),
        out_shape=jax.ShapeDtypeStruct((b, t, d), F32),
        scratch_shapes=[pltpu.VMEM((4, HALO + ROW_TILE, FFN_CHUNK), F32)],
        compiler_params=_params("parallel", "arbitrary"),
        name="convffn",
    )(h3, u3, u3, w_up, conv_w, conv_b, w_down, final_w)


def _pad_half(w, half):
    z = jnp.zeros_like(w)
    return jnp.concatenate([w, z] if half == 0 else [z, w], axis=-1)


def _input_weights(w):
    d = w.shape[0]
    nq, kvw = NSA_HEADS * HEAD_DIM, NSA_GROUPS * HEAD_DIM
    scale = HEAD_DIM ** -0.5
    o = 0
    q_n = w[:, o:o + nq] * scale; o += nq
    kc = w[:, o:o + kvw]; o += kvw
    vc = w[:, o:o + kvw]; o += kvw
    ks = w[:, o:o + kvw]; o += kvw
    vs = w[:, o:o + kvw]; o += kvw
    kw = w[:, o:o + kvw]; o += kvw
    vw = w[:, o:o + kvw]; o += kvw
    gl = w[:, o:o + NSA_HEADS * 3]; o += NSA_HEADS * 3
    sbw = SB_HEADS * HEAD_DIM
    q_s = w[:, o:o + sbw] * scale; o += sbw
    k_s = w[:, o:o + sbw]; o += sbw
    v_s = w[:, o:o + sbw]; o += sbw
    q_n = q_n.reshape(d, NSA_GROUPS, NSA_REP, HEAD_DIM)
    q_n = jnp.concatenate([_pad_half(q_n[:, g], g) for g in range(NSA_GROUPS)], axis=1).reshape(d, -1)
    q_s = q_s.reshape(d, SB_HEADS // 2, 2, HEAD_DIM)
    q_s = jnp.stack([_pad_half(q_s[:, :, h], h) for h in range(2)], axis=2).reshape(d, -1)
    gl = jnp.pad(gl, ((0, 0), (0, LANES - gl.shape[1])))
    pieces = (q_n, jnp.concatenate([kc, vc], axis=1), jnp.concatenate([ks, vs, kw, vw], axis=1),
              gl, q_s, k_s, v_s)
    return jnp.concatenate(pieces, axis=1).astype(BF16), tuple(p.shape[1] for p in pieces)


def _bias_tables(rel_bias, t):
    nq = t // NTQ
    by_dist = rel_bias[_t5_bucket(jnp.arange(t + WINDOW))]
    by_dist = by_dist.T.astype(F32)

    def by_signed(d, ok):
        return jnp.where(jnp.asarray(ok), by_dist[:, np.maximum(d, 0)], NEG_INF)

    def toeplitz(w, n_rows, n_cols):
        p = w.shape[-1]
        assert p >= n_rows + n_cols - 1
        reps = (1,) * (w.ndim - 1) + (n_rows + 1,)
        h = jnp.tile(w, reps)[..., :n_rows * (p + 1)].reshape(w.shape[:-1] + (n_rows, p + 1))
        return jnp.flip(h[..., :n_cols], axis=-1)

    def stacked(tb):
        return tb.reshape(NSA_GROUPS, NSA_REP * NTQ, tb.shape[-1])

    span = WINDOW + NTQ
    dw = np.arange(NTQ + span - 1) - (span - 1) + WINDOW
    biasw = stacked(toeplitz(by_signed(dw, (dw >= 0) & (dw < WINDOW)), NTQ, span))
    assert NTK > MAX_DISTANCE
    ds = np.arange(NTQ + 3 * NTK - 1) - (3 * NTK - 1) + 2 * NTK
    biass = stacked(toeplitz(by_signed(ds, ds >= 0), NTQ, 3 * NTK))
    n_cmp = (t - CMP_BLOCK) // CMP_STRIDE + 1
    n_a = t // CMP_STRIDE
    dc = (CMP_STRIDE * (np.arange(n_a + LANES - 1)[None, :] - (LANES - 1))
          + np.arange(CMP_STRIDE)[:, None] - (CMP_BLOCK - 1))
    bc = toeplitz(by_signed(dc, dc >= 0), n_a, LANES)
    bc = jnp.where(jnp.asarray(np.arange(LANES) < n_cmp), bc, NEG_INF)
    bc = bc.transpose(0, 2, 1, 3).reshape(NSA_GROUPS, NSA_REP, nq, NTQ, LANES)
    biasc = bc.transpose(0, 2, 1, 3, 4).reshape(NSA_GROUPS, nq, NSA_REP * NTQ, LANES)
    return biasc, biass, biasw


def _suffix_sum_matrix():
    m = np.arange(TK)[:, None] >= np.arange(TK)[None, :]
    return jnp.asarray(np.concatenate([m, m], axis=0), BF16)


def _selection_constants(t):
    n_cmp = (t - CMP_BLOCK) // CMP_STRIDE + 1
    n_sel = t // SEL_BLOCK
    n_selp = -(-n_sel // 8) * 8
    n = np.arange(LANES)[None, :]
    j = np.arange(n_selp)[:, None]
    ovl = ((CMP_STRIDE * n < SEL_BLOCK * (j + 1)) & (CMP_STRIDE * n + CMP_BLOCK > SEL_BLOCK * j)
           & (n < n_cmp) & (j < n_sel))
    ovl = np.concatenate([ovl, np.ones((1, LANES), bool), np.zeros((7, LANES), bool)], axis=0)
    own = (np.arange(t)[:, None] // SEL_BLOCK) == np.arange(LANES)[None, :]
    blockneg = jnp.where(jnp.asarray(own), NEG_INF, 0.0).astype(BF16)
    return jnp.asarray(ovl, BF16), blockneg


def kernel(x, norm1_w, w_in, cmp_pos_k, cmp_pos_v, cmp_k_w1, cmp_k_w2, cmp_v_w1, cmp_v_w2, gate_b,
           nsa_out_norm_w, sb_out_norm_w, w_out, norm2_w, w_up, conv_w, conv_b, w_down, rel_bias,
           final_norm_w):
    b, t, d = x.shape
    assert t % ROW_TILE == 0 and t % NTQ == 0 and t >= WINDOW + NTQ and (b * t) % ROW_TILE == 0
    assert (t - CMP_BLOCK) // CMP_STRIDE + 1 < LANES + 1 and t // CMP_STRIDE == LANES
    assert w_in.shape[0] == 1, "single-layer block: the closing norm is fused into the FFN kernel"
    half = CMP_STRIDE * HEAD_DIM
    nsa_w = NSA_HEADS * HEAD_DIM
    l = 0

    biasc, biass, biasw = _bias_tables(rel_bias, t)
    ovl, blockneg = _selection_constants(t)
    tri = _suffix_sum_matrix()

    h = x.reshape(b * t, d)
    w_all, splits = _input_weights(w_in[l])
    gb = jnp.pad(gate_b[l], (0, LANES - gate_b.shape[1])).reshape(1, LANES)
    qn, kcvc, kv, gates, sbq, sbk, sbv = _inproj(h, norm1_w[l].reshape(1, d), w_all, gb, splits)

    chunks = kcvc.reshape(b, t // CMP_STRIDE, CMP_STRIDE, 4, HEAD_DIM)
    chunks = chunks.transpose(0, 3, 1, 2, 4).reshape(b, 4, t // CMP_STRIDE, half)
    pos = jnp.stack([cmp_pos_k[l].reshape(2, half), cmp_pos_v[l].reshape(2, half)])
    w1 = jnp.stack([cmp_k_w1[l].reshape(2, half, -1), cmp_v_w1[l].reshape(2, half, -1)]).astype(BF16)
    w2 = jnp.stack([jnp.stack([_pad_half(w, g) for g in range(NSA_GROUPS)])
                    for w in (cmp_k_w2[l], cmp_v_w2[l])]).astype(BF16)
    kcmp, vcmp = _compress(chunks, pos, w1, w2)

    o_nsa = _nsa(qn, kv, kcmp, vcmp, gates, biasc, biass, biasw, ovl, blockneg,
                 nsa_out_norm_w[l].reshape(1, nsa_w), b, t)
    o_sb = _sb(sbq, sbk, sbv, tri, sb_out_norm_w[l].reshape(1, -1), b, t)

    wo = w_out[l].astype(BF16)
    h1, u2 = _outproj(h, o_nsa.reshape(b * t, -1), o_sb.reshape(b * t, -1),
                      wo[:nsa_w], wo[nsa_w:], norm2_w[l].reshape(1, d))
    return _ffn(h1, u2, w_up[l].astype(BF16), conv_w[l], conv_b[l].reshape(1, -1),
                w_down[l].astype(BF16), final_norm_w.reshape(1, d), b, t)
```

```python
import functools
import math

import jax
import jax.numpy as jnp
import numpy as np
from jax import lax
from jax.experimental import pallas as pl
from jax.experimental.pallas import tpu as pltpu

F32 = jnp.float32
BF16 = jnp.bfloat16

HEAD_DIM = 64
NSA_HEADS = 8
SB_HEADS = 8
NSA_GROUPS = 2
NSA_REP = NSA_HEADS // NSA_GROUPS
CMP_BLOCK = 32
CMP_STRIDE = 16
SEL_BLOCK = 64
SEL_TOP = 8
WINDOW = 512
N_BUCKETS = 32
MAX_DISTANCE = 128
CONV_WIDTH = 3
EPS = 1e-6
NEG_INF = -1e30
FORCED_BONUS = 1e6
TINY = 1e-30

LANES = 128
TQ = 256
TK = 256
NTQ = 256
NTK = 256
ROW_TILE = 512
FFN_CHUNK = 256
HALO = 16
VMEM_LIMIT = 56 * 1024 * 1024
WIN_TILES = WINDOW // NTK + 1


def _dot(a, b):
    return jnp.dot(a, b, preferred_element_type=F32)


def _dot_nt(a, b):
    return lax.dot_general(a, b, (((1,), (1,)), ((), ())), preferred_element_type=F32)


def _params(*sem):
    return pltpu.CompilerParams(dimension_semantics=sem, vmem_limit_bytes=VMEM_LIMIT)


def _t5_bucket_np(dist):
    n = np.maximum(dist, 0)
    max_exact = N_BUCKETS // 2
    nf = np.maximum(n, 1).astype(np.float32)
    log_b = max_exact + (np.log(nf / np.float32(max_exact)) / np.float32(math.log(MAX_DISTANCE / max_exact))
                         * np.float32(N_BUCKETS - max_exact)).astype(np.int32)
    log_b = np.minimum(log_b, N_BUCKETS - 1)
    return np.where(n < max_exact, n, log_b)


def _inproj_kernel(x_ref, nw_ref, w_ref, gb_ref, qn_ref, kcvc_ref, kv_ref, gates_ref,
                   sbq_ref, sbk_ref, sbv_ref, *, splits):
    x = x_ref[...]
    u = x * lax.rsqrt(jnp.mean(x * x, axis=-1, keepdims=True) + EPS) * nw_ref[...]
    u = u.astype(BF16)
    outs = (qn_ref, kcvc_ref, kv_ref, gates_ref, sbq_ref, sbk_ref, sbv_ref)
    start = 0
    for ref, size in zip(outs, splits):
        r = _dot(u, w_ref[:, start:start + size])
        if ref is gates_ref:
            r = jax.nn.sigmoid(r + gb_ref[...])
        ref[...] = r.astype(ref.dtype)
        start += size


def _inproj(x2, norm_w, w_all, gate_b_pad, splits):
    n, d = x2.shape
    dts = (BF16, F32, BF16, F32, BF16, BF16, BF16)
    return pl.pallas_call(
        functools.partial(_inproj_kernel, splits=splits),
        grid=(n // ROW_TILE,),
        in_specs=[
            pl.BlockSpec((ROW_TILE, d), lambda i: (i, 0)),
            pl.BlockSpec((1, d), lambda i: (0, 0)),
            pl.BlockSpec(w_all.shape, lambda i: (0, 0)),
            pl.BlockSpec((1, LANES), lambda i: (0, 0)),
        ],
        out_specs=[pl.BlockSpec((ROW_TILE, s), lambda i: (i, 0)) for s in splits],
        out_shape=[jax.ShapeDtypeStruct((n, s), dt) for s, dt in zip(splits, dts)],
        compiler_params=_params("parallel"),
        name="inproj",
    )(x2, norm_w, w_all, gate_b_pad)


def _gelu_tanh(x):
    return 0.5 * x * (1.0 + jnp.tanh(math.sqrt(2.0 / math.pi) * (x + 0.044715 * (x * x * x))))


def _compress_kernel(c_ref, pos_ref, w1_ref, w2_ref, kc_ref, vc_ref):
    n_chunk = c_ref.shape[2]
    row = lax.broadcasted_iota(jnp.int32, (n_chunk, LANES), 0)
    for kv, out_ref in ((0, kc_ref), (1, vc_ref)):
        acc = jnp.zeros((n_chunk, LANES), F32)
        for g in range(NSA_GROUPS):
            c = c_ref[0, 2 * kv + g]
            top = _dot((c + pos_ref[kv, 0:1, :]).astype(BF16), w1_ref[kv, 0])
            bot = _dot((c + pos_ref[kv, 1:2, :]).astype(BF16), w1_ref[kv, 1])
            hidden = top + pltpu.roll(bot, n_chunk - 1, axis=0)
            acc = acc + _dot(_gelu_tanh(hidden).astype(BF16), w2_ref[kv, g])
        out_ref[0] = jnp.where(row < n_chunk - 1, acc, 0.0).astype(out_ref.dtype)


def _compress(chunks, pos, w1, w2):
    b, _, n_chunk, width = chunks.shape
    hid = w1.shape[-1]
    out = jax.ShapeDtypeStruct((b, n_chunk, LANES), BF16)
    return pl.pallas_call(
        _compress_kernel,
        grid=(b,),
        in_specs=[
            pl.BlockSpec((1, 4, n_chunk, width), lambda i: (i, 0, 0, 0)),
            pl.BlockSpec((2, 2, width), lambda i: (0, 0, 0)),
            pl.BlockSpec((2, 2, width, hid), lambda i: (0, 0, 0, 0)),
            pl.BlockSpec((2, 2, hid, LANES), lambda i: (0, 0, 0, 0)),
        ],
        out_specs=[pl.BlockSpec((1, n_chunk, LANES), lambda i: (i, 0, 0))] * 2,
        out_shape=[out, out],
        compiler_params=_params("parallel"),
        name="compress",
    )(chunks, pos, w1, w2)


def _nsa_kernel(q_ref, ks_ref, vs_ref, kw_ref, vw_ref, kc_ref, vc_ref, gates_ref,
                biasc_ref, biass_ref, biasw_ref, ovl_ref, blockneg_ref, normw_ref,
                o_ref, s_ref, m_ref, acc_sel_ref, acc_win_ref, qsel_ref, ocmp_ref):
    i = pl.program_id(1)
    t0 = i * NTQ
    rows = NSA_REP * NTQ
    groups = range(NSA_GROUPS)
    lane = lax.broadcasted_iota(jnp.int32, (rows, LANES), 1)
    ones_v = jnp.ones((NTK, LANES), BF16)
    gates = gates_ref[0]

    n_selp = ovl_ref.shape[0] - 8
    blk = lax.broadcasted_iota(jnp.int32, (n_selp, NTQ), 0)
    cur = (t0 + lax.broadcasted_iota(jnp.int32, (n_selp, NTQ), 1)) // SEL_BLOCK
    causal_blk = blk <= cur
    forced = (blk == 0) | (blk == cur) | (blk == cur - 1)

    def attend(j_lo, q_width, key_tile, bias_tile, v_ref, acc_ref):
        m_ref[...] = jnp.full(m_ref.shape, NEG_INF, F32)

        def logits(j, c):
            off = pl.multiple_of(j * NTK, NTK)
            k = key_tile(off)
            ss = [_dot_nt(qsel_ref[g, :, :q_width], k) for g in groups]
            for g in groups:
                s = ss[g] + bias_tile(g, j)
                s_ref[g, :, pl.ds(off, NTK)] = s
                m_ref[g] = jnp.maximum(m_ref[g], jnp.maximum(s[:, :LANES], s[:, LANES:]))
            return c

        lax.fori_loop(j_lo, i + 1, logits, 0)
        for g in groups:
            m_ref[g] = jnp.broadcast_to(jnp.max(m_ref[g], axis=-1, keepdims=True), (rows, LANES))
        acc_ref[...] = jnp.zeros(acc_ref.shape, F32)

        def pv(j, c):
            off = pl.multiple_of(j * NTK, NTK)
            v = jnp.concatenate([v_ref[0, pl.ds(off, NTK), :], ones_v], axis=1)
            ps = []
            for g in groups:
                m = m_ref[g]
                ps.append(jnp.exp(s_ref[g, :, pl.ds(off, NTK)] - jnp.concatenate([m, m], axis=1)).astype(BF16))
            for g in groups:
                acc_ref[g] += _dot(ps[g], v)
            return c

        lax.fori_loop(j_lo, i + 1, pv, 0)

    for g in groups:
        qg = jnp.concatenate(
            [q_ref[0, :, (g * NSA_REP + r) * LANES:(g * NSA_REP + r + 1) * LANES]
             for r in range(NSA_REP)], axis=0)

        lc = _dot_nt(qg, kc_ref[0]) + biasc_ref[g, 0]
        mc = jnp.maximum(jnp.max(lc, axis=-1, keepdims=True), 0.1 * NEG_INF)
        ec = jnp.exp(lc - mc).astype(BF16)
        oc = _dot(ec, jnp.concatenate([vc_ref[0], ones_v[:LANES]], axis=1))
        ocmp_ref[g] = oc[:, :LANES] * (1.0 / jnp.maximum(oc[:, LANES:], TINY))

        imp4 = _dot_nt(ovl_ref[...], ec)
        imp4 = imp4[:n_selp] * (1.0 / jnp.maximum(imp4[n_selp:n_selp + 1], TINY))
        imp = imp4[:, 0:NTQ]
        for r in range(1, NSA_REP):
            imp = imp + imp4[:, r * NTQ:(r + 1) * NTQ]
        score = jnp.where(causal_blk, imp + jnp.where(forced, FORCED_BONUS, 0.0), NEG_INF)
        rank = jnp.zeros((n_selp, NTQ), F32)
        for j in range(n_selp):
            other = score[j:j + 1, :]
            ahead = (other > score) | ((other == score) & (blk > j))
            rank = rank + jnp.where(ahead, 1.0, 0.0)
        unsel_t = jnp.where((rank < SEL_TOP) & causal_blk, 0.0, 1.0)
        unsel_t = jnp.concatenate([unsel_t, jnp.zeros((LANES - n_selp, NTQ), F32)], axis=0)
        unsel = unsel_t.T.astype(BF16)

        qsel_ref[g] = jnp.concatenate([qg, jnp.concatenate([unsel] * NSA_REP, axis=0)], axis=1)

    attend(
        0, 2 * LANES,
        lambda off: jnp.concatenate([ks_ref[0, pl.ds(off, NTK), :], blockneg_ref[pl.ds(off, NTK), :]], axis=1),
        lambda g, j: biass_ref[g, :, pl.ds(pl.multiple_of((2 - jnp.minimum(i - j, 2)) * NTK, NTK), NTK)],
        vs_ref, acc_sel_ref)

    attend(
        jnp.maximum(i - (WIN_TILES - 1), 0), LANES,
        lambda off: kw_ref[0, pl.ds(off, NTK), :],
        lambda g, j: biasw_ref[g, :, pl.ds(pl.multiple_of((j - i + WIN_TILES - 1) * NTK, NTK), NTK)],
        vw_ref, acc_win_ref)

    for g in groups:
        def gate(branch):
            return jnp.concatenate(
                [gates[:, (g * NSA_REP + r) * 3 + branch:(g * NSA_REP + r) * 3 + branch + 1]
                 for r in range(NSA_REP)], axis=0)

        o_sel = acc_sel_ref[g, :, :LANES] * (1.0 / acc_sel_ref[g, :, LANES:])
        o_win = acc_win_ref[g, :, :LANES] * (1.0 / acc_win_ref[g, :, LANES:])
        o = gate(0) * ocmp_ref[g] + gate(1) * o_sel + gate(2) * o_win
        mine = (lane // HEAD_DIM) == g
        ssq = jnp.sum(jnp.where(mine, o * o, 0.0), axis=-1, keepdims=True)
        y = o * lax.rsqrt(ssq * (1.0 / HEAD_DIM) + EPS)
        left = lax.broadcasted_iota(jnp.int32, (NTQ, LANES), 1) < HEAD_DIM
        for pair in range(NSA_REP // 2):
            even = y[(2 * pair) * NTQ:(2 * pair + 1) * NTQ]
            odd = y[(2 * pair + 1) * NTQ:(2 * pair + 2) * NTQ]
            if g == 0:
                odd = pltpu.roll(odd, HEAD_DIM, axis=1)
            else:
                even = pltpu.roll(even, HEAD_DIM, axis=1)
            cb = g * (NSA_REP // 2) + pair
            packed = jnp.where(left, even, odd) * normw_ref[:, cb * LANES:(cb + 1) * LANES]
            o_ref[0, :, cb * LANES:(cb + 1) * LANES] = packed.astype(o_ref.dtype)


def _nsa(qn, kv, kcmp, vcmp, gates, biasc, biass, biasw, ovl, blockneg, normw, b, t):
    nq = t // NTQ
    rows = NSA_REP * NTQ
    qn3 = qn.reshape(b, t, qn.shape[-1])
    kv3 = kv.reshape(b, t, kv.shape[-1])
    gates3 = gates.reshape(b, t, LANES)
    kv_spec = lambda c: pl.BlockSpec((1, t, LANES), lambda bi, qi, c=c: (bi, 0, c))
    const = lambda a: pl.BlockSpec(a.shape, lambda bi, qi: (0,) * a.ndim, pipeline_mode=pl.Buffered(1))
    return pl.pallas_call(
        _nsa_kernel,
        grid=(b, nq),
        in_specs=[
            pl.BlockSpec((1, NTQ, qn3.shape[-1]), lambda bi, qi: (bi, qi, 0)),
            kv_spec(0), kv_spec(1), kv_spec(2), kv_spec(3),
            pl.BlockSpec((1,) + kcmp.shape[1:], lambda bi, qi: (bi, 0, 0)),
            pl.BlockSpec((1,) + vcmp.shape[1:], lambda bi, qi: (bi, 0, 0)),
            pl.BlockSpec((1, NTQ, LANES), lambda bi, qi: (bi, qi, 0)),
            pl.BlockSpec((NSA_GROUPS, 1, rows, LANES), lambda bi, qi: (0, qi, 0, 0)),
            const(biass), const(biasw), const(ovl), const(blockneg), const(normw),
        ],
        out_specs=pl.BlockSpec((1, NTQ, NSA_HEADS * HEAD_DIM), lambda bi, qi: (bi, qi, 0)),
        out_shape=jax.ShapeDtypeStruct((b, t, NSA_HEADS * HEAD_DIM), BF16),
        scratch_shapes=[
            pltpu.VMEM((NSA_GROUPS, rows, t), F32),
            pltpu.VMEM((NSA_GROUPS, rows, LANES), F32),
            pltpu.VMEM((NSA_GROUPS, rows, 2 * LANES), F32),
            pltpu.VMEM((NSA_GROUPS, rows, 2 * LANES), F32),
            pltpu.VMEM((NSA_GROUPS, rows, 2 * LANES), BF16),
            pltpu.VMEM((NSA_GROUPS, rows, LANES), F32),
        ],
        compiler_params=_params("parallel", "arbitrary"),
        name="nsa",
    )(qn3, kv3, kv3, kv3, kv3, kcmp, vcmp, gates3, biasc, biass, biasw, ovl, blockneg, normw)


def _sb_kernel(q_ref, k_ref, v_ref, tri_ref, normw_ref, o_ref, acc_ref, run_ref):
    i = pl.program_id(1)
    rows = 2 * TQ
    n_pairs = SB_HEADS // 2
    lane = lax.broadcasted_iota(jnp.int32, (rows, LANES), 1)
    row = lax.broadcasted_iota(jnp.int32, (rows, LANES), 0)
    strict = lax.broadcasted_iota(jnp.int32, (rows, TK), 1) < (
        lax.broadcasted_iota(jnp.int32, (rows, TK), 0) & (TQ - 1))
    mine = (lane // HEAD_DIM) == (row // TQ)
    left = lax.broadcasted_iota(jnp.int32, (TQ, LANES), 1) < HEAD_DIM

    def tile(j, diag):
        off = pl.multiple_of(j * TK, TK)
        pairs = range(n_pairs)
        cols = [slice(p * LANES, (p + 1) * LANES) for p in pairs]
        zs = []
        for p in pairs:
            q2 = q_ref[0, :, cols[p]]
            zero = jnp.zeros_like(q2)
            qp = jnp.concatenate([jnp.where(left, q2, zero), jnp.where(left, zero, q2)],
                                 axis=0)
            zs.append(_dot_nt(qp, k_ref[0, pl.ds(off, TK), cols[p]]))
        csums = []
        for p in pairs:
            z = zs[p]
            sp = jnp.maximum(z, 0.0) + jnp.log(1.0 + jnp.exp(-jnp.abs(z)))
            if diag:
                sp = jnp.where(strict, sp, 0.0)
            hi = sp.astype(BF16)
            lo = (sp - hi.astype(F32)).astype(BF16)
            csums.append(_dot(jnp.concatenate([hi, lo], axis=1), tri_ref[...]))
        for p in pairs:
            v = v_ref[0, pl.ds(off, TK), cols[p]]
            tile_sum = jnp.broadcast_to(csums[p][:, 0:1], (rows, LANES))
            if diag:
                a = jnp.where(strict, jnp.exp(zs[p] - csums[p]), 0.0)
                acc_ref[p] = _dot(a.astype(BF16), v)
                run_ref[p] = tile_sum
            else:
                run = run_ref[p]
                a = jnp.exp(zs[p] - csums[p] - jnp.concatenate([run] * (TK // LANES), axis=1))
                acc_ref[p] += _dot(a.astype(BF16), v)
                run_ref[p] = run + tile_sum

    tile(i, True)

    def body(step, carry):
        tile(i - 1 - step, False)
        return carry

    lax.fori_loop(0, i, body, 0)

    for pair in range(n_pairs):
        cols = slice(pair * LANES, (pair + 1) * LANES)
        acc = acc_ref[pair]
        ssq = jnp.sum(jnp.where(mine, acc * acc, 0.0), axis=-1, keepdims=True)
        y = acc * lax.rsqrt(ssq * (1.0 / HEAD_DIM) + EPS)
        packed = jnp.where(left, y[:TQ], y[TQ:]) * normw_ref[:, cols]
        o_ref[0, :, cols] = packed.astype(o_ref.dtype)


def _sb(q, k, v, tri, normw, b, t):
    q3 = q.reshape(b, t, q.shape[-1])
    k3 = k.reshape(b, t, k.shape[-1])
    v3 = v.reshape(b, t, v.shape[-1])
    width = SB_HEADS * HEAD_DIM
    return pl.pallas_call(
        _sb_kernel,
        grid=(b, t // TQ),
        in_specs=[
            pl.BlockSpec((1, TQ, q3.shape[-1]), lambda bi, qi: (bi, qi, 0)),
            pl.BlockSpec((1, t, width), lambda bi, qi: (bi, 0, 0)),
            pl.BlockSpec((1, t, width), lambda bi, qi: (bi, 0, 0)),
            pl.BlockSpec(tri.shape, lambda bi, qi: (0, 0)),
            pl.BlockSpec(normw.shape, lambda bi, qi: (0, 0)),
        ],
        out_specs=pl.BlockSpec((1, TQ, width), lambda bi, qi: (bi, qi, 0)),
        out_shape=jax.ShapeDtypeStruct((b, t, width), BF16),
        scratch_shapes=[pltpu.VMEM((SB_HEADS // 2, 2 * TQ, LANES), F32)] * 2,
        compiler_params=_params("parallel", "arbitrary"),
        name="stickbreak",
    )(q3, k3, v3, tri, normw)


def _outproj_kernel(x_ref, on_ref, os_ref, wn_ref, ws_ref, nw_ref, h_ref, u_ref):
    h = x_ref[...] + _dot(on_ref[...], wn_ref[...]) + _dot(os_ref[...], ws_ref[...])
    h_ref[...] = h
    u = h * lax.rsqrt(jnp.mean(h * h, axis=-1, keepdims=True) + EPS) * nw_ref[...]
    u_ref[...] = u.astype(u_ref.dtype)


def _outproj(x2, o_nsa, o_sb, w_n, w_s, norm_w):
    n, d = x2.shape
    row = lambda w: pl.BlockSpec((ROW_TILE, w), lambda i: (i, 0))
    full = lambda a: pl.BlockSpec(a.shape, lambda i: (0, 0))
    return pl.pallas_call(
        _outproj_kernel,
        grid=(n // ROW_TILE,),
        in_specs=[row(d), row(o_nsa.shape[-1]), row(o_sb.shape[-1]), full(w_n), full(w_s), full(norm_w)],
        out_specs=[row(d), row(d)],
        out_shape=[jax.ShapeDtypeStruct((n, d), F32), jax.ShapeDtypeStruct((n, d), BF16)],
        compiler_params=_params("parallel"),
        name="outproj",
    )(x2, o_nsa, o_sb, w_n, w_s, norm_w)


def _ffn_kernel(h_ref, u_ref, halo_ref, wup_ref, cw_ref, cb_ref, wdown_ref, fw_ref, o_ref, up_ref, act_ref,
                *, d_ff):
    i = pl.program_id(1)
    halo = jnp.where(i > 0, halo_ref[0], jnp.zeros_like(halo_ref[0]))
    ue = jnp.concatenate([halo, u_ref[0]], axis=0)
    tile = u_ref.shape[1]

    def conv(slot, c0):
        w = cw_ref[:, c0:c0 + FFN_CHUNK]
        taps = [up_ref[slot, HALO - k:HALO - k + tile] for k in range(CONV_WIDTH)]
        return (w[2:3] * taps[0] + w[1:2] * taps[1] + w[0:1] * taps[2]
                + cb_ref[:, c0:c0 + FFN_CHUNK])

    def up_project(c):
        slot = 2 * (c % 2)
        up_ref[slot] = _dot(ue, wup_ref[:, c * FFN_CHUNK:(c + 1) * FFN_CHUNK])
        up_ref[slot + 1] = _dot(ue, wup_ref[:, d_ff + c * FFN_CHUNK:d_ff + (c + 1) * FFN_CHUNK])

    n_chunks = d_ff // FFN_CHUNK
    up_project(0)
    for c in range(n_chunks):
        g0, v0 = c * FFN_CHUNK, d_ff + c * FFN_CHUNK
        slot = 2 * (c % 2)
        if c + 1 < n_chunks:
            up_project(c + 1)
        gate = conv(slot, g0)
        val = conv(slot + 1, v0)
        act = gate * (1.0 / (1.0 + jnp.exp(-gate))) * val
        act_ref[:, g0:g0 + FFN_CHUNK] = act.astype(BF16)
    acc = h_ref[0] + _dot(act_ref[...], wdown_ref[...])
    y = acc * lax.rsqrt(jnp.mean(acc * acc, axis=-1, keepdims=True) + EPS) * fw_ref[...]
    o_ref[0] = y.astype(o_ref.dtype)


def _ffn(h, u, w_up, conv_w, conv_b, w_down, final_w, b, t):
    d = h.shape[-1]
    d_ff = w_down.shape[0]
    h3 = h.reshape(b, t, d)
    u3 = u.reshape(b, t, d)
    per = ROW_TILE // HALO
    full = lambda a: pl.BlockSpec(a.shape, lambda bi, ti: (0, 0))
    return pl.pallas_call(
        functools.partial(_ffn_kernel, d_ff=d_ff),
        grid=(b, t // ROW_TILE),
        in_specs=[
            pl.BlockSpec((1, ROW_TILE, d), lambda bi, ti: (bi, ti, 0)),
            pl.BlockSpec((1, ROW_TILE, d), lambda bi, ti: (bi, ti, 0)),
            pl.BlockSpec((1, HALO, d), lambda bi, ti: (bi, jnp.maximum(ti * per - 1, 0), 0)),
            full(w_up), full(conv_w), full(conv_b), full(w_down), full(final_w),
        ],
        out_specs=pl.BlockSpec((1, ROW_TILE, d), lambda bi, ti: (bi, ti, 0)),
        out_shape=jax.ShapeDtypeStruct((b, t, d), F32),
        scratch_shapes=[pltpu.VMEM((4, HALO + ROW_TILE, FFN_CHUNK), F32),
                        pltpu.VMEM((ROW_TILE, d_ff), BF16)],
        compiler_params=_params("parallel", "arbitrary"),
        name="convffn",
    )(h3, u3, u3, w_up, conv_w, conv_b, w_down, final_w)


def _pad_half(w, half):
    z = jnp.zeros_like(w)
    return jnp.concatenate([w, z] if half == 0 else [z, w], axis=-1)


def _input_weights(w):
    d = w.shape[0]
    nq, kvw = NSA_HEADS * HEAD_DIM, NSA_GROUPS * HEAD_DIM
    scale = HEAD_DIM ** -0.5
    o = 0
    q_n = w[:, o:o + nq] * scale; o += nq
    kc = w[:, o:o + kvw]; o += kvw
    vc = w[:, o:o + kvw]; o += kvw
    ks = w[:, o:o + kvw]; o += kvw
    vs = w[:, o:o + kvw]; o += kvw
    kw = w[:, o:o + kvw]; o += kvw
    vw = w[:, o:o + kvw]; o += kvw
    gl = w[:, o:o + NSA_HEADS * 3]; o += NSA_HEADS * 3
    sbw = SB_HEADS * HEAD_DIM
    q_s = w[:, o:o + sbw] * scale; o += sbw
    k_s = w[:, o:o + sbw]; o += sbw
    v_s = w[:, o:o + sbw]; o += sbw
    q_n = q_n.reshape(d, NSA_GROUPS, NSA_REP, HEAD_DIM)
    q_n = jnp.concatenate([_pad_half(q_n[:, g], g) for g in range(NSA_GROUPS)], axis=1).reshape(d, -1)
    gl = jnp.pad(gl, ((0, 0), (0, LANES - gl.shape[1])))
    pieces = (q_n, jnp.concatenate([kc, vc], axis=1), jnp.concatenate([ks, vs, kw, vw], axis=1),
              gl, q_s, k_s, v_s)
    return jnp.concatenate(pieces, axis=1).astype(BF16), tuple(p.shape[1] for p in pieces)


def _bias_tables(rel_bias, t):
    nq = t // NTQ
    rb = rel_bias.T.astype(F32)

    def by_signed(d, ok):
        onehot = (_t5_bucket_np(d).reshape(1, -1) == np.arange(N_BUCKETS)[:, None]) & ok.reshape(1, -1)
        vals = jnp.dot(rb, jnp.asarray(onehot, F32), precision=lax.Precision.HIGHEST)
        vals = vals + jnp.asarray(np.where(ok.reshape(1, -1), 0.0, NEG_INF), F32)
        return vals.reshape((rb.shape[0],) + d.shape)

    def toeplitz(d, ok, n_rows, n_cols):
        p = d.shape[-1]
        assert p == n_rows + n_cols - 1 and n_rows > 1
        perm = (n_cols - 1 - np.arange(p)) % p
        w = by_signed(d[..., perm], ok[..., perm])
        reps = (1,) * (w.ndim - 1) + (n_rows,)
        h = jnp.tile(w, reps)[..., :n_rows * (p - 1)].reshape(w.shape[:-1] + (n_rows, p - 1))
        return h[..., :n_cols]

    def stacked(tb):
        return tb.reshape(NSA_GROUPS, NSA_REP * NTQ, tb.shape[-1])

    span = WINDOW + NTQ
    dw = np.arange(NTQ + span - 1) - (span - 1) + WINDOW
    biasw = stacked(toeplitz(dw, (dw >= 0) & (dw < WINDOW), NTQ, span))
    assert NTK > MAX_DISTANCE
    ds = np.arange(NTQ + 3 * NTK - 1) - (3 * NTK - 1) + 2 * NTK
    biass = stacked(toeplitz(ds, ds >= 0, NTQ, 3 * NTK))
    n_cmp = (t - CMP_BLOCK) // CMP_STRIDE + 1
    n_a = t // CMP_STRIDE
    dc = (CMP_STRIDE * (np.arange(n_a + LANES - 1)[None, :] - (LANES - 1))
          + np.arange(CMP_STRIDE)[:, None] - (CMP_BLOCK - 1))
    bc = toeplitz(dc, dc >= 0, n_a, LANES)
    bc = jnp.where(jnp.asarray(np.arange(LANES) < n_cmp), bc, NEG_INF)
    bc = bc.transpose(0, 2, 1, 3).reshape(NSA_GROUPS, NSA_REP, nq, NTQ, LANES)
    biasc = bc.transpose(0, 2, 1, 3, 4).reshape(NSA_GROUPS, nq, NSA_REP * NTQ, LANES)
    return biasc, biass, biasw


def _suffix_sum_matrix():
    m = np.arange(TK)[:, None] >= np.arange(TK)[None, :]
    return jnp.asarray(np.concatenate([m, m], axis=0), BF16)


def _selection_constants(t):
    n_cmp = (t - CMP_BLOCK) // CMP_STRIDE + 1
    n_sel = t // SEL_BLOCK
    n_selp = -(-n_sel // 8) * 8
    n = np.arange(LANES)[None, :]
    j = np.arange(n_selp)[:, None]
    ovl = ((CMP_STRIDE * n < SEL_BLOCK * (j + 1)) & (CMP_STRIDE * n + CMP_BLOCK > SEL_BLOCK * j)
           & (n < n_cmp) & (j < n_sel))
    ovl = np.concatenate([ovl, np.ones((1, LANES), bool), np.zeros((7, LANES), bool)], axis=0)
    own = (np.arange(t)[:, None] // SEL_BLOCK) == np.arange(LANES)[None, :]
    blockneg = jnp.where(jnp.asarray(own), NEG_INF, 0.0).astype(BF16)
    return jnp.asarray(ovl, BF16), blockneg


def kernel(x, norm1_w, w_in, cmp_pos_k, cmp_pos_v, cmp_k_w1, cmp_k_w2, cmp_v_w1, cmp_v_w2, gate_b,
           nsa_out_norm_w, sb_out_norm_w, w_out, norm2_w, w_up, conv_w, conv_b, w_down, rel_bias,
           final_norm_w):
    b, t, d = x.shape
    assert t % ROW_TILE == 0 and t % NTQ == 0 and t >= WINDOW + NTQ and (b * t) % ROW_TILE == 0
    assert (t - CMP_BLOCK) // CMP_STRIDE + 1 < LANES + 1 and t // CMP_STRIDE == LANES
    assert w_in.shape[0] == 1, "single-layer block: the closing norm is fused into the FFN kernel"
    half = CMP_STRIDE * HEAD_DIM
    nsa_w = NSA_HEADS * HEAD_DIM
    l = 0

    biasc, biass, biasw = _bias_tables(rel_bias, t)
    ovl, blockneg = _selection_constants(t)
    tri = _suffix_sum_matrix()

    h = x.reshape(b * t, d)
    w_all, splits = _input_weights(w_in[l])
    gb = jnp.pad(gate_b[l], (0, LANES - gate_b.shape[1])).reshape(1, LANES)
    qn, kcvc, kv, gates, sbq, sbk, sbv = _inproj(h, norm1_w[l].reshape(1, d), w_all, gb, splits)

    chunks = kcvc.reshape(b, t // CMP_STRIDE, CMP_STRIDE, 4, HEAD_DIM)
    chunks = chunks.transpose(0, 3, 1, 2, 4).reshape(b, 4, t // CMP_STRIDE, half)
    pos = jnp.stack([cmp_pos_k[l].reshape(2, half), cmp_pos_v[l].reshape(2, half)])
    w1 = jnp.stack([cmp_k_w1[l].reshape(2, half, -1), cmp_v_w1[l].reshape(2, half, -1)]).astype(BF16)
    w2 = jnp.stack([jnp.stack([_pad_half(w, g) for g in range(NSA_GROUPS)])
                    for w in (cmp_k_w2[l], cmp_v_w2[l])]).astype(BF16)
    kcmp, vcmp = _compress(chunks, pos, w1, w2)

    o_nsa = _nsa(qn, kv, kcmp, vcmp, gates, biasc, biass, biasw, ovl, blockneg,
                 nsa_out_norm_w[l].reshape(1, nsa_w), b, t)
    o_sb = _sb(sbq, sbk, sbv, tri, sb_out_norm_w[l].reshape(1, -1), b, t)

    wo = w_out[l].astype(BF16)
    h1, u2 = _outproj(h, o_nsa.reshape(b * t, -1), o_sb.reshape(b * t, -1),
                      wo[:nsa_w], wo[nsa_w:], norm2_w[l].reshape(1, d))
    return _ffn(h1, u2, w_up[l].astype(BF16), conv_w[l], conv_b[l].reshape(1, -1),
                w_down[l].astype(BF16), final_norm_w.reshape(1, d), b, t)
```

```python
import functools
import math

import jax
import jax.numpy as jnp
import numpy as np
from jax import lax
from jax.experimental import pallas as pl
from jax.experimental.pallas import tpu as pltpu

F32 = jnp.float32
BF16 = jnp.bfloat16

HEAD_DIM = 64
NSA_HEADS = 8
SB_HEADS = 8
NSA_GROUPS = 2
NSA_REP = NSA_HEADS // NSA_GROUPS
CMP_BLOCK = 32
CMP_STRIDE = 16
SEL_BLOCK = 64
SEL_TOP = 8
WINDOW = 512
N_BUCKETS = 32
MAX_DISTANCE = 128
CONV_WIDTH = 3
EPS = 1e-6
NEG_INF = -1e30
FORCED_BONUS = 1e6
TINY = 1e-30

LANES = 128
TQ = 256
TK = 256
NTQ = 256
NTK = 256
ROW_TILE = 512
FFN_CHUNK = 256
HALO = 16
VMEM_LIMIT = 56 * 1024 * 1024
WIN_TILES = WINDOW // NTK + 1


def _dot(a, b):
    return jnp.dot(a, b, preferred_element_type=F32)


def _dot_nt(a, b):
    return lax.dot_general(a, b, (((1,), (1,)), ((), ())), preferred_element_type=F32)


def _params(*sem):
    return pltpu.CompilerParams(dimension_semantics=sem, vmem_limit_bytes=VMEM_LIMIT)


def _t5_bucket_np(dist):
    n = np.maximum(dist, 0)
    max_exact = N_BUCKETS // 2
    nf = np.maximum(n, 1).astype(np.float32)
    log_b = max_exact + (np.log(nf / np.float32(max_exact)) / np.float32(math.log(MAX_DISTANCE / max_exact))
                         * np.float32(N_BUCKETS - max_exact)).astype(np.int32)
    log_b = np.minimum(log_b, N_BUCKETS - 1)
    return np.where(n < max_exact, n, log_b)


def _inproj_kernel(x_ref, nw_ref, w_ref, gb_ref, qn_ref, kcvc_ref, kv_ref, gates_ref,
                   sbq_ref, sbk_ref, sbv_ref, *, splits):
    x = x_ref[...]
    u = x * lax.rsqrt(jnp.mean(x * x, axis=-1, keepdims=True) + EPS) * nw_ref[...]
    u = u.astype(BF16)
    outs = (qn_ref, kcvc_ref, kv_ref, gates_ref, sbq_ref, sbk_ref, sbv_ref)
    start = 0
    for ref, size in zip(outs, splits):
        r = _dot(u, w_ref[:, start:start + size])
        if ref is gates_ref:
            r = jax.nn.sigmoid(r + gb_ref[...])
        ref[...] = r.astype(ref.dtype)
        start += size


def _inproj(x2, norm_w, w_all, gate_b_pad, splits):
    n, d = x2.shape
    dts = (BF16, F32, BF16, F32, BF16, BF16, BF16)
    return pl.pallas_call(
        functools.partial(_inproj_kernel, splits=splits),
        grid=(n // ROW_TILE,),
        in_specs=[
            pl.BlockSpec((ROW_TILE, d), lambda i: (i, 0)),
            pl.BlockSpec((1, d), lambda i: (0, 0)),
            pl.BlockSpec(w_all.shape, lambda i: (0, 0)),
            pl.BlockSpec((1, LANES), lambda i: (0, 0)),
        ],
        out_specs=[pl.BlockSpec((ROW_TILE, s), lambda i: (i, 0)) for s in splits],
        out_shape=[jax.ShapeDtypeStruct((n, s), dt) for s, dt in zip(splits, dts)],
        compiler_params=_params("parallel"),
        name="inproj",
    )(x2, norm_w, w_all, gate_b_pad)


def _gelu_tanh(x):
    return 0.5 * x * (1.0 + jnp.tanh(math.sqrt(2.0 / math.pi) * (x + 0.044715 * (x * x * x))))


def _compress_kernel(c_ref, pos_ref, w1_ref, w2_ref, kc_ref, vc_ref):
    n_chunk = c_ref.shape[2]
    row = lax.broadcasted_iota(jnp.int32, (n_chunk, LANES), 0)
    for kv, out_ref in ((0, kc_ref), (1, vc_ref)):
        acc = jnp.zeros((n_chunk, LANES), F32)
        for g in range(NSA_GROUPS):
            c = c_ref[0, 2 * kv + g]
            top = _dot((c + pos_ref[kv, 0:1, :]).astype(BF16), w1_ref[kv, 0])
            bot = _dot((c + pos_ref[kv, 1:2, :]).astype(BF16), w1_ref[kv, 1])
            hidden = top + pltpu.roll(bot, n_chunk - 1, axis=0)
            acc = acc + _dot(_gelu_tanh(hidden).astype(BF16), w2_ref[kv, g])
        out_ref[0] = jnp.where(row < n_chunk - 1, acc, 0.0).astype(out_ref.dtype)


def _compress(chunks, pos, w1, w2):
    b, _, n_chunk, width = chunks.shape
    hid = w1.shape[-1]
    out = jax.ShapeDtypeStruct((b, n_chunk, LANES), BF16)
    return pl.pallas_call(
        _compress_kernel,
        grid=(b,),
        in_specs=[
            pl.BlockSpec((1, 4, n_chunk, width), lambda i: (i, 0, 0, 0)),
            pl.BlockSpec((2, 2, width), lambda i: (0, 0, 0)),
            pl.BlockSpec((2, 2, width, hid), lambda i: (0, 0, 0, 0)),
            pl.BlockSpec((2, 2, hid, LANES), lambda i: (0, 0, 0, 0)),
        ],
        out_specs=[pl.BlockSpec((1, n_chunk, LANES), lambda i: (i, 0, 0))] * 2,
        out_shape=[out, out],
        compiler_params=_params("parallel"),
        name="compress",
    )(chunks, pos, w1, w2)


def _nsa_kernel(q_ref, ks_ref, vs_ref, kw_ref, vw_ref, kc_ref, vc_ref, gates_ref,
                biasc_ref, biass_ref, biasw_ref, ovl_ref, blockneg_ref, normw_ref,
                o_ref, s_ref, m_ref, acc_sel_ref, acc_win_ref, qsel_ref, ocmp_ref):
    i = pl.program_id(1)
    t0 = i * NTQ
    rows = NSA_REP * NTQ
    groups = range(NSA_GROUPS)
    lane = lax.broadcasted_iota(jnp.int32, (rows, LANES), 1)
    ones_v = jnp.ones((NTK, LANES), BF16)
    gates = gates_ref[0]

    n_selp = ovl_ref.shape[0] - 8
    blk = lax.broadcasted_iota(jnp.int32, (n_selp, NTQ), 0)
    cur = (t0 + lax.broadcasted_iota(jnp.int32, (n_selp, NTQ), 1)) // SEL_BLOCK
    causal_blk = blk <= cur
    forced = (blk == 0) | (blk == cur) | (blk == cur - 1)

    def attend(j_lo, q_width, key_tile, bias_tile, v_ref, acc_ref):
        m_ref[...] = jnp.full(m_ref.shape, NEG_INF, F32)

        def logits(j, c):
            off = pl.multiple_of(j * NTK, NTK)
            k = key_tile(off)
            ss = [_dot_nt(qsel_ref[g, :, :q_width], k) for g in groups]
            for g in groups:
                s = ss[g] + bias_tile(g, j)
                s_ref[g, :, pl.ds(off, NTK)] = s
                m_ref[g] = jnp.maximum(m_ref[g], jnp.maximum(s[:, :LANES], s[:, LANES:]))
            return c

        lax.fori_loop(j_lo, i + 1, logits, 0)
        for g in groups:
            m_ref[g] = jnp.broadcast_to(jnp.max(m_ref[g], axis=-1, keepdims=True), (rows, LANES))
        acc_ref[...] = jnp.zeros(acc_ref.shape, F32)

        def pv(j, c):
            off = pl.multiple_of(j * NTK, NTK)
            v = jnp.concatenate([v_ref[0, pl.ds(off, NTK), :], ones_v], axis=1)
            ps = []
            for g in groups:
                m = m_ref[g]
                ps.append(jnp.exp(s_ref[g, :, pl.ds(off, NTK)] - jnp.concatenate([m, m], axis=1)).astype(BF16))
            for g in groups:
                acc_ref[g] += _dot(ps[g], v)
            return c

        lax.fori_loop(j_lo, i + 1, pv, 0)

    for g in groups:
        qg = jnp.concatenate(
            [q_ref[0, :, (g * NSA_REP + r) * LANES:(g * NSA_REP + r + 1) * LANES]
             for r in range(NSA_REP)], axis=0)

        lc = _dot_nt(qg, kc_ref[0]) + biasc_ref[g, 0]
        mc = jnp.maximum(jnp.max(lc, axis=-1, keepdims=True), 0.1 * NEG_INF)
        ec = jnp.exp(lc - mc).astype(BF16)
        oc = _dot(ec, jnp.concatenate([vc_ref[0], ones_v[:LANES]], axis=1))
        ocmp_ref[g] = oc[:, :LANES] * (1.0 / jnp.maximum(oc[:, LANES:], TINY))

        imp4 = _dot_nt(ovl_ref[...], ec)
        imp4 = imp4[:n_selp] * (1.0 / jnp.maximum(imp4[n_selp:n_selp + 1], TINY))
        imp = imp4[:, 0:NTQ]
        for r in range(1, NSA_REP):
            imp = imp + imp4[:, r * NTQ:(r + 1) * NTQ]
        score = jnp.where(causal_blk, imp + jnp.where(forced, FORCED_BONUS, 0.0), NEG_INF)
        rank = jnp.zeros((n_selp, NTQ), F32)
        for j in range(n_selp):
            other = score[j:j + 1, :]
            ahead = (other > score) | ((other == score) & (blk > j))
            rank = rank + jnp.where(ahead, 1.0, 0.0)
        unsel_t = jnp.where((rank < SEL_TOP) & causal_blk, 0.0, 1.0)
        unsel_t = jnp.concatenate([unsel_t, jnp.zeros((LANES - n_selp, NTQ), F32)], axis=0)
        unsel = unsel_t.T.astype(BF16)

        qsel_ref[g] = jnp.concatenate([qg, jnp.concatenate([unsel] * NSA_REP, axis=0)], axis=1)

    attend(
        0, 2 * LANES,
        lambda off: jnp.concatenate([ks_ref[0, pl.ds(off, NTK), :], blockneg_ref[pl.ds(off, NTK), :]], axis=1),
        lambda g, j: biass_ref[g, :, pl.ds(pl.multiple_of((2 - jnp.minimum(i - j, 2)) * NTK, NTK), NTK)],
        vs_ref, acc_sel_ref)

    attend(
        jnp.maximum(i - (WIN_TILES - 1), 0), LANES,
        lambda off: kw_ref[0, pl.ds(off, NTK), :],
        lambda g, j: biasw_ref[g, :, pl.ds(pl.multiple_of((j - i + WIN_TILES - 1) * NTK, NTK), NTK)],
        vw_ref, acc_win_ref)

    for g in groups:
        def gate(branch):
            return jnp.concatenate(
                [gates[:, (g * NSA_REP + r) * 3 + branch:(g * NSA_REP + r) * 3 + branch + 1]
                 for r in range(NSA_REP)], axis=0)

        o_sel = acc_sel_ref[g, :, :LANES] * (1.0 / acc_sel_ref[g, :, LANES:])
        o_win = acc_win_ref[g, :, :LANES] * (1.0 / acc_win_ref[g, :, LANES:])
        o = gate(0) * ocmp_ref[g] + gate(1) * o_sel + gate(2) * o_win
        mine = (lane // HEAD_DIM) == g
        ssq = jnp.sum(jnp.where(mine, o * o, 0.0), axis=-1, keepdims=True)
        y = o * lax.rsqrt(ssq * (1.0 / HEAD_DIM) + EPS)
        left = lax.broadcasted_iota(jnp.int32, (NTQ, LANES), 1) < HEAD_DIM
        for pair in range(NSA_REP // 2):
            even = y[(2 * pair) * NTQ:(2 * pair + 1) * NTQ]
            odd = y[(2 * pair + 1) * NTQ:(2 * pair + 2) * NTQ]
            if g == 0:
                odd = pltpu.roll(odd, HEAD_DIM, axis=1)
            else:
                even = pltpu.roll(even, HEAD_DIM, axis=1)
            cb = g * (NSA_REP // 2) + pair
            packed = jnp.where(left, even, odd) * normw_ref[:, cb * LANES:(cb + 1) * LANES]
            o_ref[0, :, cb * LANES:(cb + 1) * LANES] = packed.astype(o_ref.dtype)


def _nsa(qn, kv, kcmp, vcmp, gates, biasc, biass, biasw, ovl, blockneg, normw, b, t):
    nq = t // NTQ
    rows = NSA_REP * NTQ
    qn3 = qn.reshape(b, t, qn.shape[-1])
    kv3 = kv.reshape(b, t, kv.shape[-1])
    gates3 = gates.reshape(b, t, LANES)
    kv_spec = lambda c: pl.BlockSpec((1, t, LANES), lambda bi, qi, c=c: (bi, 0, c))
    const = lambda a: pl.BlockSpec(a.shape, lambda bi, qi: (0,) * a.ndim, pipeline_mode=pl.Buffered(1))
    return pl.pallas_call(
        _nsa_kernel,
        grid=(b, nq),
        in_specs=[
            pl.BlockSpec((1, NTQ, qn3.shape[-1]), lambda bi, qi: (bi, qi, 0)),
            kv_spec(0), kv_spec(1), kv_spec(2), kv_spec(3),
            pl.BlockSpec((1,) + kcmp.shape[1:], lambda bi, qi: (bi, 0, 0)),
            pl.BlockSpec((1,) + vcmp.shape[1:], lambda bi, qi: (bi, 0, 0)),
            pl.BlockSpec((1, NTQ, LANES), lambda bi, qi: (bi, qi, 0)),
            pl.BlockSpec((NSA_GROUPS, 1, rows, LANES), lambda bi, qi: (0, qi, 0, 0)),
            const(biass), const(biasw), const(ovl), const(blockneg), const(normw),
        ],
        out_specs=pl.BlockSpec((1, NTQ, NSA_HEADS * HEAD_DIM), lambda bi, qi: (bi, qi, 0)),
        out_shape=jax.ShapeDtypeStruct((b, t, NSA_HEADS * HEAD_DIM), BF16),
        scratch_shapes=[
            pltpu.VMEM((NSA_GROUPS, rows, t), F32),
            pltpu.VMEM((NSA_GROUPS, rows, LANES), F32),
            pltpu.VMEM((NSA_GROUPS, rows, 2 * LANES), F32),
            pltpu.VMEM((NSA_GROUPS, rows, 2 * LANES), F32),
            pltpu.VMEM((NSA_GROUPS, rows, 2 * LANES), BF16),
            pltpu.VMEM((NSA_GROUPS, rows, LANES), F32),
        ],
        compiler_params=_params("parallel", "arbitrary"),
        name="nsa",
    )(qn3, kv3, kv3, kv3, kv3, kcmp, vcmp, gates3, biasc, biass, biasw, ovl, blockneg, normw)


def _sb_kernel(q_ref, k_ref, v_ref, tri_ref, normw_ref, o_ref, acc_ref, run_ref):
    i = pl.program_id(1)
    rows = 2 * TQ
    n_pairs = SB_HEADS // 2
    lane = lax.broadcasted_iota(jnp.int32, (rows, LANES), 1)
    row = lax.broadcasted_iota(jnp.int32, (rows, LANES), 0)
    strict = lax.broadcasted_iota(jnp.int32, (rows, TK), 1) < (
        lax.broadcasted_iota(jnp.int32, (rows, TK), 0) & (TQ - 1))
    mine = (lane // HEAD_DIM) == (row // TQ)
    left = lax.broadcasted_iota(jnp.int32, (TQ, LANES), 1) < HEAD_DIM

    def tile(j, diag):
        off = pl.multiple_of(j * TK, TK)
        cols = [slice(p * LANES, (p + 1) * LANES) for p in range(n_pairs)]
        zs, csums = {}, {}

        def scores(p):
            q2 = q_ref[0, :, cols[p]]
            zero = jnp.zeros_like(q2)
            qp = jnp.concatenate([jnp.where(left, q2, zero), jnp.where(left, zero, q2)],
                                 axis=0)
            zs[p] = _dot_nt(qp, k_ref[0, pl.ds(off, TK), cols[p]])

        def suffix_sums(p):
            z = zs[p]
            neg_abs = lax.bitcast_convert_type(
                lax.bitcast_convert_type(z, jnp.uint32) | jnp.uint32(0x80000000), F32)
            sp = jnp.maximum(z, 0.0) + jnp.log(1.0 + jnp.exp(neg_abs))
            if diag:
                sp = jnp.where(strict, sp, 0.0)
            csums[p] = _dot(sp.astype(BF16), tri_ref[...])

        def weights_times_v(p):
            v = v_ref[0, pl.ds(off, TK), cols[p]]
            z, csum = zs.pop(p), csums.pop(p)
            tile_sum = jnp.broadcast_to(csum[:, 0:1], (rows, LANES))
            if diag:
                a = jnp.where(strict, jnp.exp(z - csum), 0.0)
                acc_ref[p] = _dot(a.astype(BF16), v)
                run_ref[p] = tile_sum
            else:
                run = run_ref[p]
                a = jnp.exp(z - csum - jnp.concatenate([run] * (TK // LANES), axis=1))
                acc_ref[p] += _dot(a.astype(BF16), v)
                run_ref[p] = run + tile_sum

        stages = (scores, suffix_sums, weights_times_v)
        for tick in range(n_pairs + len(stages) - 1):
            for k, stage in enumerate(stages):
                if 0 <= tick - k < n_pairs:
                    stage(tick - k)

    tile(i, True)

    def body(step, carry):
        tile(i - 1 - step, False)
        return carry

    lax.fori_loop(0, i, body, 0)

    for pair in range(n_pairs):
        cols = slice(pair * LANES, (pair + 1) * LANES)
        acc = acc_ref[pair]
        ssq = jnp.sum(jnp.where(mine, acc * acc, 0.0), axis=-1, keepdims=True)
        y = acc * lax.rsqrt(ssq * (1.0 / HEAD_DIM) + EPS)
        packed = jnp.where(left, y[:TQ], y[TQ:]) * normw_ref[:, cols]
        o_ref[0, :, cols] = packed.astype(o_ref.dtype)


def _sb(q, k, v, tri, normw, b, t):
    q3 = q.reshape(b, t, q.shape[-1])
    k3 = k.reshape(b, t, k.shape[-1])
    v3 = v.reshape(b, t, v.shape[-1])
    width = SB_HEADS * HEAD_DIM
    return pl.pallas_call(
        _sb_kernel,
        grid=(b, t // TQ),
        in_specs=[
            pl.BlockSpec((1, TQ, q3.shape[-1]), lambda bi, qi: (bi, qi, 0)),
            pl.BlockSpec((1, t, width), lambda bi, qi: (bi, 0, 0)),
            pl.BlockSpec((1, t, width), lambda bi, qi: (bi, 0, 0)),
            pl.BlockSpec(tri.shape, lambda bi, qi: (0, 0)),
            pl.BlockSpec(normw.shape, lambda bi, qi: (0, 0)),
        ],
        out_specs=pl.BlockSpec((1, TQ, width), lambda bi, qi: (bi, qi, 0)),
        out_shape=jax.ShapeDtypeStruct((b, t, width), BF16),
        scratch_shapes=[pltpu.VMEM((SB_HEADS // 2, 2 * TQ, LANES), F32)] * 2,
        compiler_params=_params("parallel", "arbitrary"),
        name="stickbreak",
    )(q3, k3, v3, tri, normw)


def _mixffn_kernel(x_ref, on_ref, os_ref, xh_ref, onh_ref, osh_ref, wn_ref, ws_ref, n2_ref,
                   wup_ref, cw_ref, cb_ref, wdown_ref, fw_ref, o_ref, up_ref, act_ref, *, d_ff):
    i = pl.program_id(1)
    tile = x_ref.shape[1]

    def mix(x, o_nsa, o_sb):
        h = x + _dot(o_nsa, wn_ref[...]) + _dot(o_sb, ws_ref[...])
        u = h * lax.rsqrt(jnp.mean(h * h, axis=-1, keepdims=True) + EPS) * n2_ref[...]
        return h, u.astype(BF16)

    h, u = mix(x_ref[0], on_ref[0], os_ref[0])
    _, u_halo = mix(xh_ref[0], onh_ref[0], osh_ref[0])
    halo = jnp.where(i > 0, u_halo, jnp.zeros_like(u_halo))
    ue = jnp.concatenate([halo, u], axis=0)

    def conv(slot, c0):
        w = cw_ref[:, c0:c0 + FFN_CHUNK]
        taps = [up_ref[slot, HALO - k:HALO - k + tile] for k in range(CONV_WIDTH)]
        return (w[2:3] * taps[0] + w[1:2] * taps[1] + w[0:1] * taps[2]
                + cb_ref[:, c0:c0 + FFN_CHUNK])

    def up_project(c):
        slot = 2 * (c % 2)
        up_ref[slot] = _dot(ue, wup_ref[:, c * FFN_CHUNK:(c + 1) * FFN_CHUNK])
        up_ref[slot + 1] = _dot(ue, wup_ref[:, d_ff + c * FFN_CHUNK:d_ff + (c + 1) * FFN_CHUNK])

    n_chunks = d_ff // FFN_CHUNK
    up_project(0)
    for c in range(n_chunks):
        g0, v0 = c * FFN_CHUNK, d_ff + c * FFN_CHUNK
        slot = 2 * (c % 2)
        if c + 1 < n_chunks:
            up_project(c + 1)
        gate = conv(slot, g0)
        val = conv(slot + 1, v0)
        act = gate * (1.0 / (1.0 + jnp.exp(-gate))) * val
        act_ref[:, g0:g0 + FFN_CHUNK] = act.astype(BF16)
    acc = h + _dot(act_ref[...], wdown_ref[...])
    y = acc * lax.rsqrt(jnp.mean(acc * acc, axis=-1, keepdims=True) + EPS) * fw_ref[...]
    o_ref[0] = y.astype(o_ref.dtype)


def _mixffn(x, o_nsa, o_sb, w_n, w_s, norm2_w, w_up, conv_w, conv_b, w_down, final_w):
    b, t, d = x.shape
    d_ff = w_down.shape[0]
    per = ROW_TILE // HALO
    tile = lambda a: pl.BlockSpec((1, ROW_TILE, a.shape[-1]), lambda bi, ti: (bi, ti, 0))
    halo = lambda a: pl.BlockSpec((1, HALO, a.shape[-1]), lambda bi, ti: (bi, jnp.maximum(ti * per - 1, 0), 0))
    const = lambda a: pl.BlockSpec(a.shape, lambda bi, ti: (0, 0), pipeline_mode=pl.Buffered(1))
    weights = (w_n, w_s, norm2_w, w_up, conv_w, conv_b, w_down, final_w)
    return pl.pallas_call(
        functools.partial(_mixffn_kernel, d_ff=d_ff),
        grid=(b, t // ROW_TILE),
        in_specs=[tile(x), tile(o_nsa), tile(o_sb), halo(x), halo(o_nsa), halo(o_sb)]
                 + [const(w) for w in weights],
        out_specs=pl.BlockSpec((1, ROW_TILE, d), lambda bi, ti: (bi, ti, 0)),
        out_shape=jax.ShapeDtypeStruct((b, t, d), F32),
        scratch_shapes=[pltpu.VMEM((4, HALO + ROW_TILE, FFN_CHUNK), F32),
                        pltpu.VMEM((ROW_TILE, d_ff), BF16)],
        compiler_params=_params("parallel", "arbitrary"),
        name="mixffn",
    )(x, o_nsa, o_sb, x, o_nsa, o_sb, *weights)


def _pad_half(w, half):
    z = jnp.zeros_like(w)
    return jnp.concatenate([w, z] if half == 0 else [z, w], axis=-1)


def _input_weights(w):
    d = w.shape[0]
    nq, kvw = NSA_HEADS * HEAD_DIM, NSA_GROUPS * HEAD_DIM
    scale = HEAD_DIM ** -0.5
    o = 0
    q_n = w[:, o:o + nq] * scale; o += nq
    kc = w[:, o:o + kvw]; o += kvw
    vc = w[:, o:o + kvw]; o += kvw
    ks = w[:, o:o + kvw]; o += kvw
    vs = w[:, o:o + kvw]; o += kvw
    kw = w[:, o:o + kvw]; o += kvw
    vw = w[:, o:o + kvw]; o += kvw
    gl = w[:, o:o + NSA_HEADS * 3]; o += NSA_HEADS * 3
    sbw = SB_HEADS * HEAD_DIM
    q_s = w[:, o:o + sbw] * scale; o += sbw
    k_s = w[:, o:o + sbw]; o += sbw
    v_s = w[:, o:o + sbw]; o += sbw
    q_n = q_n.reshape(d, NSA_GROUPS, NSA_REP, HEAD_DIM)
    q_n = jnp.concatenate([_pad_half(q_n[:, g], g) for g in range(NSA_GROUPS)], axis=1).reshape(d, -1)
    gl = jnp.pad(gl, ((0, 0), (0, LANES - gl.shape[1])))
    pieces = (q_n, jnp.concatenate([kc, vc], axis=1), jnp.concatenate([ks, vs, kw, vw], axis=1),
              gl, q_s, k_s, v_s)
    return jnp.concatenate(pieces, axis=1).astype(BF16), tuple(p.shape[1] for p in pieces)


def _bias_tables(rel_bias, t):
    nq = t // NTQ
    rb = rel_bias.T.astype(F32)

    def by_signed(d, ok):
        onehot = (_t5_bucket_np(d).reshape(1, -1) == np.arange(N_BUCKETS)[:, None]) & ok.reshape(1, -1)
        vals = jnp.dot(rb, jnp.asarray(onehot, F32), precision=lax.Precision.HIGHEST)
        vals = vals + jnp.asarray(np.where(ok.reshape(1, -1), 0.0, NEG_INF), F32)
        return vals.reshape((rb.shape[0],) + d.shape)

    def toeplitz(d, ok, n_rows, n_cols):
        p = d.shape[-1]
        assert p == n_rows + n_cols - 1 and n_rows > 1
        perm = (n_cols - 1 - np.arange(p)) % p
        w = by_signed(d[..., perm], ok[..., perm])
        reps = (1,) * (w.ndim - 1) + (n_rows,)
        h = jnp.tile(w, reps)[..., :n_rows * (p - 1)].reshape(w.shape[:-1] + (n_rows, p - 1))
        return h[..., :n_cols]

    def stacked(tb):
        return tb.reshape(NSA_GROUPS, NSA_REP * NTQ, tb.shape[-1])

    span = WINDOW + NTQ
    dw = np.arange(NTQ + span - 1) - (span - 1) + WINDOW
    biasw = stacked(toeplitz(dw, (dw >= 0) & (dw < WINDOW), NTQ, span))
    assert NTK > MAX_DISTANCE
    ds = np.arange(NTQ + 3 * NTK - 1) - (3 * NTK - 1) + 2 * NTK
    biass = stacked(toeplitz(ds, ds >= 0, NTQ, 3 * NTK))
    n_cmp = (t - CMP_BLOCK) // CMP_STRIDE + 1
    n_a = t // CMP_STRIDE
    dc = (CMP_STRIDE * (np.arange(n_a + LANES - 1)[None, :] - (LANES - 1))
          + np.arange(CMP_STRIDE)[:, None] - (CMP_BLOCK - 1))
    bc = toeplitz(dc, dc >= 0, n_a, LANES)
    bc = jnp.where(jnp.asarray(np.arange(LANES) < n_cmp), bc, NEG_INF)
    bc = bc.transpose(0, 2, 1, 3).reshape(NSA_GROUPS, NSA_REP, nq, NTQ, LANES)
    biasc = bc.transpose(0, 2, 1, 3, 4).reshape(NSA_GROUPS, nq, NSA_REP * NTQ, LANES)
    return biasc, biass, biasw


def _suffix_sum_matrix():
    return jnp.asarray(np.arange(TK)[:, None] >= np.arange(TK)[None, :], BF16)


def _selection_constants(t):
    n_cmp = (t - CMP_BLOCK) // CMP_STRIDE + 1
    n_sel = t // SEL_BLOCK
    n_selp = -(-n_sel // 8) * 8
    n = np.arange(LANES)[None, :]
    j = np.arange(n_selp)[:, None]
    ovl = ((CMP_STRIDE * n < SEL_BLOCK * (j + 1)) & (CMP_STRIDE * n + CMP_BLOCK > SEL_BLOCK * j)
           & (n < n_cmp) & (j < n_sel))
    ovl = np.concatenate([ovl, np.ones((1, LANES), bool), np.zeros((7, LANES), bool)], axis=0)
    own = (np.arange(t)[:, None] // SEL_BLOCK) == np.arange(LANES)[None, :]
    blockneg = jnp.where(jnp.asarray(own), NEG_INF, 0.0).astype(BF16)
    return jnp.asarray(ovl, BF16), blockneg


def kernel(x, norm1_w, w_in, cmp_pos_k, cmp_pos_v, cmp_k_w1, cmp_k_w2, cmp_v_w1, cmp_v_w2, gate_b,
           nsa_out_norm_w, sb_out_norm_w, w_out, norm2_w, w_up, conv_w, conv_b, w_down, rel_bias,
           final_norm_w):
    b, t, d = x.shape
    assert t % ROW_TILE == 0 and t % NTQ == 0 and t >= WINDOW + NTQ and (b * t) % ROW_TILE == 0
    assert (t - CMP_BLOCK) // CMP_STRIDE + 1 < LANES + 1 and t // CMP_STRIDE == LANES
    assert w_in.shape[0] == 1, "single-layer block: the closing norm is fused into the FFN kernel"
    half = CMP_STRIDE * HEAD_DIM
    nsa_w = NSA_HEADS * HEAD_DIM
    l = 0

    biasc, biass, biasw = _bias_tables(rel_bias, t)
    ovl, blockneg = _selection_constants(t)
    tri = _suffix_sum_matrix()

    h = x.reshape(b * t, d)
    w_all, splits = _input_weights(w_in[l])
    gb = jnp.pad(gate_b[l], (0, LANES - gate_b.shape[1])).reshape(1, LANES)
    qn, kcvc, kv, gates, sbq, sbk, sbv = _inproj(h, norm1_w[l].reshape(1, d), w_all, gb, splits)

    chunks = kcvc.reshape(b, t // CMP_STRIDE, CMP_STRIDE, 4, HEAD_DIM)
    chunks = chunks.transpose(0, 3, 1, 2, 4).reshape(b, 4, t // CMP_STRIDE, half)
    pos = jnp.stack([cmp_pos_k[l].reshape(2, half), cmp_pos_v[l].reshape(2, half)])
    w1 = jnp.stack([cmp_k_w1[l].reshape(2, half, -1), cmp_v_w1[l].reshape(2, half, -1)]).astype(BF16)
    w2 = jnp.stack([jnp.stack([_pad_half(w, g) for g in range(NSA_GROUPS)])
                    for w in (cmp_k_w2[l], cmp_v_w2[l])]).astype(BF16)
    kcmp, vcmp = _compress(chunks, pos, w1, w2)

    o_nsa = _nsa(qn, kv, kcmp, vcmp, gates, biasc, biass, biasw, ovl, blockneg,
                 nsa_out_norm_w[l].reshape(1, nsa_w), b, t)
    o_sb = _sb(sbq, sbk, sbv, tri, sb_out_norm_w[l].reshape(1, -1), b, t)

    wo = w_out[l].astype(BF16)
    return _mixffn(x, o_nsa, o_sb, wo[:nsa_w], wo[nsa_w:], norm2_w[l].reshape(1, d),
                   w_up[l].astype(BF16), conv_w[l], conv_b[l].reshape(1, -1),
                   w_down[l].astype(BF16), final_norm_w.reshape(1, d))
```

```python
import functools
import math

import jax
import jax.numpy as jnp
import numpy as np
from jax import lax
from jax.experimental import pallas as pl
from jax.experimental.pallas import tpu as pltpu

F32 = jnp.float32
BF16 = jnp.bfloat16

HEAD_DIM = 64
NSA_HEADS = 8
SB_HEADS = 8
NSA_GROUPS = 2
NSA_REP = NSA_HEADS // NSA_GROUPS
CMP_BLOCK = 32
CMP_STRIDE = 16
SEL_BLOCK = 64
SEL_TOP = 8
WINDOW = 512
N_BUCKETS = 32
MAX_DISTANCE = 128
CONV_WIDTH = 3
EPS = 1e-6
NEG_INF = -1e30
FORCED_BONUS = 1e6
TINY = 1e-30

LANES = 128
TQ = 256
TK = 256
NTQ = 256
NTK = 256
ROW_TILE = 512
FFN_CHUNK = 256
HALO = 16
VMEM_LIMIT = 56 * 1024 * 1024
WIN_TILES = WINDOW // NTK + 1


def _dot(a, b):
    return jnp.dot(a, b, preferred_element_type=F32)


def _dot_nt(a, b):
    return lax.dot_general(a, b, (((1,), (1,)), ((), ())), preferred_element_type=F32)


def _params(*sem):
    return pltpu.CompilerParams(dimension_semantics=sem, vmem_limit_bytes=VMEM_LIMIT)


def _t5_bucket_np(dist):
    n = np.maximum(dist, 0)
    max_exact = N_BUCKETS // 2
    nf = np.maximum(n, 1).astype(np.float32)
    log_b = max_exact + (np.log(nf / np.float32(max_exact)) / np.float32(math.log(MAX_DISTANCE / max_exact))
                         * np.float32(N_BUCKETS - max_exact)).astype(np.int32)
    log_b = np.minimum(log_b, N_BUCKETS - 1)
    return np.where(n < max_exact, n, log_b)


def _inproj_kernel(x_ref, nw_ref, w_ref, gb_ref, qn_ref, kcvc_ref, kv_ref, gates_ref,
                   sbq_ref, sbk_ref, sbv_ref, *, splits):
    x = x_ref[...]
    u = x * lax.rsqrt(jnp.mean(x * x, axis=-1, keepdims=True) + EPS) * nw_ref[...]
    u = u.astype(BF16)
    outs = (qn_ref, kcvc_ref, kv_ref, gates_ref, sbq_ref, sbk_ref, sbv_ref)
    start = 0
    for ref, size in zip(outs, splits):
        r = _dot(u, w_ref[:, start:start + size])
        if ref is gates_ref:
            r = jax.nn.sigmoid(r + gb_ref[...])
        ref[...] = r.astype(ref.dtype)
        start += size


def _inproj(x2, norm_w, w_all, gate_b_pad, splits):
    n, d = x2.shape
    dts = (BF16, F32, BF16, F32, BF16, BF16, BF16)
    return pl.pallas_call(
        functools.partial(_inproj_kernel, splits=splits),
        grid=(n // ROW_TILE,),
        in_specs=[
            pl.BlockSpec((ROW_TILE, d), lambda i: (i, 0)),
            pl.BlockSpec((1, d), lambda i: (0, 0)),
            pl.BlockSpec(w_all.shape, lambda i: (0, 0)),
            pl.BlockSpec((1, LANES), lambda i: (0, 0)),
        ],
        out_specs=[pl.BlockSpec((ROW_TILE, s), lambda i: (i, 0)) for s in splits],
        out_shape=[jax.ShapeDtypeStruct((n, s), dt) for s, dt in zip(splits, dts)],
        compiler_params=_params("parallel"),
        name="inproj",
    )(x2, norm_w, w_all, gate_b_pad)


def _gelu_tanh(x):
    return 0.5 * x * (1.0 + jnp.tanh(math.sqrt(2.0 / math.pi) * (x + 0.044715 * (x * x * x))))


def _compress_kernel(c_ref, pos_ref, w1_ref, w2_ref, kc_ref, vc_ref):
    n_chunk = c_ref.shape[2]
    row = lax.broadcasted_iota(jnp.int32, (n_chunk, LANES), 0)
    for kv, out_ref in ((0, kc_ref), (1, vc_ref)):
        acc = jnp.zeros((n_chunk, LANES), F32)
        for g in range(NSA_GROUPS):
            c = c_ref[0, 2 * kv + g]
            top = _dot((c + pos_ref[kv, 0:1, :]).astype(BF16), w1_ref[kv, 0])
            bot = _dot((c + pos_ref[kv, 1:2, :]).astype(BF16), w1_ref[kv, 1])
            hidden = top + pltpu.roll(bot, n_chunk - 1, axis=0)
            acc = acc + _dot(_gelu_tanh(hidden).astype(BF16), w2_ref[kv, g])
        out_ref[0] = jnp.where(row < n_chunk - 1, acc, 0.0).astype(out_ref.dtype)


def _compress(chunks, pos, w1, w2):
    b, _, n_chunk, width = chunks.shape
    hid = w1.shape[-1]
    out = jax.ShapeDtypeStruct((b, n_chunk, LANES), BF16)
    return pl.pallas_call(
        _compress_kernel,
        grid=(b,),
        in_specs=[
            pl.BlockSpec((1, 4, n_chunk, width), lambda i: (i, 0, 0, 0)),
            pl.BlockSpec((2, 2, width), lambda i: (0, 0, 0)),
            pl.BlockSpec((2, 2, width, hid), lambda i: (0, 0, 0, 0)),
            pl.BlockSpec((2, 2, hid, LANES), lambda i: (0, 0, 0, 0)),
        ],
        out_specs=[pl.BlockSpec((1, n_chunk, LANES), lambda i: (i, 0, 0))] * 2,
        out_shape=[out, out],
        compiler_params=_params("parallel"),
        name="compress",
    )(chunks, pos, w1, w2)


def _nsa_kernel(q_ref, ks_ref, vs_ref, kw_ref, vw_ref, kc_ref, vc_ref, gates_ref,
                biasc_ref, biass_ref, biasw_ref, ovl_ref, blockneg_ref, normw_ref,
                o_ref, s_sel_ref, s_win_ref, m_sel_ref, acc_sel_ref, acc_win_ref, qsel_ref, ocmp_ref):
    i = pl.program_id(1)
    t0 = i * NTQ
    rows = NSA_REP * NTQ
    groups = range(NSA_GROUPS)
    lane = lax.broadcasted_iota(jnp.int32, (rows, LANES), 1)
    ones_v = jnp.ones((NTK, LANES), BF16)
    gates = gates_ref[0]

    n_selp = ovl_ref.shape[0] - 8
    blk = lax.broadcasted_iota(jnp.int32, (n_selp, NTQ), 0)
    cur = (t0 + lax.broadcasted_iota(jnp.int32, (n_selp, NTQ), 1)) // SEL_BLOCK
    causal_blk = blk <= cur
    forced = (blk == 0) | (blk == cur) | (blk == cur - 1)

    def probabilities(s, m):
        return jnp.exp(s - jnp.concatenate([m, m], axis=1)).astype(BF16)

    qgs = [jnp.concatenate(
        [q_ref[0, :, (g * NSA_REP + r) * LANES:(g * NSA_REP + r + 1) * LANES] for r in range(NSA_REP)],
        axis=0) for g in groups]

    lcs = [_dot_nt(qgs[g], kc_ref[0]) + biasc_ref[g, 0] for g in groups]

    win_tiles = []
    for k in range(WIN_TILES):
        j = i - (WIN_TILES - 1) + k
        win_tiles.append((pl.multiple_of(jnp.maximum(j, 0) * NTK, NTK),
                          pl.multiple_of(jnp.where(j < 0, WIN_TILES, k) * NTK, NTK)))
    m_win = [jnp.full((rows, LANES), NEG_INF, F32) for g in groups]
    for k, (off, boff) in enumerate(win_tiles):
        key = kw_ref[0, pl.ds(off, NTK), :]
        raw = [_dot_nt(qgs[g], key) for g in groups]
        for g in groups:
            s = raw[g] + biasw_ref[g, :, pl.ds(boff, NTK)]
            s_win_ref[g, :, k * NTK:(k + 1) * NTK] = s
            m_win[g] = jnp.maximum(m_win[g], jnp.maximum(s[:, :LANES], s[:, LANES:]))

    ecs = []
    for g in groups:
        mc = jnp.maximum(jnp.max(lcs[g], axis=-1, keepdims=True), 0.1 * NEG_INF)
        ec = jnp.exp(lcs[g] - mc).astype(BF16)
        oc = _dot(ec, jnp.concatenate([vc_ref[0], ones_v[:LANES]], axis=1))
        ocmp_ref[g] = oc[:, :LANES] * (1.0 / jnp.maximum(oc[:, LANES:], TINY))
        ecs.append(ec)

    for g in groups:
        imp4 = _dot_nt(ovl_ref[...], ecs[g])
        imp4 = imp4[:n_selp] * (1.0 / jnp.maximum(imp4[n_selp:n_selp + 1], TINY))
        imp = imp4[:, 0:NTQ]
        for r in range(1, NSA_REP):
            imp = imp + imp4[:, r * NTQ:(r + 1) * NTQ]
        score = jnp.where(causal_blk, imp + jnp.where(forced, FORCED_BONUS, 0.0), NEG_INF)
        rank = jnp.zeros((n_selp, NTQ), F32)
        for j in range(n_selp):
            other = score[j:j + 1, :]
            ahead = (other > score) | ((other == score) & (blk > j))
            rank = rank + jnp.where(ahead, 1.0, 0.0)
        unsel_t = jnp.where((rank < SEL_TOP) & causal_blk, 0.0, 1.0)
        unsel_t = jnp.concatenate([unsel_t, jnp.zeros((LANES - n_selp, NTQ), F32)], axis=0)
        unsel = unsel_t.T.astype(BF16)

        qsel_ref[g] = jnp.concatenate([qgs[g], jnp.concatenate([unsel] * NSA_REP, axis=0)], axis=1)

    for g in groups:
        m_win[g] = jnp.broadcast_to(jnp.max(m_win[g], axis=-1, keepdims=True), (rows, LANES))
    acc_win = [jnp.zeros((rows, 2 * LANES), F32) for g in groups]
    for k, (off, _) in enumerate(win_tiles):
        v = jnp.concatenate([vw_ref[0, pl.ds(off, NTK), :], ones_v], axis=1)
        ps = [probabilities(s_win_ref[g, :, k * NTK:(k + 1) * NTK], m_win[g]) for g in groups]
        for g in groups:
            acc_win[g] = acc_win[g] + _dot(ps[g], v)
    for g in groups:
        acc_win_ref[g] = acc_win[g]

    m_sel_ref[...] = jnp.full(m_sel_ref.shape, NEG_INF, F32)

    def sel_logits(j, c):
        off = pl.multiple_of(j * NTK, NTK)
        boff = pl.multiple_of((2 - jnp.minimum(i - j, 2)) * NTK, NTK)
        key = jnp.concatenate([ks_ref[0, pl.ds(off, NTK), :], blockneg_ref[pl.ds(off, NTK), :]], axis=1)
        raw = [_dot_nt(qsel_ref[g], key) for g in groups]
        for g in groups:
            s = raw[g] + biass_ref[g, :, pl.ds(boff, NTK)]
            s_sel_ref[g, :, pl.ds(off, NTK)] = s
            m_sel_ref[g] = jnp.maximum(m_sel_ref[g], jnp.maximum(s[:, :LANES], s[:, LANES:]))
        return c

    lax.fori_loop(0, i + 1, sel_logits, 0)
    for g in groups:
        m_sel_ref[g] = jnp.broadcast_to(jnp.max(m_sel_ref[g], axis=-1, keepdims=True), (rows, LANES))
    acc_sel_ref[...] = jnp.zeros(acc_sel_ref.shape, F32)

    def sel_weights(j, c):
        off = pl.multiple_of(j * NTK, NTK)
        v = jnp.concatenate([vs_ref[0, pl.ds(off, NTK), :], ones_v], axis=1)
        ps = [probabilities(s_sel_ref[g, :, pl.ds(off, NTK)], m_sel_ref[g]) for g in groups]
        for g in groups:
            acc_sel_ref[g] += _dot(ps[g], v)
        return c

    lax.fori_loop(0, i + 1, sel_weights, 0)

    for g in groups:
        def gate(branch):
            return jnp.concatenate(
                [gates[:, (g * NSA_REP + r) * 3 + branch:(g * NSA_REP + r) * 3 + branch + 1]
                 for r in range(NSA_REP)], axis=0)

        o_sel = acc_sel_ref[g, :, :LANES] * (1.0 / acc_sel_ref[g, :, LANES:])
        o_win = acc_win_ref[g, :, :LANES] * (1.0 / acc_win_ref[g, :, LANES:])
        o = gate(0) * ocmp_ref[g] + gate(1) * o_sel + gate(2) * o_win
        mine = (lane // HEAD_DIM) == g
        ssq = jnp.sum(jnp.where(mine, o * o, 0.0), axis=-1, keepdims=True)
        y = o * lax.rsqrt(ssq * (1.0 / HEAD_DIM) + EPS)
        left = lax.broadcasted_iota(jnp.int32, (NTQ, LANES), 1) < HEAD_DIM
        for pair in range(NSA_REP // 2):
            even = y[(2 * pair) * NTQ:(2 * pair + 1) * NTQ]
            odd = y[(2 * pair + 1) * NTQ:(2 * pair + 2) * NTQ]
            if g == 0:
                odd = pltpu.roll(odd, HEAD_DIM, axis=1)
            else:
                even = pltpu.roll(even, HEAD_DIM, axis=1)
            cb = g * (NSA_REP // 2) + pair
            packed = jnp.where(left, even, odd) * normw_ref[:, cb * LANES:(cb + 1) * LANES]
            o_ref[0, :, cb * LANES:(cb + 1) * LANES] = packed.astype(o_ref.dtype)


def _nsa(qn, kv, kcmp, vcmp, gates, biasc, biass, biasw, ovl, blockneg, normw, b, t):
    nq = t // NTQ
    rows = NSA_REP * NTQ
    qn3 = qn.reshape(b, t, qn.shape[-1])
    kv3 = kv.reshape(b, t, kv.shape[-1])
    gates3 = gates.reshape(b, t, LANES)
    kv_spec = lambda c: pl.BlockSpec((1, t, LANES), lambda bi, qi, c=c: (bi, 0, c))
    const = lambda a: pl.BlockSpec(a.shape, lambda bi, qi: (0,) * a.ndim, pipeline_mode=pl.Buffered(1))
    return pl.pallas_call(
        _nsa_kernel,
        grid=(b, nq),
        in_specs=[
            pl.BlockSpec((1, NTQ, qn3.shape[-1]), lambda bi, qi: (bi, qi, 0)),
            kv_spec(0), kv_spec(1), kv_spec(2), kv_spec(3),
            pl.BlockSpec((1,) + kcmp.shape[1:], lambda bi, qi: (bi, 0, 0)),
            pl.BlockSpec((1,) + vcmp.shape[1:], lambda bi, qi: (bi, 0, 0)),
            pl.BlockSpec((1, NTQ, LANES), lambda bi, qi: (bi, qi, 0)),
            pl.BlockSpec((NSA_GROUPS, 1, rows, LANES), lambda bi, qi: (0, qi, 0, 0)),
            const(biass), const(biasw), const(ovl), const(blockneg), const(normw),
        ],
        out_specs=pl.BlockSpec((1, NTQ, NSA_HEADS * HEAD_DIM), lambda bi, qi: (bi, qi, 0)),
        out_shape=jax.ShapeDtypeStruct((b, t, NSA_HEADS * HEAD_DIM), BF16),
        scratch_shapes=[
            pltpu.VMEM((NSA_GROUPS, rows, t), F32),
            pltpu.VMEM((NSA_GROUPS, rows, WIN_TILES * NTK), F32),
            pltpu.VMEM((NSA_GROUPS, rows, LANES), F32),
            pltpu.VMEM((NSA_GROUPS, rows, 2 * LANES), F32),
            pltpu.VMEM((NSA_GROUPS, rows, 2 * LANES), F32),
            pltpu.VMEM((NSA_GROUPS, rows, 2 * LANES), BF16),
            pltpu.VMEM((NSA_GROUPS, rows, LANES), F32),
        ],
        compiler_params=_params("parallel", "arbitrary"),
        name="nsa",
    )(qn3, kv3, kv3, kv3, kv3, kcmp, vcmp, gates3, biasc, biass, biasw, ovl, blockneg, normw)


def _sb_kernel(q_ref, k_ref, v_ref, tri_ref, normw_ref, o_ref, acc_ref, run_ref):
    i = pl.program_id(1)
    rows = 2 * TQ
    n_pairs = SB_HEADS // 2
    lane = lax.broadcasted_iota(jnp.int32, (rows, LANES), 1)
    row = lax.broadcasted_iota(jnp.int32, (rows, LANES), 0)
    strict = lax.broadcasted_iota(jnp.int32, (rows, TK), 1) < (
        lax.broadcasted_iota(jnp.int32, (rows, TK), 0) & (TQ - 1))
    mine = (lane // HEAD_DIM) == (row // TQ)
    left = lax.broadcasted_iota(jnp.int32, (TQ, LANES), 1) < HEAD_DIM

    def tile(j, diag):
        off = pl.multiple_of(j * TK, TK)
        cols = [slice(p * LANES, (p + 1) * LANES) for p in range(n_pairs)]
        zs, csums = {}, {}

        def scores(p):
            q2 = q_ref[0, :, cols[p]]
            zero = jnp.zeros_like(q2)
            qp = jnp.concatenate([jnp.where(left, q2, zero), jnp.where(left, zero, q2)],
                                 axis=0)
            zs[p] = _dot_nt(qp, k_ref[0, pl.ds(off, TK), cols[p]])

        def suffix_sums(p):
            z = zs[p]
            neg_abs = lax.bitcast_convert_type(
                lax.bitcast_convert_type(z, jnp.uint32) | jnp.uint32(0x80000000), F32)
            sp = jnp.maximum(z, 0.0) + jnp.log(1.0 + jnp.exp(neg_abs))
            if diag:
                sp = jnp.where(strict, sp, 0.0)
            csums[p] = _dot(sp.astype(BF16), tri_ref[...])

        def weights_times_v(p):
            v = v_ref[0, pl.ds(off, TK), cols[p]]
            z, csum = zs.pop(p), csums.pop(p)
            tile_sum = jnp.broadcast_to(csum[:, 0:1], (rows, LANES))
            if diag:
                a = jnp.where(strict, jnp.exp(z - csum), 0.0)
                acc_ref[p] = _dot(a.astype(BF16), v)
                run_ref[p] = tile_sum
            else:
                run = run_ref[p]
                a = jnp.exp(z - csum - jnp.concatenate([run] * (TK // LANES), axis=1))
                acc_ref[p] += _dot(a.astype(BF16), v)
                run_ref[p] = run + tile_sum

        stages = (scores, suffix_sums, weights_times_v)
        for tick in range(n_pairs + len(stages) - 1):
            for k, stage in enumerate(stages):
                if 0 <= tick - k < n_pairs:
                    stage(tick - k)

    tile(i, True)

    def body(step, carry):
        tile(i - 1 - step, False)
        return carry

    lax.fori_loop(0, i, body, 0)

    for pair in range(n_pairs):
        cols = slice(pair * LANES, (pair + 1) * LANES)
        acc = acc_ref[pair]
        ssq = jnp.sum(jnp.where(mine, acc * acc, 0.0), axis=-1, keepdims=True)
        y = acc * lax.rsqrt(ssq * (1.0 / HEAD_DIM) + EPS)
        packed = jnp.where(left, y[:TQ], y[TQ:]) * normw_ref[:, cols]
        o_ref[0, :, cols] = packed.astype(o_ref.dtype)


def _sb(q, k, v, tri, normw, b, t):
    q3 = q.reshape(b, t, q.shape[-1])
    k3 = k.reshape(b, t, k.shape[-1])
    v3 = v.reshape(b, t, v.shape[-1])
    width = SB_HEADS * HEAD_DIM
    return pl.pallas_call(
        _sb_kernel,
        grid=(b, t // TQ),
        in_specs=[
            pl.BlockSpec((1, TQ, q3.shape[-1]), lambda bi, qi: (bi, qi, 0)),
            pl.BlockSpec((1, t, width), lambda bi, qi: (bi, 0, 0)),
            pl.BlockSpec((1, t, width), lambda bi, qi: (bi, 0, 0)),
            pl.BlockSpec(tri.shape, lambda bi, qi: (0, 0)),
            pl.BlockSpec(normw.shape, lambda bi, qi: (0, 0)),
        ],
        out_specs=pl.BlockSpec((1, TQ, width), lambda bi, qi: (bi, qi, 0)),
        out_shape=jax.ShapeDtypeStruct((b, t, width), BF16),
        scratch_shapes=[pltpu.VMEM((SB_HEADS // 2, 2 * TQ, LANES), F32)] * 2,
        compiler_params=_params("parallel", "arbitrary"),
        name="stickbreak",
    )(q3, k3, v3, tri, normw)


def _mixffn_kernel(x_ref, on_ref, os_ref, xh_ref, onh_ref, osh_ref, wn_ref, ws_ref, n2_ref,
                   wup_ref, cw_ref, cb_ref, wdown_ref, fw_ref, o_ref, up_ref, act_ref, *, d_ff):
    i = pl.program_id(1)
    tile = x_ref.shape[1]

    def mix(x, o_nsa, o_sb):
        h = x + _dot(o_nsa, wn_ref[...]) + _dot(o_sb, ws_ref[...])
        u = h * lax.rsqrt(jnp.mean(h * h, axis=-1, keepdims=True) + EPS) * n2_ref[...]
        return h, u.astype(BF16)

    h, u = mix(x_ref[0], on_ref[0], os_ref[0])
    _, u_halo = mix(xh_ref[0], onh_ref[0], osh_ref[0])
    halo = jnp.where(i > 0, u_halo, jnp.zeros_like(u_halo))
    ue = jnp.concatenate([halo, u], axis=0)

    def conv(slot, c0):
        w = cw_ref[:, c0:c0 + FFN_CHUNK]
        taps = [up_ref[slot, HALO - k:HALO - k + tile] for k in range(CONV_WIDTH)]
        return (w[2:3] * taps[0] + w[1:2] * taps[1] + w[0:1] * taps[2]
                + cb_ref[:, c0:c0 + FFN_CHUNK])

    def up_project(c):
        slot = 2 * (c % 2)
        up_ref[slot] = _dot(ue, wup_ref[:, c * FFN_CHUNK:(c + 1) * FFN_CHUNK])
        up_ref[slot + 1] = _dot(ue, wup_ref[:, d_ff + c * FFN_CHUNK:d_ff + (c + 1) * FFN_CHUNK])

    n_chunks = d_ff // FFN_CHUNK
    up_project(0)
    for c in range(n_chunks):
        g0, v0 = c * FFN_CHUNK, d_ff + c * FFN_CHUNK
        slot = 2 * (c % 2)
        if c + 1 < n_chunks:
            up_project(c + 1)
        gate = conv(slot, g0)
        val = conv(slot + 1, v0)
        act = gate * (1.0 / (1.0 + jnp.exp(-gate))) * val
        act_ref[:, g0:g0 + FFN_CHUNK] = act.astype(BF16)
    acc = h + _dot(act_ref[...], wdown_ref[...])
    y = acc * lax.rsqrt(jnp.mean(acc * acc, axis=-1, keepdims=True) + EPS) * fw_ref[...]
    o_ref[0] = y.astype(o_ref.dtype)


def _mixffn(x, o_nsa, o_sb, w_n, w_s, norm2_w, w_up, conv_w, conv_b, w_down, final_w):
    b, t, d = x.shape
    d_ff = w_down.shape[0]
    per = ROW_TILE // HALO
    tile = lambda a: pl.BlockSpec((1, ROW_TILE, a.shape[-1]), lambda bi, ti: (bi, ti, 0))
    halo = lambda a: pl.BlockSpec((1, HALO, a.shape[-1]), lambda bi, ti: (bi, jnp.maximum(ti * per - 1, 0), 0))
    const = lambda a: pl.BlockSpec(a.shape, lambda bi, ti: (0, 0), pipeline_mode=pl.Buffered(1))
    weights = (w_n, w_s, norm2_w, w_up, conv_w, conv_b, w_down, final_w)
    return pl.pallas_call(
        functools.partial(_mixffn_kernel, d_ff=d_ff),
        grid=(b, t // ROW_TILE),
        in_specs=[tile(x), tile(o_nsa), tile(o_sb), halo(x), halo(o_nsa), halo(o_sb)]
                 + [const(w) for w in weights],
        out_specs=pl.BlockSpec((1, ROW_TILE, d), lambda bi, ti: (bi, ti, 0)),
        out_shape=jax.ShapeDtypeStruct((b, t, d), F32),
        scratch_shapes=[pltpu.VMEM((4, HALO + ROW_TILE, FFN_CHUNK), F32),
                        pltpu.VMEM((ROW_TILE, d_ff), BF16)],
        compiler_params=_params("parallel", "arbitrary"),
        name="mixffn",
    )(x, o_nsa, o_sb, x, o_nsa, o_sb, *weights)


def _pad_half(w, half):
    z = jnp.zeros_like(w)
    return jnp.concatenate([w, z] if half == 0 else [z, w], axis=-1)


def _input_weights(w):
    d = w.shape[0]
    nq, kvw = NSA_HEADS * HEAD_DIM, NSA_GROUPS * HEAD_DIM
    scale = HEAD_DIM ** -0.5
    o = 0
    q_n = w[:, o:o + nq] * scale; o += nq
    kc = w[:, o:o + kvw]; o += kvw
    vc = w[:, o:o + kvw]; o += kvw
    ks = w[:, o:o + kvw]; o += kvw
    vs = w[:, o:o + kvw]; o += kvw
    kw = w[:, o:o + kvw]; o += kvw
    vw = w[:, o:o + kvw]; o += kvw
    gl = w[:, o:o + NSA_HEADS * 3]; o += NSA_HEADS * 3
    sbw = SB_HEADS * HEAD_DIM
    q_s = w[:, o:o + sbw] * scale; o += sbw
    k_s = w[:, o:o + sbw]; o += sbw
    v_s = w[:, o:o + sbw]; o += sbw
    q_n = q_n.reshape(d, NSA_GROUPS, NSA_REP, HEAD_DIM)
    q_n = jnp.concatenate([_pad_half(q_n[:, g], g) for g in range(NSA_GROUPS)], axis=1).reshape(d, -1)
    gl = jnp.pad(gl, ((0, 0), (0, LANES - gl.shape[1])))
    pieces = (q_n, jnp.concatenate([kc, vc], axis=1), jnp.concatenate([ks, vs, kw, vw], axis=1),
              gl, q_s, k_s, v_s)
    return jnp.concatenate(pieces, axis=1).astype(BF16), tuple(p.shape[1] for p in pieces)


def _bias_tables(rel_bias, t):
    nq = t // NTQ
    rb = rel_bias.T.astype(F32)

    def by_signed(d, ok):
        onehot = (_t5_bucket_np(d).reshape(1, -1) == np.arange(N_BUCKETS)[:, None]) & ok.reshape(1, -1)
        vals = jnp.dot(rb, jnp.asarray(onehot, F32), precision=lax.Precision.HIGHEST)
        vals = vals + jnp.asarray(np.where(ok.reshape(1, -1), 0.0, NEG_INF), F32)
        return vals.reshape((rb.shape[0],) + d.shape)

    def toeplitz(d, ok, n_rows, n_cols):
        p = d.shape[-1]
        assert p == n_rows + n_cols - 1 and n_rows > 1
        perm = (n_cols - 1 - np.arange(p)) % p
        w = by_signed(d[..., perm], ok[..., perm])
        reps = (1,) * (w.ndim - 1) + (n_rows,)
        h = jnp.tile(w, reps)[..., :n_rows * (p - 1)].reshape(w.shape[:-1] + (n_rows, p - 1))
        return h[..., :n_cols]

    def stacked(tb):
        return tb.reshape(NSA_GROUPS, NSA_REP * NTQ, tb.shape[-1])

    span = WINDOW + NTQ
    dw = np.arange(NTQ + span - 1) - (span - 1) + WINDOW
    biasw = stacked(toeplitz(dw, (dw >= 0) & (dw < WINDOW), NTQ, span))
    biasw = jnp.concatenate([biasw, jnp.full(biasw.shape[:-1] + (NTK,), NEG_INF, F32)], axis=-1)
    assert NTK > MAX_DISTANCE
    ds = np.arange(NTQ + 3 * NTK - 1) - (3 * NTK - 1) + 2 * NTK
    biass = stacked(toeplitz(ds, ds >= 0, NTQ, 3 * NTK))
    n_cmp = (t - CMP_BLOCK) // CMP_STRIDE + 1
    n_a = t // CMP_STRIDE
    dc = (CMP_STRIDE * (np.arange(n_a + LANES - 1)[None, :] - (LANES - 1))
          + np.arange(CMP_STRIDE)[:, None] - (CMP_BLOCK - 1))
    bc = toeplitz(dc, dc >= 0, n_a, LANES)
    bc = jnp.where(jnp.asarray(np.arange(LANES) < n_cmp), bc, NEG_INF)
    bc = bc.transpose(0, 2, 1, 3).reshape(NSA_GROUPS, NSA_REP, nq, NTQ, LANES)
    biasc = bc.transpose(0, 2, 1, 3, 4).reshape(NSA_GROUPS, nq, NSA_REP * NTQ, LANES)
    return biasc, biass, biasw


def _suffix_sum_matrix():
    return jnp.asarray(np.arange(TK)[:, None] >= np.arange(TK)[None, :], BF16)


def _selection_constants(t):
    n_cmp = (t - CMP_BLOCK) // CMP_STRIDE + 1
    n_sel = t // SEL_BLOCK
    n_selp = -(-n_sel // 8) * 8
    n = np.arange(LANES)[None, :]
    j = np.arange(n_selp)[:, None]
    ovl = ((CMP_STRIDE * n < SEL_BLOCK * (j + 1)) & (CMP_STRIDE * n + CMP_BLOCK > SEL_BLOCK * j)
           & (n < n_cmp) & (j < n_sel))
    ovl = np.concatenate([ovl, np.ones((1, LANES), bool), np.zeros((7, LANES), bool)], axis=0)
    own = (np.arange(t)[:, None] // SEL_BLOCK) == np.arange(LANES)[None, :]
    blockneg = jnp.where(jnp.asarray(own), NEG_INF, 0.0).astype(BF16)
    return jnp.asarray(ovl, BF16), blockneg


def kernel(x, norm1_w, w_in, cmp_pos_k, cmp_pos_v, cmp_k_w1, cmp_k_w2, cmp_v_w1, cmp_v_w2, gate_b,
           nsa_out_norm_w, sb_out_norm_w, w_out, norm2_w, w_up, conv_w, conv_b, w_down, rel_bias,
           final_norm_w):
    b, t, d = x.shape
    assert t % ROW_TILE == 0 and t % NTQ == 0 and t >= WINDOW + NTQ and (b * t) % ROW_TILE == 0
    assert (t - CMP_BLOCK) // CMP_STRIDE + 1 < LANES + 1 and t // CMP_STRIDE == LANES
    assert w_in.shape[0] == 1, "single-layer block: the closing norm is fused into the FFN kernel"
    half = CMP_STRIDE * HEAD_DIM
    nsa_w = NSA_HEADS * HEAD_DIM
    l = 0

    biasc, biass, biasw = _bias_tables(rel_bias, t)
    ovl, blockneg = _selection_constants(t)
    tri = _suffix_sum_matrix()

    h = x.reshape(b * t, d)
    w_all, splits = _input_weights(w_in[l])
    gb = jnp.pad(gate_b[l], (0, LANES - gate_b.shape[1])).reshape(1, LANES)
    qn, kcvc, kv, gates, sbq, sbk, sbv = _inproj(h, norm1_w[l].reshape(1, d), w_all, gb, splits)

    chunks = kcvc.reshape(b, t // CMP_STRIDE, CMP_STRIDE, 4, HEAD_DIM)
    chunks = chunks.transpose(0, 3, 1, 2, 4).reshape(b, 4, t // CMP_STRIDE, half)
    pos = jnp.stack([cmp_pos_k[l].reshape(2, half), cmp_pos_v[l].reshape(2, half)])
    w1 = jnp.stack([cmp_k_w1[l].reshape(2, half, -1), cmp_v_w1[l].reshape(2, half, -1)]).astype(BF16)
    w2 = jnp.stack([jnp.stack([_pad_half(w, g) for g in range(NSA_GROUPS)])
                    for w in (cmp_k_w2[l], cmp_v_w2[l])]).astype(BF16)
    kcmp, vcmp = _compress(chunks, pos, w1, w2)

    o_nsa = _nsa(qn, kv, kcmp, vcmp, gates, biasc, biass, biasw, ovl, blockneg,
                 nsa_out_norm_w[l].reshape(1, nsa_w), b, t)
    o_sb = _sb(sbq, sbk, sbv, tri, sb_out_norm_w[l].reshape(1, -1), b, t)

    wo = w_out[l].astype(BF16)
    return _mixffn(x, o_nsa, o_sb, wo[:nsa_w], wo[nsa_w:], norm2_w[l].reshape(1, d),
                   w_up[l].astype(BF16), conv_w[l], conv_b[l].reshape(1, -1),
                   w_down[l].astype(BF16), final_norm_w.reshape(1, d))
```

```python
import functools
import math

import jax
import jax.numpy as jnp
import numpy as np
from jax import lax
from jax.experimental import pallas as pl
from jax.experimental.pallas import tpu as pltpu

F32 = jnp.float32
BF16 = jnp.bfloat16

HEAD_DIM = 64
NSA_HEADS = 8
SB_HEADS = 8
NSA_GROUPS = 2
NSA_REP = NSA_HEADS // NSA_GROUPS
CMP_BLOCK = 32
CMP_STRIDE = 16
SEL_BLOCK = 64
SEL_TOP = 8
WINDOW = 512
N_BUCKETS = 32
MAX_DISTANCE = 128
CONV_WIDTH = 3
EPS = 1e-6
NEG_INF = -1e30
FORCED_BONUS = 1e6
TINY = 1e-30

LANES = 128
TQ = 256
TK = 256
NTQ = 256
NTK = 256
ROW_TILE = 512
FFN_CHUNK = 256
HALO = 16
VMEM_LIMIT = 56 * 1024 * 1024
WIN_TILES = WINDOW // NTK + 1


def _dot(a, b):
    return jnp.dot(a, b, preferred_element_type=F32)


def _dot_nt(a, b):
    return lax.dot_general(a, b, (((1,), (1,)), ((), ())), preferred_element_type=F32)


def _params(*sem):
    return pltpu.CompilerParams(dimension_semantics=sem, vmem_limit_bytes=VMEM_LIMIT)


def _t5_bucket_np(dist):
    n = np.maximum(dist, 0)
    max_exact = N_BUCKETS // 2
    nf = np.maximum(n, 1).astype(np.float32)
    log_b = max_exact + (np.log(nf / np.float32(max_exact)) / np.float32(math.log(MAX_DISTANCE / max_exact))
                         * np.float32(N_BUCKETS - max_exact)).astype(np.int32)
    log_b = np.minimum(log_b, N_BUCKETS - 1)
    return np.where(n < max_exact, n, log_b)


def _inproj_kernel(x_ref, nw_ref, w_ref, gb_ref, qn_ref, kcvc_ref, kv_ref, gates_ref,
                   sbq_ref, sbk_ref, sbv_ref, *, splits):
    x = x_ref[...]
    u = x * lax.rsqrt(jnp.mean(x * x, axis=-1, keepdims=True) + EPS) * nw_ref[...]
    u = u.astype(BF16)
    outs = (qn_ref, kcvc_ref, kv_ref, gates_ref, sbq_ref, sbk_ref, sbv_ref)
    start = 0
    for ref, size in zip(outs, splits):
        r = _dot(u, w_ref[:, start:start + size])
        if ref is gates_ref:
            r = jax.nn.sigmoid(r + gb_ref[...])
        ref[...] = r.astype(ref.dtype)
        start += size


def _inproj(x2, norm_w, w_all, gate_b_pad, splits):
    n, d = x2.shape
    dts = (BF16, F32, BF16, F32, BF16, BF16, BF16)
    return pl.pallas_call(
        functools.partial(_inproj_kernel, splits=splits),
        grid=(n // ROW_TILE,),
        in_specs=[
            pl.BlockSpec((ROW_TILE, d), lambda i: (i, 0)),
            pl.BlockSpec((1, d), lambda i: (0, 0)),
            pl.BlockSpec(w_all.shape, lambda i: (0, 0)),
            pl.BlockSpec((1, LANES), lambda i: (0, 0)),
        ],
        out_specs=[pl.BlockSpec((ROW_TILE, s), lambda i: (i, 0)) for s in splits],
        out_shape=[jax.ShapeDtypeStruct((n, s), dt) for s, dt in zip(splits, dts)],
        compiler_params=_params("parallel"),
        name="inproj",
    )(x2, norm_w, w_all, gate_b_pad)


def _gelu_tanh(x):
    return 0.5 * x * (1.0 + jnp.tanh(math.sqrt(2.0 / math.pi) * (x + 0.044715 * (x * x * x))))


def _compress_kernel(c_ref, pos_ref, w1_ref, w2_ref, kc_ref, vc_ref):
    n_chunk = c_ref.shape[2]
    row = lax.broadcasted_iota(jnp.int32, (n_chunk, LANES), 0)
    for kv, out_ref in ((0, kc_ref), (1, vc_ref)):
        acc = jnp.zeros((n_chunk, LANES), F32)
        for g in range(NSA_GROUPS):
            c = c_ref[0, 2 * kv + g]
            top = _dot((c + pos_ref[kv, 0:1, :]).astype(BF16), w1_ref[kv, 0])
            bot = _dot((c + pos_ref[kv, 1:2, :]).astype(BF16), w1_ref[kv, 1])
            hidden = top + pltpu.roll(bot, n_chunk - 1, axis=0)
            acc = acc + _dot(_gelu_tanh(hidden).astype(BF16), w2_ref[kv, g])
        out_ref[0] = jnp.where(row < n_chunk - 1, acc, 0.0).astype(out_ref.dtype)


def _compress(chunks, pos, w1, w2):
    b, _, n_chunk, width = chunks.shape
    hid = w1.shape[-1]
    out = jax.ShapeDtypeStruct((b, n_chunk, LANES), BF16)
    return pl.pallas_call(
        _compress_kernel,
        grid=(b,),
        in_specs=[
            pl.BlockSpec((1, 4, n_chunk, width), lambda i: (i, 0, 0, 0)),
            pl.BlockSpec((2, 2, width), lambda i: (0, 0, 0)),
            pl.BlockSpec((2, 2, width, hid), lambda i: (0, 0, 0, 0)),
            pl.BlockSpec((2, 2, hid, LANES), lambda i: (0, 0, 0, 0)),
        ],
        out_specs=[pl.BlockSpec((1, n_chunk, LANES), lambda i: (i, 0, 0))] * 2,
        out_shape=[out, out],
        compiler_params=_params("parallel"),
        name="compress",
    )(chunks, pos, w1, w2)


def _nsa_kernel(q_ref, ks_ref, vs_ref, kw_ref, vw_ref, kc_ref, vc_ref, gates_ref,
                biasc_ref, biass_ref, biasw_ref, ovl_ref, blockneg_ref, normw_ref,
                o_ref, s_sel_ref, s_win_ref, m_sel_ref, acc_sel_ref, owin_ref, gsel_ref, qsel_ref, ocmp_ref):
    i = pl.program_id(1)
    t0 = i * NTQ
    rows = NSA_REP * NTQ
    groups = range(NSA_GROUPS)
    lane = lax.broadcasted_iota(jnp.int32, (rows, LANES), 1)
    ones_v = jnp.ones((NTK, LANES), BF16)
    gates = gates_ref[0]

    n_selp = ovl_ref.shape[0] - 8
    blk = lax.broadcasted_iota(jnp.int32, (n_selp, NTQ), 0)
    cur = (t0 + lax.broadcasted_iota(jnp.int32, (n_selp, NTQ), 1)) // SEL_BLOCK
    causal_blk = blk <= cur
    forced = (blk == 0) | (blk == cur) | (blk == cur - 1)

    def probabilities(s, m):
        return jnp.exp(s - jnp.concatenate([m, m], axis=1)).astype(BF16)

    qgs = [jnp.concatenate(
        [q_ref[0, :, (g * NSA_REP + r) * LANES:(g * NSA_REP + r + 1) * LANES] for r in range(NSA_REP)],
        axis=0) for g in groups]

    lcs = [_dot_nt(qgs[g], kc_ref[0]) + biasc_ref[g, 0] for g in groups]

    win_tiles = []
    for k in range(WIN_TILES):
        j = i - (WIN_TILES - 1) + k
        win_tiles.append((pl.multiple_of(jnp.maximum(j, 0) * NTK, NTK),
                          pl.multiple_of(jnp.where(j < 0, WIN_TILES, k) * NTK, NTK)))
    m_win = [jnp.full((rows, LANES), NEG_INF, F32) for g in groups]
    for k, (off, boff) in enumerate(win_tiles):
        key = kw_ref[0, pl.ds(off, NTK), :]
        raw = [_dot_nt(qgs[g], key) for g in groups]
        for g in groups:
            s = raw[g] + biasw_ref[g, :, pl.ds(boff, NTK)]
            s_win_ref[g, :, k * NTK:(k + 1) * NTK] = s
            m_win[g] = jnp.maximum(m_win[g], jnp.maximum(s[:, :LANES], s[:, LANES:]))

    ecs = []
    for g in groups:
        mc = jnp.maximum(jnp.max(lcs[g], axis=-1, keepdims=True), 0.1 * NEG_INF)
        ec = jnp.exp(lcs[g] - mc).astype(BF16)
        oc = _dot(ec, jnp.concatenate([vc_ref[0], ones_v[:LANES]], axis=1))
        ocmp_ref[g] = oc[:, :LANES] * (1.0 / jnp.maximum(oc[:, LANES:], TINY))
        ecs.append(ec)

    for g in groups:
        imp4 = _dot_nt(ovl_ref[...], ecs[g])
        imp4 = imp4[:n_selp] * (1.0 / jnp.maximum(imp4[n_selp:n_selp + 1], TINY))
        imp = imp4[:, 0:NTQ]
        for r in range(1, NSA_REP):
            imp = imp + imp4[:, r * NTQ:(r + 1) * NTQ]
        score = jnp.where(causal_blk, imp + jnp.where(forced, FORCED_BONUS, 0.0), NEG_INF)
        rank = jnp.zeros((n_selp, NTQ), F32)
        for j in range(n_selp):
            other = score[j:j + 1, :]
            ahead = (other > score) | ((other == score) & (blk > j))
            rank = rank + jnp.where(ahead, 1.0, 0.0)
        unsel_t = jnp.where((rank < SEL_TOP) & causal_blk, 0.0, 1.0)
        unsel_t = jnp.concatenate([unsel_t, jnp.zeros((LANES - n_selp, NTQ), F32)], axis=0)
        unsel = unsel_t.T.astype(BF16)

        qsel_ref[g] = jnp.concatenate([qgs[g], jnp.concatenate([unsel] * NSA_REP, axis=0)], axis=1)

    for g in groups:
        m_win[g] = jnp.broadcast_to(jnp.max(m_win[g], axis=-1, keepdims=True), (rows, LANES))
    acc_win = [jnp.zeros((rows, 2 * LANES), F32) for g in groups]
    for k, (off, _) in enumerate(win_tiles):
        v = jnp.concatenate([vw_ref[0, pl.ds(off, NTK), :], ones_v], axis=1)
        ps = [probabilities(s_win_ref[g, :, k * NTK:(k + 1) * NTK], m_win[g]) for g in groups]
        for g in groups:
            acc_win[g] = acc_win[g] + _dot(ps[g], v)

    def gate(g, branch):
        return jnp.concatenate(
            [gates[:, (g * NSA_REP + r) * 3 + branch:(g * NSA_REP + r) * 3 + branch + 1]
             for r in range(NSA_REP)], axis=0)

    for g in groups:
        ocmp_ref[g] = gate(g, 0) * ocmp_ref[g]
        gsel_ref[g] = jnp.broadcast_to(gate(g, 1), (rows, LANES))
        owin_ref[g] = gate(g, 2) * (acc_win[g][:, :LANES] * (1.0 / acc_win[g][:, LANES:]))

    m_sel_ref[...] = jnp.full(m_sel_ref.shape, NEG_INF, F32)

    n_double = (i + 1) // 2

    def sel_logits(j, n_tiles):
        off = pl.multiple_of(j * NTK, NTK)
        width = n_tiles * NTK
        key = jnp.concatenate([ks_ref[0, pl.ds(off, width), :], blockneg_ref[pl.ds(off, width), :]], axis=1)
        raw = [_dot_nt(qsel_ref[g], key) for g in groups]
        for g in groups:
            m = m_sel_ref[g]
            for k in range(n_tiles):
                boff = pl.multiple_of((2 - jnp.minimum(i - j - k, 2)) * NTK, NTK)
                s = raw[g][:, k * NTK:(k + 1) * NTK] + biass_ref[g, :, pl.ds(boff, NTK)]
                s_sel_ref[g, :, pl.ds(pl.multiple_of(off + k * NTK, NTK), NTK)] = s
                m = jnp.maximum(m, jnp.maximum(s[:, :LANES], s[:, LANES:]))
            m_sel_ref[g] = m

    def sel_weights(j, n_tiles):
        off = pl.multiple_of(j * NTK, NTK)
        width = n_tiles * NTK
        v = jnp.concatenate([vs_ref[0, pl.ds(off, width), :], jnp.ones((width, LANES), BF16)], axis=1)
        ps = []
        for g in groups:
            m = m_sel_ref[g]
            ps.append(jnp.exp(s_sel_ref[g, :, pl.ds(off, width)]
                              - jnp.concatenate([m] * (width // LANES), axis=1)).astype(BF16))
        for g in groups:
            acc_sel_ref[g] += _dot(ps[g], v)

    def sweep(tile_fn):
        def double(d, c):
            tile_fn(2 * d, 2)
            return c

        def single(_, c):
            tile_fn(i, 1)
            return c

        lax.fori_loop(0, n_double, double, 0)
        lax.fori_loop(0, (i + 1) % 2, single, 0)

    sweep(sel_logits)
    for g in groups:
        m_sel_ref[g] = jnp.broadcast_to(jnp.max(m_sel_ref[g], axis=-1, keepdims=True), (rows, LANES))
    acc_sel_ref[...] = jnp.zeros(acc_sel_ref.shape, F32)
    sweep(sel_weights)

    for g in groups:
        o_sel = acc_sel_ref[g, :, :LANES] * (1.0 / acc_sel_ref[g, :, LANES:])
        o = ocmp_ref[g] + gsel_ref[g] * o_sel + owin_ref[g]
        mine = (lane // HEAD_DIM) == g
        ssq = jnp.sum(jnp.where(mine, o * o, 0.0), axis=-1, keepdims=True)
        y = o * lax.rsqrt(ssq * (1.0 / HEAD_DIM) + EPS)
        left = lax.broadcasted_iota(jnp.int32, (NTQ, LANES), 1) < HEAD_DIM
        for pair in range(NSA_REP // 2):
            even = y[(2 * pair) * NTQ:(2 * pair + 1) * NTQ]
            odd = y[(2 * pair + 1) * NTQ:(2 * pair + 2) * NTQ]
            if g == 0:
                odd = pltpu.roll(odd, HEAD_DIM, axis=1)
            else:
                even = pltpu.roll(even, HEAD_DIM, axis=1)
            cb = g * (NSA_REP // 2) + pair
            packed = jnp.where(left, even, odd) * normw_ref[:, cb * LANES:(cb + 1) * LANES]
            o_ref[0, :, cb * LANES:(cb + 1) * LANES] = packed.astype(o_ref.dtype)


def _nsa(qn, kv, kcmp, vcmp, gates, biasc, biass, biasw, ovl, blockneg, normw, b, t):
    nq = t // NTQ
    rows = NSA_REP * NTQ
    qn3 = qn.reshape(b, t, qn.shape[-1])
    kv3 = kv.reshape(b, t, kv.shape[-1])
    gates3 = gates.reshape(b, t, LANES)
    kv_spec = lambda c: pl.BlockSpec((1, t, LANES), lambda bi, qi, c=c: (bi, 0, c))
    const = lambda a: pl.BlockSpec(a.shape, lambda bi, qi: (0,) * a.ndim, pipeline_mode=pl.Buffered(1))
    return pl.pallas_call(
        _nsa_kernel,
        grid=(b, nq),
        in_specs=[
            pl.BlockSpec((1, NTQ, qn3.shape[-1]), lambda bi, qi: (bi, qi, 0)),
            kv_spec(0), kv_spec(1), kv_spec(2), kv_spec(3),
            pl.BlockSpec((1,) + kcmp.shape[1:], lambda bi, qi: (bi, 0, 0)),
            pl.BlockSpec((1,) + vcmp.shape[1:], lambda bi, qi: (bi, 0, 0)),
            pl.BlockSpec((1, NTQ, LANES), lambda bi, qi: (bi, qi, 0)),
            pl.BlockSpec((NSA_GROUPS, 1, rows, LANES), lambda bi, qi: (0, qi, 0, 0)),
            const(biass), const(biasw), const(ovl), const(blockneg), const(normw),
        ],
        out_specs=pl.BlockSpec((1, NTQ, NSA_HEADS * HEAD_DIM), lambda bi, qi: (bi, qi, 0)),
        out_shape=jax.ShapeDtypeStruct((b, t, NSA_HEADS * HEAD_DIM), BF16),
        scratch_shapes=[
            pltpu.VMEM((NSA_GROUPS, rows, t), F32),
            pltpu.VMEM((NSA_GROUPS, rows, WIN_TILES * NTK), F32),
            pltpu.VMEM((NSA_GROUPS, rows, LANES), F32),
            pltpu.VMEM((NSA_GROUPS, rows, 2 * LANES), F32),
            pltpu.VMEM((NSA_GROUPS, rows, LANES), F32),
            pltpu.VMEM((NSA_GROUPS, rows, LANES), F32),
            pltpu.VMEM((NSA_GROUPS, rows, 2 * LANES), BF16),
            pltpu.VMEM((NSA_GROUPS, rows, LANES), F32),
        ],
        compiler_params=_params("parallel", "arbitrary"),
        name="nsa",
    )(qn3, kv3, kv3, kv3, kv3, kcmp, vcmp, gates3, biasc, biass, biasw, ovl, blockneg, normw)


def _sb_kernel(q_ref, k_ref, v_ref, tri_ref, normw_ref, o_ref, acc_ref, run_ref):
    i = pl.program_id(1)
    rows = 2 * TQ
    n_pairs = SB_HEADS // 2
    lane = lax.broadcasted_iota(jnp.int32, (rows, LANES), 1)
    row = lax.broadcasted_iota(jnp.int32, (rows, LANES), 0)
    strict = lax.broadcasted_iota(jnp.int32, (rows, TK), 1) < (
        lax.broadcasted_iota(jnp.int32, (rows, TK), 0) & (TQ - 1))
    mine = (lane // HEAD_DIM) == (row // TQ)
    left = lax.broadcasted_iota(jnp.int32, (TQ, LANES), 1) < HEAD_DIM

    def tile(j, diag):
        off = pl.multiple_of(j * TK, TK)
        cols = [slice(p * LANES, (p + 1) * LANES) for p in range(n_pairs)]
        zs, csums = {}, {}

        def scores(p):
            q2 = q_ref[0, :, cols[p]]
            zero = jnp.zeros_like(q2)
            qp = jnp.concatenate([jnp.where(left, q2, zero), jnp.where(left, zero, q2)],
                                 axis=0)
            zs[p] = _dot_nt(qp, k_ref[0, pl.ds(off, TK), cols[p]])

        def suffix_sums(p):
            z = zs[p]
            neg_abs = lax.bitcast_convert_type(
                lax.bitcast_convert_type(z, jnp.uint32) | jnp.uint32(0x80000000), F32)
            sp = jnp.maximum(z, 0.0) + jnp.log(1.0 + jnp.exp(neg_abs))
            if diag:
                sp = jnp.where(strict, sp, 0.0)
            csums[p] = _dot(sp.astype(BF16), tri_ref[...])

        def weights_times_v(p):
            v = v_ref[0, pl.ds(off, TK), cols[p]]
            z, csum = zs.pop(p), csums.pop(p)
            tile_sum = jnp.broadcast_to(csum[:, 0:1], (rows, LANES))
            if diag:
                a = jnp.where(strict, jnp.exp(z - csum), 0.0)
                acc_ref[p] = _dot(a.astype(BF16), v)
                run_ref[p] = tile_sum
            else:
                run = run_ref[p]
                a = jnp.exp(z - csum - jnp.concatenate([run] * (TK // LANES), axis=1))
                acc_ref[p] += _dot(a.astype(BF16), v)
                run_ref[p] = run + tile_sum

        stages = (scores, suffix_sums, weights_times_v)
        for tick in range(n_pairs + len(stages) - 1):
            for k, stage in enumerate(stages):
                if 0 <= tick - k < n_pairs:
                    stage(tick - k)

    tile(i, True)

    def body(step, carry):
        tile(i - 1 - step, False)
        return carry

    lax.fori_loop(0, i, body, 0)

    for pair in range(n_pairs):
        cols = slice(pair * LANES, (pair + 1) * LANES)
        acc = acc_ref[pair]
        ssq = jnp.sum(jnp.where(mine, acc * acc, 0.0), axis=-1, keepdims=True)
        y = acc * lax.rsqrt(ssq * (1.0 / HEAD_DIM) + EPS)
        packed = jnp.where(left, y[:TQ], y[TQ:]) * normw_ref[:, cols]
        o_ref[0, :, cols] = packed.astype(o_ref.dtype)


def _sb(q, k, v, tri, normw, b, t):
    q3 = q.reshape(b, t, q.shape[-1])
    k3 = k.reshape(b, t, k.shape[-1])
    v3 = v.reshape(b, t, v.shape[-1])
    width = SB_HEADS * HEAD_DIM
    return pl.pallas_call(
        _sb_kernel,
        grid=(b, t // TQ),
        in_specs=[
            pl.BlockSpec((1, TQ, q3.shape[-1]), lambda bi, qi: (bi, qi, 0)),
            pl.BlockSpec((1, t, width), lambda bi, qi: (bi, 0, 0)),
            pl.BlockSpec((1, t, width), lambda bi, qi: (bi, 0, 0)),
            pl.BlockSpec(tri.shape, lambda bi, qi: (0, 0)),
            pl.BlockSpec(normw.shape, lambda bi, qi: (0, 0)),
        ],
        out_specs=pl.BlockSpec((1, TQ, width), lambda bi, qi: (bi, qi, 0)),
        out_shape=jax.ShapeDtypeStruct((b, t, width), BF16),
        scratch_shapes=[pltpu.VMEM((SB_HEADS // 2, 2 * TQ, LANES), F32)] * 2,
        compiler_params=_params("parallel", "arbitrary"),
        name="stickbreak",
    )(q3, k3, v3, tri, normw)


def _mixffn_kernel(x_ref, on_ref, os_ref, xh_ref, onh_ref, osh_ref, wn_ref, ws_ref, n2_ref,
                   wup_ref, cw_ref, cb_ref, wdown_ref, fw_ref, o_ref, up_ref, act_ref, *, d_ff):
    i = pl.program_id(1)
    tile = x_ref.shape[1]

    def mix(x, o_nsa, o_sb):
        h = x + _dot(o_nsa, wn_ref[...]) + _dot(o_sb, ws_ref[...])
        u = h * lax.rsqrt(jnp.mean(h * h, axis=-1, keepdims=True) + EPS) * n2_ref[...]
        return h, u.astype(BF16)

    h, u = mix(x_ref[0], on_ref[0], os_ref[0])
    _, u_halo = mix(xh_ref[0], onh_ref[0], osh_ref[0])
    halo = jnp.where(i > 0, u_halo, jnp.zeros_like(u_halo))
    ue = jnp.concatenate([halo, u], axis=0)

    def conv(slot, c0):
        w = cw_ref[:, c0:c0 + FFN_CHUNK]
        taps = [up_ref[slot, HALO - k:HALO - k + tile] for k in range(CONV_WIDTH)]
        return (w[2:3] * taps[0] + w[1:2] * taps[1] + w[0:1] * taps[2]
                + cb_ref[:, c0:c0 + FFN_CHUNK])

    def up_project(c):
        slot = 2 * (c % 2)
        up_ref[slot] = _dot(ue, wup_ref[:, c * FFN_CHUNK:(c + 1) * FFN_CHUNK])
        up_ref[slot + 1] = _dot(ue, wup_ref[:, d_ff + c * FFN_CHUNK:d_ff + (c + 1) * FFN_CHUNK])

    n_chunks = d_ff // FFN_CHUNK
    up_project(0)
    for c in range(n_chunks):
        g0, v0 = c * FFN_CHUNK, d_ff + c * FFN_CHUNK
        slot = 2 * (c % 2)
        if c + 1 < n_chunks:
            up_project(c + 1)
        gate = conv(slot, g0)
        val = conv(slot + 1, v0)
        act = gate * (1.0 / (1.0 + jnp.exp(-gate))) * val
        act_ref[:, g0:g0 + FFN_CHUNK] = act.astype(BF16)
    acc = h + _dot(act_ref[...], wdown_ref[...])
    y = acc * lax.rsqrt(jnp.mean(acc * acc, axis=-1, keepdims=True) + EPS) * fw_ref[...]
    o_ref[0] = y.astype(o_ref.dtype)


def _mixffn(x, o_nsa, o_sb, w_n, w_s, norm2_w, w_up, conv_w, conv_b, w_down, final_w):
    b, t, d = x.shape
    d_ff = w_down.shape[0]
    per = ROW_TILE // HALO
    tile = lambda a: pl.BlockSpec((1, ROW_TILE, a.shape[-1]), lambda bi, ti: (bi, ti, 0))
    halo = lambda a: pl.BlockSpec((1, HALO, a.shape[-1]), lambda bi, ti: (bi, jnp.maximum(ti * per - 1, 0), 0))
    const = lambda a: pl.BlockSpec(a.shape, lambda bi, ti: (0, 0), pipeline_mode=pl.Buffered(1))
    weights = (w_n, w_s, norm2_w, w_up, conv_w, conv_b, w_down, final_w)
    return pl.pallas_call(
        functools.partial(_mixffn_kernel, d_ff=d_ff),
        grid=(b, t // ROW_TILE),
        in_specs=[tile(x), tile(o_nsa), tile(o_sb), halo(x), halo(o_nsa), halo(o_sb)]
                 + [const(w) for w in weights],
        out_specs=pl.BlockSpec((1, ROW_TILE, d), lambda bi, ti: (bi, ti, 0)),
        out_shape=jax.ShapeDtypeStruct((b, t, d), F32),
        scratch_shapes=[pltpu.VMEM((4, HALO + ROW_TILE, FFN_CHUNK), F32),
                        pltpu.VMEM((ROW_TILE, d_ff), BF16)],
        compiler_params=_params("parallel", "arbitrary"),
        name="mixffn",
    )(x, o_nsa, o_sb, x, o_nsa, o_sb, *weights)


def _pad_half(w, half):
    z = jnp.zeros_like(w)
    return jnp.concatenate([w, z] if half == 0 else [z, w], axis=-1)


def _input_weights(w):
    d = w.shape[0]
    nq, kvw = NSA_HEADS * HEAD_DIM, NSA_GROUPS * HEAD_DIM
    scale = HEAD_DIM ** -0.5
    o = 0
    q_n = w[:, o:o + nq] * scale; o += nq
    kc = w[:, o:o + kvw]; o += kvw
    vc = w[:, o:o + kvw]; o += kvw
    ks = w[:, o:o + kvw]; o += kvw
    vs = w[:, o:o + kvw]; o += kvw
    kw = w[:, o:o + kvw]; o += kvw
    vw = w[:, o:o + kvw]; o += kvw
    gl = w[:, o:o + NSA_HEADS * 3]; o += NSA_HEADS * 3
    sbw = SB_HEADS * HEAD_DIM
    q_s = w[:, o:o + sbw] * scale; o += sbw
    k_s = w[:, o:o + sbw]; o += sbw
    v_s = w[:, o:o + sbw]; o += sbw
    q_n = q_n.reshape(d, NSA_GROUPS, NSA_REP, HEAD_DIM)
    q_n = jnp.concatenate([_pad_half(q_n[:, g], g) for g in range(NSA_GROUPS)], axis=1).reshape(d, -1)
    gl = jnp.pad(gl, ((0, 0), (0, LANES - gl.shape[1])))
    pieces = (q_n, jnp.concatenate([kc, vc], axis=1), jnp.concatenate([ks, vs, kw, vw], axis=1),
              gl, q_s, k_s, v_s)
    return jnp.concatenate(pieces, axis=1).astype(BF16), tuple(p.shape[1] for p in pieces)


def _bias_tables(rel_bias, t):
    nq = t // NTQ
    rb = rel_bias.T.astype(F32)

    def by_signed(d, ok):
        onehot = (_t5_bucket_np(d).reshape(1, -1) == np.arange(N_BUCKETS)[:, None]) & ok.reshape(1, -1)
        vals = jnp.dot(rb, jnp.asarray(onehot, F32), precision=lax.Precision.HIGHEST)
        vals = vals + jnp.asarray(np.where(ok.reshape(1, -1), 0.0, NEG_INF), F32)
        return vals.reshape((rb.shape[0],) + d.shape)

    def toeplitz(d, ok, n_rows, n_cols):
        p = d.shape[-1]
        assert p == n_rows + n_cols - 1 and n_rows > 1
        perm = (n_cols - 1 - np.arange(p)) % p
        w = by_signed(d[..., perm], ok[..., perm])
        reps = (1,) * (w.ndim - 1) + (n_rows,)
        h = jnp.tile(w, reps)[..., :n_rows * (p - 1)].reshape(w.shape[:-1] + (n_rows, p - 1))
        return h[..., :n_cols]

    def stacked(tb):
        return tb.reshape(NSA_GROUPS, NSA_REP * NTQ, tb.shape[-1])

    span = WINDOW + NTQ
    dw = np.arange(NTQ + span - 1) - (span - 1) + WINDOW
    biasw = stacked(toeplitz(dw, (dw >= 0) & (dw < WINDOW), NTQ, span))
    biasw = jnp.concatenate([biasw, jnp.full(biasw.shape[:-1] + (NTK,), NEG_INF, F32)], axis=-1)
    assert NTK > MAX_DISTANCE
    ds = np.arange(NTQ + 3 * NTK - 1) - (3 * NTK - 1) + 2 * NTK
    biass = stacked(toeplitz(ds, ds >= 0, NTQ, 3 * NTK))
    n_cmp = (t - CMP_BLOCK) // CMP_STRIDE + 1
    n_a = t // CMP_STRIDE
    dc = (CMP_STRIDE * (np.arange(n_a + LANES - 1)[None, :] - (LANES - 1))
          + np.arange(CMP_STRIDE)[:, None] - (CMP_BLOCK - 1))
    bc = toeplitz(dc, dc >= 0, n_a, LANES)
    bc = jnp.where(jnp.asarray(np.arange(LANES) < n_cmp), bc, NEG_INF)
    bc = bc.transpose(0, 2, 1, 3).reshape(NSA_GROUPS, NSA_REP, nq, NTQ, LANES)
    biasc = bc.transpose(0, 2, 1, 3, 4).reshape(NSA_GROUPS, nq, NSA_REP * NTQ, LANES)
    return biasc, biass, biasw


def _suffix_sum_matrix():
    return jnp.asarray(np.arange(TK)[:, None] >= np.arange(TK)[None, :], BF16)


def _selection_constants(t):
    n_cmp = (t - CMP_BLOCK) // CMP_STRIDE + 1
    n_sel = t // SEL_BLOCK
    n_selp = -(-n_sel // 8) * 8
    n = np.arange(LANES)[None, :]
    j = np.arange(n_selp)[:, None]
    ovl = ((CMP_STRIDE * n < SEL_BLOCK * (j + 1)) & (CMP_STRIDE * n + CMP_BLOCK > SEL_BLOCK * j)
           & (n < n_cmp) & (j < n_sel))
    ovl = np.concatenate([ovl, np.ones((1, LANES), bool), np.zeros((7, LANES), bool)], axis=0)
    own = (np.arange(t)[:, None] // SEL_BLOCK) == np.arange(LANES)[None, :]
    blockneg = jnp.where(jnp.asarray(own), NEG_INF, 0.0).astype(BF16)
    return jnp.asarray(ovl, BF16), blockneg


def kernel(x, norm1_w, w_in, cmp_pos_k, cmp_pos_v, cmp_k_w1, cmp_k_w2, cmp_v_w1, cmp_v_w2, gate_b,
           nsa_out_norm_w, sb_out_norm_w, w_out, norm2_w, w_up, conv_w, conv_b, w_down, rel_bias,
           final_norm_w):
    b, t, d = x.shape
    assert t % ROW_TILE == 0 and t % NTQ == 0 and t >= WINDOW + NTQ and (b * t) % ROW_TILE == 0
    assert (t - CMP_BLOCK) // CMP_STRIDE + 1 < LANES + 1 and t // CMP_STRIDE == LANES
    assert w_in.shape[0] == 1, "single-layer block: the closing norm is fused into the FFN kernel"
    half = CMP_STRIDE * HEAD_DIM
    nsa_w = NSA_HEADS * HEAD_DIM
    l = 0

    biasc, biass, biasw = _bias_tables(rel_bias, t)
    ovl, blockneg = _selection_constants(t)
    tri = _suffix_sum_matrix()

    h = x.reshape(b * t, d)
    w_all, splits = _input_weights(w_in[l])
    gb = jnp.pad(gate_b[l], (0, LANES - gate_b.shape[1])).reshape(1, LANES)
    qn, kcvc, kv, gates, sbq, sbk, sbv = _inproj(h, norm1_w[l].reshape(1, d), w_all, gb, splits)

    chunks = kcvc.reshape(b, t // CMP_STRIDE, CMP_STRIDE, 4, HEAD_DIM)
    chunks = chunks.transpose(0, 3, 1, 2, 4).reshape(b, 4, t // CMP_STRIDE, half)
    pos = jnp.stack([cmp_pos_k[l].reshape(2, half), cmp_pos_v[l].reshape(2, half)])
    w1 = jnp.stack([cmp_k_w1[l].reshape(2, half, -1), cmp_v_w1[l].reshape(2, half, -1)]).astype(BF16)
    w2 = jnp.stack([jnp.stack([_pad_half(w, g) for g in range(NSA_GROUPS)])
                    for w in (cmp_k_w2[l], cmp_v_w2[l])]).astype(BF16)
    kcmp, vcmp = _compress(chunks, pos, w1, w2)

    o_nsa = _nsa(qn, kv, kcmp, vcmp, gates, biasc, biass, biasw, ovl, blockneg,
                 nsa_out_norm_w[l].reshape(1, nsa_w), b, t)
    o_sb = _sb(sbq, sbk, sbv, tri, sb_out_norm_w[l].reshape(1, -1), b, t)

    wo = w_out[l].astype(BF16)
    return _mixffn(x, o_nsa, o_sb, wo[:nsa_w], wo[nsa_w:], norm2_w[l].reshape(1, d),
                   w_up[l].astype(BF16), conv_w[l], conv_b[l].reshape(1, -1),
                   w_down[l].astype(BF16), final_norm_w.reshape(1, d))
```

```python
import functools
import math

import jax
import jax.numpy as jnp
import numpy as np
from jax import lax
from jax.experimental import pallas as pl
from jax.experimental.pallas import tpu as pltpu

F32 = jnp.float32
BF16 = jnp.bfloat16

HEAD_DIM = 64
NSA_HEADS = 8
SB_HEADS = 8
NSA_GROUPS = 2
NSA_REP = NSA_HEADS // NSA_GROUPS
CMP_BLOCK = 32
CMP_STRIDE = 16
SEL_BLOCK = 64
SEL_TOP = 8
WINDOW = 512
N_BUCKETS = 32
MAX_DISTANCE = 128
CONV_WIDTH = 3
EPS = 1e-6
NEG_INF = -1e30
FORCED_BONUS = 1e6
TINY = 1e-30

LANES = 128
TQ = 256
TK = 256
NTQ = 256
NTK = 256
ROW_TILE = 512
FFN_CHUNK = 256
HALO = 16
VMEM_LIMIT = 56 * 1024 * 1024
WIN_TILES = WINDOW // NTK + 1


def _dot(a, b):
    return jnp.dot(a, b, preferred_element_type=F32)


def _dot_nt(a, b):
    return lax.dot_general(a, b, (((1,), (1,)), ((), ())), preferred_element_type=F32)


def _params(*sem):
    return pltpu.CompilerParams(dimension_semantics=sem, vmem_limit_bytes=VMEM_LIMIT)


def _t5_bucket_np(dist):
    n = np.maximum(dist, 0)
    max_exact = N_BUCKETS // 2
    nf = np.maximum(n, 1).astype(np.float32)
    log_b = max_exact + (np.log(nf / np.float32(max_exact)) / np.float32(math.log(MAX_DISTANCE / max_exact))
                         * np.float32(N_BUCKETS - max_exact)).astype(np.int32)
    log_b = np.minimum(log_b, N_BUCKETS - 1)
    return np.where(n < max_exact, n, log_b)


def _inproj_kernel(x_ref, nw_ref, w_ref, gb_ref, qn_ref, kcvc_ref, kv_ref, gates_ref,
                   sbq_ref, sbk_ref, sbv_ref, *, splits):
    x = x_ref[...]
    u = x * lax.rsqrt(jnp.mean(x * x, axis=-1, keepdims=True) + EPS) * nw_ref[...]
    u = u.astype(BF16)
    outs = (qn_ref, kcvc_ref, kv_ref, gates_ref, sbq_ref, sbk_ref, sbv_ref)
    start = 0
    for ref, size in zip(outs, splits):
        r = _dot(u, w_ref[:, start:start + size])
        if ref is gates_ref:
            r = jax.nn.sigmoid(r + gb_ref[...])
        ref[...] = r.astype(ref.dtype)
        start += size


def _inproj(x2, norm_w, w_all, gate_b_pad, splits):
    n, d = x2.shape
    dts = (BF16, F32, BF16, F32, BF16, BF16, BF16)
    return pl.pallas_call(
        functools.partial(_inproj_kernel, splits=splits),
        grid=(n // ROW_TILE,),
        in_specs=[
            pl.BlockSpec((ROW_TILE, d), lambda i: (i, 0)),
            pl.BlockSpec((1, d), lambda i: (0, 0)),
            pl.BlockSpec(w_all.shape, lambda i: (0, 0)),
            pl.BlockSpec((1, LANES), lambda i: (0, 0)),
        ],
        out_specs=[pl.BlockSpec((ROW_TILE, s), lambda i: (i, 0)) for s in splits],
        out_shape=[jax.ShapeDtypeStruct((n, s), dt) for s, dt in zip(splits, dts)],
        compiler_params=_params("parallel"),
        name="inproj",
    )(x2, norm_w, w_all, gate_b_pad)


def _gelu_tanh(x):
    return 0.5 * x * (1.0 + jnp.tanh(math.sqrt(2.0 / math.pi) * (x + 0.044715 * (x * x * x))))


def _compress_kernel(c_ref, pos_ref, w1_ref, w2_ref, kc_ref, vc_ref):
    n_chunk = c_ref.shape[2]
    row = lax.broadcasted_iota(jnp.int32, (n_chunk, LANES), 0)
    for kv, out_ref in ((0, kc_ref), (1, vc_ref)):
        acc = jnp.zeros((n_chunk, LANES), F32)
        for g in range(NSA_GROUPS):
            c = c_ref[0, 2 * kv + g]
            top = _dot((c + pos_ref[kv, 0:1, :]).astype(BF16), w1_ref[kv, 0])
            bot = _dot((c + pos_ref[kv, 1:2, :]).astype(BF16), w1_ref[kv, 1])
            hidden = top + pltpu.roll(bot, n_chunk - 1, axis=0)
            acc = acc + _dot(_gelu_tanh(hidden).astype(BF16), w2_ref[kv, g])
        out_ref[0] = jnp.where(row < n_chunk - 1, acc, 0.0).astype(out_ref.dtype)


def _compress(chunks, pos, w1, w2):
    b, _, n_chunk, width = chunks.shape
    hid = w1.shape[-1]
    out = jax.ShapeDtypeStruct((b, n_chunk, LANES), BF16)
    return pl.pallas_call(
        _compress_kernel,
        grid=(b,),
        in_specs=[
            pl.BlockSpec((1, 4, n_chunk, width), lambda i: (i, 0, 0, 0)),
            pl.BlockSpec((2, 2, width), lambda i: (0, 0, 0)),
            pl.BlockSpec((2, 2, width, hid), lambda i: (0, 0, 0, 0)),
            pl.BlockSpec((2, 2, hid, LANES), lambda i: (0, 0, 0, 0)),
        ],
        out_specs=[pl.BlockSpec((1, n_chunk, LANES), lambda i: (i, 0, 0))] * 2,
        out_shape=[out, out],
        compiler_params=_params("parallel"),
        name="compress",
    )(chunks, pos, w1, w2)


def _nsa_kernel(q_ref, ks_ref, vs_ref, kw_ref, vw_ref, kc_ref, vc_ref, gates_ref,
                biasc_ref, biass_ref, biasw_ref, ovl_ref, blockneg_ref, normw_ref,
                o_ref, s_sel_ref, s_win_ref, m_sel_ref, acc_sel_ref, owin_ref, gsel_ref, qsel_ref, ocmp_ref):
    i = pl.program_id(1)
    t0 = i * NTQ
    rows = NSA_REP * NTQ
    groups = range(NSA_GROUPS)
    lane = lax.broadcasted_iota(jnp.int32, (rows, LANES), 1)
    ones_v = jnp.ones((NTK, LANES), BF16)
    gates = gates_ref[0]

    n_selp = ovl_ref.shape[0] - 8
    blk = lax.broadcasted_iota(jnp.int32, (n_selp, NTQ), 0)
    cur = (t0 + lax.broadcasted_iota(jnp.int32, (n_selp, NTQ), 1)) // SEL_BLOCK
    causal_blk = blk <= cur
    forced = (blk == 0) | (blk == cur) | (blk == cur - 1)

    def probabilities(s, m):
        return jnp.exp(s - jnp.concatenate([m, m], axis=1)).astype(BF16)

    qgs = [jnp.concatenate(
        [q_ref[0, :, (g * NSA_REP + r) * LANES:(g * NSA_REP + r + 1) * LANES] for r in range(NSA_REP)],
        axis=0) for g in groups]

    lcs = [_dot_nt(qgs[g], kc_ref[0]) + biasc_ref[g, 0] for g in groups]

    win_tiles = []
    for k in range(WIN_TILES):
        j = i - (WIN_TILES - 1) + k
        win_tiles.append((pl.multiple_of(jnp.maximum(j, 0) * NTK, NTK),
                          pl.multiple_of(jnp.where(j < 0, WIN_TILES, k) * NTK, NTK)))
    m_win = [jnp.full((rows, LANES), NEG_INF, F32) for g in groups]
    for k, (off, boff) in enumerate(win_tiles):
        key = kw_ref[0, pl.ds(off, NTK), :]
        raw = [_dot_nt(qgs[g], key) for g in groups]
        for g in groups:
            s = raw[g] + biasw_ref[g, :, pl.ds(boff, NTK)]
            s_win_ref[g, :, k * NTK:(k + 1) * NTK] = s
            m_win[g] = jnp.maximum(m_win[g], jnp.maximum(s[:, :LANES], s[:, LANES:]))

    ecs = []
    for g in groups:
        mc = jnp.maximum(jnp.max(lcs[g], axis=-1, keepdims=True), 0.1 * NEG_INF)
        ec = jnp.exp(lcs[g] - mc).astype(BF16)
        oc = _dot(ec, jnp.concatenate([vc_ref[0], ones_v[:LANES]], axis=1))
        ocmp_ref[g] = oc[:, :LANES] * (1.0 / jnp.maximum(oc[:, LANES:], TINY))
        ecs.append(ec)

    for g in groups:
        imp4 = _dot_nt(ovl_ref[...], ecs[g])
        imp4 = imp4[:n_selp] * (1.0 / jnp.maximum(imp4[n_selp:n_selp + 1], TINY))
        imp = imp4[:, 0:NTQ]
        for r in range(1, NSA_REP):
            imp = imp + imp4[:, r * NTQ:(r + 1) * NTQ]
        score = jnp.where(causal_blk, imp + jnp.where(forced, FORCED_BONUS, 0.0), NEG_INF)
        rank = jnp.zeros((n_selp, NTQ), F32)
        for j in range(n_selp):
            other = score[j:j + 1, :]
            ahead = (other > score) | ((other == score) & (blk > j))
            rank = rank + jnp.where(ahead, 1.0, 0.0)
        unsel_t = jnp.where((rank < SEL_TOP) & causal_blk, 0.0, 1.0)
        unsel_t = jnp.concatenate([unsel_t, jnp.zeros((LANES - n_selp, NTQ), F32)], axis=0)
        unsel = unsel_t.T.astype(BF16)

        qsel_ref[g] = jnp.concatenate([qgs[g], jnp.concatenate([unsel] * NSA_REP, axis=0)], axis=1)

    for g in groups:
        m_win[g] = jnp.broadcast_to(jnp.max(m_win[g], axis=-1, keepdims=True), (rows, LANES))
    acc_win = [jnp.zeros((rows, 2 * LANES), F32) for g in groups]
    for k, (off, _) in enumerate(win_tiles):
        v = jnp.concatenate([vw_ref[0, pl.ds(off, NTK), :], ones_v], axis=1)
        ps = [probabilities(s_win_ref[g, :, k * NTK:(k + 1) * NTK], m_win[g]) for g in groups]
        for g in groups:
            acc_win[g] = acc_win[g] + _dot(ps[g], v)

    def gate(g, branch):
        return jnp.concatenate(
            [gates[:, (g * NSA_REP + r) * 3 + branch:(g * NSA_REP + r) * 3 + branch + 1]
             for r in range(NSA_REP)], axis=0)

    for g in groups:
        ocmp_ref[g] = gate(g, 0) * ocmp_ref[g]
        gsel_ref[g] = jnp.broadcast_to(gate(g, 1), (rows, LANES))
        owin_ref[g] = gate(g, 2) * (acc_win[g][:, :LANES] * (1.0 / acc_win[g][:, LANES:]))

    m_sel_ref[...] = jnp.full(m_sel_ref.shape, NEG_INF, F32)

    n_double = (i + 1) // 2

    def sel_logits(j, n_tiles):
        off = pl.multiple_of(j * NTK, NTK)
        width = n_tiles * NTK
        key = jnp.concatenate([ks_ref[0, pl.ds(off, width), :], blockneg_ref[pl.ds(off, width), :]], axis=1)
        raw = [_dot_nt(qsel_ref[g], key) for g in groups]
        for g in groups:
            m = m_sel_ref[g]
            for k in range(n_tiles):
                boff = pl.multiple_of((2 - jnp.minimum(i - j - k, 2)) * NTK, NTK)
                s = raw[g][:, k * NTK:(k + 1) * NTK] + biass_ref[g, :, pl.ds(boff, NTK)]
                s_sel_ref[g, :, pl.ds(pl.multiple_of(off + k * NTK, NTK), NTK)] = s
                m = jnp.maximum(m, jnp.maximum(s[:, :LANES], s[:, LANES:]))
            m_sel_ref[g] = m

    def sel_weights(j, n_tiles):
        off = pl.multiple_of(j * NTK, NTK)
        width = n_tiles * NTK
        v = jnp.concatenate([vs_ref[0, pl.ds(off, width), :], jnp.ones((width, LANES), BF16)], axis=1)
        ps = []
        for g in groups:
            m = m_sel_ref[g]
            ps.append(jnp.exp(s_sel_ref[g, :, pl.ds(off, width)]
                              - jnp.concatenate([m] * (width // LANES), axis=1)).astype(BF16))
        for g in groups:
            acc_sel_ref[g] += _dot(ps[g], v)

    def sweep(tile_fn):
        def double(d, c):
            tile_fn(2 * d, 2)
            return c

        def single(_, c):
            tile_fn(i, 1)
            return c

        lax.fori_loop(0, n_double, double, 0)
        lax.fori_loop(0, (i + 1) % 2, single, 0)

    sweep(sel_logits)
    for g in groups:
        m_sel_ref[g] = jnp.broadcast_to(jnp.max(m_sel_ref[g], axis=-1, keepdims=True), (rows, LANES))
    acc_sel_ref[...] = jnp.zeros(acc_sel_ref.shape, F32)
    sweep(sel_weights)

    for g in groups:
        o_sel = acc_sel_ref[g, :, :LANES] * (1.0 / acc_sel_ref[g, :, LANES:])
        o = ocmp_ref[g] + gsel_ref[g] * o_sel + owin_ref[g]
        mine = (lane // HEAD_DIM) == g
        ssq = jnp.sum(jnp.where(mine, o * o, 0.0), axis=-1, keepdims=True)
        y = o * lax.rsqrt(ssq * (1.0 / HEAD_DIM) + EPS)
        left = lax.broadcasted_iota(jnp.int32, (NTQ, LANES), 1) < HEAD_DIM
        for pair in range(NSA_REP // 2):
            even = y[(2 * pair) * NTQ:(2 * pair + 1) * NTQ]
            odd = y[(2 * pair + 1) * NTQ:(2 * pair + 2) * NTQ]
            if g == 0:
                odd = pltpu.roll(odd, HEAD_DIM, axis=1)
            else:
                even = pltpu.roll(even, HEAD_DIM, axis=1)
            cb = g * (NSA_REP // 2) + pair
            packed = jnp.where(left, even, odd) * normw_ref[:, cb * LANES:(cb + 1) * LANES]
            o_ref[0, :, cb * LANES:(cb + 1) * LANES] = packed.astype(o_ref.dtype)


def _nsa(qn, kv, kcmp, vcmp, gates, biasc, biass, biasw, ovl, blockneg, normw, b, t):
    nq = t // NTQ
    rows = NSA_REP * NTQ
    qn3 = qn.reshape(b, t, qn.shape[-1])
    kv3 = kv.reshape(b, t, kv.shape[-1])
    gates3 = gates.reshape(b, t, LANES)
    kv_spec = lambda c: pl.BlockSpec((1, t, LANES), lambda bi, qi, c=c: (bi, 0, c))
    const = lambda a: pl.BlockSpec(a.shape, lambda bi, qi: (0,) * a.ndim, pipeline_mode=pl.Buffered(1))
    return pl.pallas_call(
        _nsa_kernel,
        grid=(b, nq),
        in_specs=[
            pl.BlockSpec((1, NTQ, qn3.shape[-1]), lambda bi, qi: (bi, qi, 0)),
            kv_spec(0), kv_spec(1), kv_spec(2), kv_spec(3),
            pl.BlockSpec((1,) + kcmp.shape[1:], lambda bi, qi: (bi, 0, 0)),
            pl.BlockSpec((1,) + vcmp.shape[1:], lambda bi, qi: (bi, 0, 0)),
            pl.BlockSpec((1, NTQ, LANES), lambda bi, qi: (bi, qi, 0)),
            pl.BlockSpec((NSA_GROUPS, 1, rows, LANES), lambda bi, qi: (0, qi, 0, 0)),
            const(biass), const(biasw), const(ovl), const(blockneg), const(normw),
        ],
        out_specs=pl.BlockSpec((1, NTQ, NSA_HEADS * HEAD_DIM), lambda bi, qi: (bi, qi, 0)),
        out_shape=jax.ShapeDtypeStruct((b, t, NSA_HEADS * HEAD_DIM), BF16),
        scratch_shapes=[
            pltpu.VMEM((NSA_GROUPS, rows, t), F32),
            pltpu.VMEM((NSA_GROUPS, rows, WIN_TILES * NTK), F32),
            pltpu.VMEM((NSA_GROUPS, rows, LANES), F32),
            pltpu.VMEM((NSA_GROUPS, rows, 2 * LANES), F32),
            pltpu.VMEM((NSA_GROUPS, rows, LANES), F32),
            pltpu.VMEM((NSA_GROUPS, rows, LANES), F32),
            pltpu.VMEM((NSA_GROUPS, rows, 2 * LANES), BF16),
            pltpu.VMEM((NSA_GROUPS, rows, LANES), F32),
        ],
        compiler_params=_params("parallel", "arbitrary"),
        name="nsa",
    )(qn3, kv3, kv3, kv3, kv3, kcmp, vcmp, gates3, biasc, biass, biasw, ovl, blockneg, normw)


def _sb_kernel(q_ref, k_ref, v_ref, tri_ref, normw_ref, o_ref, acc_ref, run_ref):
    i = pl.program_id(1)
    rows = 2 * TQ
    n_pairs = SB_HEADS // 2
    lane = lax.broadcasted_iota(jnp.int32, (rows, LANES), 1)
    row = lax.broadcasted_iota(jnp.int32, (rows, LANES), 0)
    strict = lax.broadcasted_iota(jnp.int32, (rows, TK), 1) < (
        lax.broadcasted_iota(jnp.int32, (rows, TK), 0) & (TQ - 1))
    mine = (lane // HEAD_DIM) == (row // TQ)
    left = lax.broadcasted_iota(jnp.int32, (TQ, LANES), 1) < HEAD_DIM

    def tile(j, diag):
        off = pl.multiple_of(j * TK, TK)
        cols = [slice(p * LANES, (p + 1) * LANES) for p in range(n_pairs)]
        zs, csums = {}, {}

        def scores(p):
            q2 = q_ref[0, :, cols[p]]
            zero = jnp.zeros_like(q2)
            qp = jnp.concatenate([jnp.where(left, q2, zero), jnp.where(left, zero, q2)],
                                 axis=0)
            zs[p] = _dot_nt(qp, k_ref[0, pl.ds(off, TK), cols[p]])

        def suffix_sums(p):
            z = zs[p]
            neg_abs = lax.bitcast_convert_type(
                lax.bitcast_convert_type(z, jnp.uint32) | jnp.uint32(0x80000000), F32)
            sp = jnp.maximum(z, 0.0) + jnp.log(1.0 + jnp.exp(neg_abs))
            if diag:
                sp = jnp.where(strict, sp, 0.0)
            csums[p] = _dot(sp.astype(BF16), tri_ref[...])

        def weights_times_v(p):
            v = v_ref[0, pl.ds(off, TK), cols[p]]
            z, csum = zs.pop(p), csums.pop(p)
            tile_sum = jnp.broadcast_to(csum[:, 0:1], (rows, LANES))
            if diag:
                a = jnp.where(strict, jnp.exp(z - csum), 0.0)
                acc_ref[p] = _dot(a.astype(BF16), v)
                run_ref[p] = tile_sum
            else:
                run = run_ref[p]
                a = jnp.exp(z - csum - jnp.concatenate([run] * (TK // LANES), axis=1))
                acc_ref[p] += _dot(a.astype(BF16), v)
                run_ref[p] = run + tile_sum

        stages = (scores, suffix_sums, weights_times_v)
        for tick in range(n_pairs + len(stages) - 1):
            for k, stage in enumerate(stages):
                if 0 <= tick - k < n_pairs:
                    stage(tick - k)

    tile(i, True)

    def body(step, carry):
        tile(i - 1 - step, False)
        return carry

    lax.fori_loop(0, i, body, 0)

    for pair in range(n_pairs):
        cols = slice(pair * LANES, (pair + 1) * LANES)
        acc = acc_ref[pair]
        ssq = jnp.sum(jnp.where(mine, acc * acc, 0.0), axis=-1, keepdims=True)
        y = acc * lax.rsqrt(ssq * (1.0 / HEAD_DIM) + EPS)
        packed = jnp.where(left, y[:TQ], y[TQ:]) * normw_ref[:, cols]
        o_ref[0, :, cols] = packed.astype(o_ref.dtype)


def _sb(q, k, v, tri, normw, b, t):
    q3 = q.reshape(b, t, q.shape[-1])
    k3 = k.reshape(b, t, k.shape[-1])
    v3 = v.reshape(b, t, v.shape[-1])
    width = SB_HEADS * HEAD_DIM
    return pl.pallas_call(
        _sb_kernel,
        grid=(b, t // TQ),
        in_specs=[
            pl.BlockSpec((1, TQ, q3.shape[-1]), lambda bi, qi: (bi, qi, 0)),
            pl.BlockSpec((1, t, width), lambda bi, qi: (bi, 0, 0)),
            pl.BlockSpec((1, t, width), lambda bi, qi: (bi, 0, 0)),
            pl.BlockSpec(tri.shape, lambda bi, qi: (0, 0)),
            pl.BlockSpec(normw.shape, lambda bi, qi: (0, 0)),
        ],
        out_specs=pl.BlockSpec((1, TQ, width), lambda bi, qi: (bi, qi, 0)),
        out_shape=jax.ShapeDtypeStruct((b, t, width), BF16),
        scratch_shapes=[pltpu.VMEM((SB_HEADS // 2, 2 * TQ, LANES), F32)] * 2,
        compiler_params=_params("parallel", "arbitrary"),
        name="stickbreak",
    )(q3, k3, v3, tri, normw)


def _mixffn_kernel(x_ref, on_ref, os_ref, xh_ref, onh_ref, osh_ref, wn_ref, ws_ref, n2_ref,
                   wup_ref, cw_ref, cb_ref, wdown_ref, fw_ref, o_ref, up_ref, act_ref, *, d_ff):
    i = pl.program_id(1)
    tile = x_ref.shape[1]

    def mix(x, o_nsa, o_sb):
        h = x + _dot(o_nsa, wn_ref[...]) + _dot(o_sb, ws_ref[...])
        u = h * lax.rsqrt(jnp.mean(h * h, axis=-1, keepdims=True) + EPS) * n2_ref[...]
        return h, u.astype(BF16)

    h, u = mix(x_ref[0], on_ref[0], os_ref[0])
    _, u_halo = mix(xh_ref[0], onh_ref[0], osh_ref[0])
    halo = jnp.where(i > 0, u_halo, jnp.zeros_like(u_halo))
    ue = jnp.concatenate([halo, u], axis=0)

    def conv(slot, c0):
        w = cw_ref[:, c0:c0 + FFN_CHUNK]
        taps = [up_ref[slot, HALO - k:HALO - k + tile] for k in range(CONV_WIDTH)]
        return (w[2:3] * taps[0] + w[1:2] * taps[1] + w[0:1] * taps[2]
                + cb_ref[:, c0:c0 + FFN_CHUNK])

    def up_project(c):
        slot = 2 * (c % 2)
        up_ref[slot] = _dot(ue, wup_ref[:, c * FFN_CHUNK:(c + 1) * FFN_CHUNK])
        up_ref[slot + 1] = _dot(ue, wup_ref[:, d_ff + c * FFN_CHUNK:d_ff + (c + 1) * FFN_CHUNK])

    n_chunks = d_ff // FFN_CHUNK
    up_project(0)
    for c in range(n_chunks):
        g0, v0 = c * FFN_CHUNK, d_ff + c * FFN_CHUNK
        slot = 2 * (c % 2)
        if c + 1 < n_chunks:
            up_project(c + 1)
        gate = conv(slot, g0)
        val = conv(slot + 1, v0)
        act = gate * (1.0 / (1.0 + jnp.exp(-gate))) * val
        act_ref[:, g0:g0 + FFN_CHUNK] = act.astype(BF16)
    acc = h + _dot(act_ref[...], wdown_ref[...])
    y = acc * lax.rsqrt(jnp.mean(acc * acc, axis=-1, keepdims=True) + EPS) * fw_ref[...]
    o_ref[0] = y.astype(o_ref.dtype)


def _mixffn(x, o_nsa, o_sb, w_n, w_s, norm2_w, w_up, conv_w, conv_b, w_down, final_w):
    b, t, d = x.shape
    d_ff = w_down.shape[0]
    per = ROW_TILE // HALO
    tile = lambda a: pl.BlockSpec((1, ROW_TILE, a.shape[-1]), lambda bi, ti: (bi, ti, 0))
    halo = lambda a: pl.BlockSpec((1, HALO, a.shape[-1]), lambda bi, ti: (bi, jnp.maximum(ti * per - 1, 0), 0))
    const = lambda a: pl.BlockSpec(a.shape, lambda bi, ti: (0, 0), pipeline_mode=pl.Buffered(1))
    weights = (w_n, w_s, norm2_w, w_up, conv_w, conv_b, w_down, final_w)
    return pl.pallas_call(
        functools.partial(_mixffn_kernel, d_ff=d_ff),
        grid=(b, t // ROW_TILE),
        in_specs=[tile(x), tile(o_nsa), tile(o_sb), halo(x), halo(o_nsa), halo(o_sb)]
                 + [const(w) for w in weights],
        out_specs=pl.BlockSpec((1, ROW_TILE, d), lambda bi, ti: (bi, ti, 0)),
        out_shape=jax.ShapeDtypeStruct((b, t, d), F32),
        scratch_shapes=[pltpu.VMEM((4, HALO + ROW_TILE, FFN_CHUNK), F32),
                        pltpu.VMEM((ROW_TILE, d_ff), BF16)],
        compiler_params=_params("parallel", "arbitrary"),
        name="mixffn",
    )(x, o_nsa, o_sb, x, o_nsa, o_sb, *weights)


def _pad_half(w, half):
    z = jnp.zeros_like(w)
    return jnp.concatenate([w, z] if half == 0 else [z, w], axis=-1)


def _input_weights(w):
    d = w.shape[0]
    nq, kvw = NSA_HEADS * HEAD_DIM, NSA_GROUPS * HEAD_DIM
    scale = HEAD_DIM ** -0.5
    o = 0
    q_n = w[:, o:o + nq] * scale; o += nq
    kc = w[:, o:o + kvw]; o += kvw
    vc = w[:, o:o + kvw]; o += kvw
    ks = w[:, o:o + kvw]; o += kvw
    vs = w[:, o:o + kvw]; o += kvw
    kw = w[:, o:o + kvw]; o += kvw
    vw = w[:, o:o + kvw]; o += kvw
    gl = w[:, o:o + NSA_HEADS * 3]; o += NSA_HEADS * 3
    sbw = SB_HEADS * HEAD_DIM
    q_s = w[:, o:o + sbw] * scale; o += sbw
    k_s = w[:, o:o + sbw]; o += sbw
    v_s = w[:, o:o + sbw]; o += sbw
    q_n = q_n.reshape(d, NSA_GROUPS, NSA_REP, HEAD_DIM)
    q_n = jnp.concatenate([_pad_half(q_n[:, g], g) for g in range(NSA_GROUPS)], axis=1).reshape(d, -1)
    gl = jnp.pad(gl, ((0, 0), (0, LANES - gl.shape[1])))
    pieces = (q_n, jnp.concatenate([kc, vc], axis=1), jnp.concatenate([ks, vs, kw, vw], axis=1),
              gl, q_s, k_s, v_s)
    return jnp.concatenate(pieces, axis=1).astype(BF16), tuple(p.shape[1] for p in pieces)


def _toeplitz_kernel(w_ref, o_ref, *, n_cols):
    n_rows = o_ref.shape[2]
    for k in range(w_ref.shape[1]):
        x = jnp.broadcast_to(w_ref[0, k], (n_rows, w_ref.shape[-1]))
        o_ref[0, k, :, :n_cols] = pltpu.roll(x, 0, 1, stride=1, stride_axis=0)[:, :n_cols]
        if n_cols < o_ref.shape[-1]:
            o_ref[0, k, :, n_cols:] = jnp.full((n_rows, o_ref.shape[-1] - n_cols), NEG_INF, F32)


def _toeplitz_rows(w, n_rows, n_cols, neg_cols):
    h, k, width = w.shape
    return pl.pallas_call(
        functools.partial(_toeplitz_kernel, n_cols=n_cols),
        grid=(h,),
        in_specs=[pl.BlockSpec((1, k, 1, width), lambda i: (i, 0, 0, 0))],
        out_specs=pl.BlockSpec((1, k, n_rows, n_cols + neg_cols), lambda i: (i, 0, 0, 0)),
        out_shape=jax.ShapeDtypeStruct((h, k, n_rows, n_cols + neg_cols), F32),
        compiler_params=_params("parallel"),
        name="toeplitz",
    )(w.reshape(h, k, 1, width))


def _bias_tables(rel_bias, t):
    nq = t // NTQ
    rb = rel_bias.T.astype(F32)

    def by_signed(d, ok):
        onehot = (_t5_bucket_np(d).reshape(1, -1) == np.arange(N_BUCKETS)[:, None]) & ok.reshape(1, -1)
        vals = jnp.dot(rb, jnp.asarray(onehot, F32), precision=lax.Precision.HIGHEST)
        vals = vals + jnp.asarray(np.where(ok.reshape(1, -1), 0.0, NEG_INF), F32)
        return vals.reshape((rb.shape[0],) + d.shape)

    def toeplitz(d, ok, n_rows, n_cols, neg_cols=0):
        p = d.shape[-1]
        assert p == n_rows + n_cols - 1
        width = -(-p // LANES) * LANES
        src = (n_cols - 1 - np.arange(width)) % width
        w = by_signed(d[..., np.minimum(src, p - 1)], ok[..., np.minimum(src, p - 1)] & (src < p))
        return _toeplitz_rows(w.reshape(rb.shape[0], -1, width), n_rows, n_cols, neg_cols).reshape(
            w.shape[:-1] + (n_rows, n_cols + neg_cols))

    def stacked(tb):
        return tb.reshape(NSA_GROUPS, NSA_REP * NTQ, tb.shape[-1])

    span = WINDOW + NTQ
    dw = np.arange(NTQ + span - 1) - (span - 1) + WINDOW
    biasw = stacked(toeplitz(dw, (dw >= 0) & (dw < WINDOW), NTQ, span, neg_cols=NTK))
    assert NTK > MAX_DISTANCE
    ds = np.arange(NTQ + 3 * NTK - 1) - (3 * NTK - 1) + 2 * NTK
    biass = stacked(toeplitz(ds, ds >= 0, NTQ, 3 * NTK))
    n_cmp = (t - CMP_BLOCK) // CMP_STRIDE + 1
    n_a = t // CMP_STRIDE
    dc = (CMP_STRIDE * (np.arange(n_a + LANES - 1)[None, :] - (LANES - 1))
          + np.arange(CMP_STRIDE)[:, None] - (CMP_BLOCK - 1))
    bc = toeplitz(dc, dc >= 0, n_a, LANES)
    bc = jnp.where(jnp.asarray(np.arange(LANES) < n_cmp), bc, NEG_INF)
    bc = bc.transpose(0, 2, 1, 3).reshape(NSA_GROUPS, NSA_REP, nq, NTQ, LANES)
    biasc = bc.transpose(0, 2, 1, 3, 4).reshape(NSA_GROUPS, nq, NSA_REP * NTQ, LANES)
    return biasc, biass, biasw


def _suffix_sum_matrix():
    return jnp.asarray(np.arange(TK)[:, None] >= np.arange(TK)[None, :], BF16)


def _selection_constants(t):
    n_cmp = (t - CMP_BLOCK) // CMP_STRIDE + 1
    n_sel = t // SEL_BLOCK
    n_selp = -(-n_sel // 8) * 8
    n = np.arange(LANES)[None, :]
    j = np.arange(n_selp)[:, None]
    ovl = ((CMP_STRIDE * n < SEL_BLOCK * (j + 1)) & (CMP_STRIDE * n + CMP_BLOCK > SEL_BLOCK * j)
           & (n < n_cmp) & (j < n_sel))
    ovl = np.concatenate([ovl, np.ones((1, LANES), bool), np.zeros((7, LANES), bool)], axis=0)
    own = (np.arange(t)[:, None] // SEL_BLOCK) == np.arange(LANES)[None, :]
    blockneg = jnp.where(jnp.asarray(own), NEG_INF, 0.0).astype(BF16)
    return jnp.asarray(ovl, BF16), blockneg


def kernel(x, norm1_w, w_in, cmp_pos_k, cmp_pos_v, cmp_k_w1, cmp_k_w2, cmp_v_w1, cmp_v_w2, gate_b,
           nsa_out_norm_w, sb_out_norm_w, w_out, norm2_w, w_up, conv_w, conv_b, w_down, rel_bias,
           final_norm_w):
    b, t, d = x.shape
    assert t % ROW_TILE == 0 and t % NTQ == 0 and t >= WINDOW + NTQ and (b * t) % ROW_TILE == 0
    assert (t - CMP_BLOCK) // CMP_STRIDE + 1 < LANES + 1 and t // CMP_STRIDE == LANES
    assert w_in.shape[0] == 1, "single-layer block: the closing norm is fused into the FFN kernel"
    half = CMP_STRIDE * HEAD_DIM
    nsa_w = NSA_HEADS * HEAD_DIM
    l = 0

    biasc, biass, biasw = _bias_tables(rel_bias, t)
    ovl, blockneg = _selection_constants(t)
    tri = _suffix_sum_matrix()

    h = x.reshape(b * t, d)
    w_all, splits = _input_weights(w_in[l])
    gb = jnp.pad(gate_b[l], (0, LANES - gate_b.shape[1])).reshape(1, LANES)
    qn, kcvc, kv, gates, sbq, sbk, sbv = _inproj(h, norm1_w[l].reshape(1, d), w_all, gb, splits)

    chunks = kcvc.reshape(b, t // CMP_STRIDE, CMP_STRIDE, 4, HEAD_DIM)
    chunks = chunks.transpose(0, 3, 1, 2, 4).reshape(b, 4, t // CMP_STRIDE, half)
    pos = jnp.stack([cmp_pos_k[l].reshape(2, half), cmp_pos_v[l].reshape(2, half)])
    w1 = jnp.stack([cmp_k_w1[l].reshape(2, half, -1), cmp_v_w1[l].reshape(2, half, -1)]).astype(BF16)
    w2 = jnp.stack([jnp.stack([_pad_half(w, g) for g in range(NSA_GROUPS)])
                    for w in (cmp_k_w2[l], cmp_v_w2[l])]).astype(BF16)
    kcmp, vcmp = _compress(chunks, pos, w1, w2)

    o_nsa = _nsa(qn, kv, kcmp, vcmp, gates, biasc, biass, biasw, ovl, blockneg,
                 nsa_out_norm_w[l].reshape(1, nsa_w), b, t)
    o_sb = _sb(sbq, sbk, sbv, tri, sb_out_norm_w[l].reshape(1, -1), b, t)

    wo = w_out[l].astype(BF16)
    return _mixffn(x, o_nsa, o_sb, wo[:nsa_w], wo[nsa_w:], norm2_w[l].reshape(1, d),
                   w_up[l].astype(BF16), conv_w[l], conv_b[l].reshape(1, -1),
                   w_down[l].astype(BF16), final_norm_w.reshape(1, d))
```

```python
import functools
import math

import jax
import jax.numpy as jnp
import numpy as np
from jax import lax
from jax.experimental import pallas as pl
from jax.experimental.pallas import tpu as pltpu

F32 = jnp.float32
BF16 = jnp.bfloat16

HEAD_DIM = 64
NSA_HEADS = 8
SB_HEADS = 8
NSA_GROUPS = 2
NSA_REP = NSA_HEADS // NSA_GROUPS
CMP_BLOCK = 32
CMP_STRIDE = 16
SEL_BLOCK = 64
SEL_TOP = 8
WINDOW = 512
N_BUCKETS = 32
MAX_DISTANCE = 128
CONV_WIDTH = 3
EPS = 1e-6
NEG_INF = -1e30
FORCED_BONUS = 1e6
TINY = 1e-30

LANES = 128
TQ = 256
TK = 256
NTQ = 256
NTK = 256
ROW_TILE = 512
FFN_CHUNK = 256
HALO = 16
VMEM_LIMIT = 56 * 1024 * 1024
WIN_TILES = WINDOW // NTK + 1


def _dot(a, b):
    return jnp.dot(a, b, preferred_element_type=F32)


def _dot_nt(a, b):
    return lax.dot_general(a, b, (((1,), (1,)), ((), ())), preferred_element_type=F32)


def _params(*sem):
    return pltpu.CompilerParams(dimension_semantics=sem, vmem_limit_bytes=VMEM_LIMIT)


def _t5_bucket_np(dist):
    n = np.maximum(dist, 0)
    max_exact = N_BUCKETS // 2
    nf = np.maximum(n, 1).astype(np.float32)
    log_b = max_exact + (np.log(nf / np.float32(max_exact)) / np.float32(math.log(MAX_DISTANCE / max_exact))
                         * np.float32(N_BUCKETS - max_exact)).astype(np.int32)
    log_b = np.minimum(log_b, N_BUCKETS - 1)
    return np.where(n < max_exact, n, log_b)


def _inproj_kernel(x_ref, nw_ref, w_ref, gb_ref, qn_ref, kcvc_ref, kv_ref, gates_ref,
                   sbq_ref, sbk_ref, sbv_ref, stage_ref, *, splits):
    x = x_ref[...]
    u = x * lax.rsqrt(jnp.mean(x * x, axis=-1, keepdims=True) + EPS) * nw_ref[...]
    u = u.astype(BF16)
    outs = (qn_ref, kcvc_ref, kv_ref, gates_ref, sbq_ref, sbk_ref, sbv_ref)
    start = 0
    for ref, size in zip(outs, splits):
        r = _dot(u, w_ref[:, start:start + size])
        if ref is gates_ref:
            r = jax.nn.sigmoid(r + gb_ref[...])
        if ref is kcvc_ref:
            for c in range(size // LANES):
                stage_ref[c] = r[:, c * LANES:(c + 1) * LANES]
                for p in range(CMP_STRIDE):
                    ref[0, p, :, c * LANES:(c + 1) * LANES] = (
                        stage_ref[c, pl.ds(p, ROW_TILE // CMP_STRIDE, stride=CMP_STRIDE), :])
        else:
            ref[...] = r.astype(ref.dtype)
        start += size


def _inproj(x2, norm_w, w_all, gate_b_pad, splits, b, t):
    n, d = x2.shape
    dts = (BF16, F32, BF16, F32, BF16, BF16, BF16)
    per_seq = t // ROW_TILE
    chunks = ROW_TILE // CMP_STRIDE
    out_specs = [pl.BlockSpec((ROW_TILE, s), lambda i: (i, 0)) for s in splits]
    out_shape = [jax.ShapeDtypeStruct((n, s), dt) for s, dt in zip(splits, dts)]
    out_specs[1] = pl.BlockSpec((1, CMP_STRIDE, chunks, splits[1]), lambda i: (i // per_seq, 0, i % per_seq, 0))
    out_shape[1] = jax.ShapeDtypeStruct((b, CMP_STRIDE, t // CMP_STRIDE, splits[1]), F32)
    return pl.pallas_call(
        functools.partial(_inproj_kernel, splits=splits),
        grid=(n // ROW_TILE,),
        in_specs=[
            pl.BlockSpec((ROW_TILE, d), lambda i: (i, 0)),
            pl.BlockSpec((1, d), lambda i: (0, 0)),
            pl.BlockSpec(w_all.shape, lambda i: (0, 0)),
            pl.BlockSpec((1, LANES), lambda i: (0, 0)),
        ],
        out_specs=out_specs,
        out_shape=out_shape,
        scratch_shapes=[pltpu.VMEM((splits[1] // LANES, ROW_TILE, LANES), F32)],
        compiler_params=_params("parallel"),
        name="inproj",
    )(x2, norm_w, w_all, gate_b_pad)


def _gelu_tanh(x):
    return 0.5 * x * (1.0 + jnp.tanh(math.sqrt(2.0 / math.pi) * (x + 0.044715 * (x * x * x))))


def _compress_kernel(c_ref, pos_ref, w1_ref, w2_ref, kc_ref, vc_ref):
    n_chunk = c_ref.shape[2]
    row = lax.broadcasted_iota(jnp.int32, (n_chunk, LANES), 0)
    for kv, out_ref in ((0, kc_ref), (1, vc_ref)):
        def half(hf):
            x = jnp.concatenate(
                [(c_ref[0, p, :, kv * LANES:(kv + 1) * LANES]
                  + pos_ref[kv, hf, :, p * LANES:(p + 1) * LANES]).astype(BF16)
                 for p in range(CMP_STRIDE)], axis=1)
            return _dot(x, w1_ref[kv, hf])

        hidden = half(0) + pltpu.roll(half(1), n_chunk - 1, axis=0)
        out = _dot(_gelu_tanh(hidden).astype(BF16), w2_ref[kv])
        out_ref[0] = jnp.where(row < n_chunk - 1, out, 0.0).astype(out_ref.dtype)


def _compress(tokens, pos, w1, w2):
    b, _, n_chunk, _ = tokens.shape
    out = jax.ShapeDtypeStruct((b, n_chunk, LANES), BF16)
    const = lambda a: pl.BlockSpec(a.shape, lambda i: (0,) * a.ndim, pipeline_mode=pl.Buffered(1))
    return pl.pallas_call(
        _compress_kernel,
        grid=(b,),
        in_specs=[pl.BlockSpec((1,) + tokens.shape[1:], lambda i: (i, 0, 0, 0)), const(pos), const(w1), const(w2)],
        out_specs=[pl.BlockSpec((1, n_chunk, LANES), lambda i: (i, 0, 0))] * 2,
        out_shape=[out, out],
        compiler_params=_params("parallel"),
        name="compress",
    )(tokens, pos, w1, w2)


def _nsa_kernel(q_ref, ks_ref, vs_ref, kw_ref, vw_ref, kc_ref, vc_ref, gates_ref,
                biasc_ref, biass_ref, biasw_ref, ovl_ref, blockneg_ref, normw_ref,
                o_ref, s_sel_ref, s_win_ref, m_sel_ref, acc_sel_ref, owin_ref, gsel_ref, qsel_ref, ocmp_ref):
    i = pl.program_id(1)
    t0 = i * NTQ
    rows = NSA_REP * NTQ
    groups = range(NSA_GROUPS)
    lane = lax.broadcasted_iota(jnp.int32, (rows, LANES), 1)
    ones_v = jnp.ones((NTK, LANES), BF16)
    gates = gates_ref[0]

    n_selp = ovl_ref.shape[0] - 8
    blk = lax.broadcasted_iota(jnp.int32, (n_selp, NTQ), 0)
    cur = (t0 + lax.broadcasted_iota(jnp.int32, (n_selp, NTQ), 1)) // SEL_BLOCK
    causal_blk = blk <= cur
    forced = (blk == 0) | (blk == cur) | (blk == cur - 1)

    def probabilities(s, m):
        return jnp.exp(s - jnp.concatenate([m, m], axis=1)).astype(BF16)

    qgs = [jnp.concatenate(
        [q_ref[0, :, (g * NSA_REP + r) * LANES:(g * NSA_REP + r + 1) * LANES] for r in range(NSA_REP)],
        axis=0) for g in groups]

    lcs = [_dot_nt(qgs[g], kc_ref[0]) + biasc_ref[g, 0] for g in groups]

    win_tiles = []
    for k in range(WIN_TILES):
        j = i - (WIN_TILES - 1) + k
        win_tiles.append((pl.multiple_of(jnp.maximum(j, 0) * NTK, NTK),
                          pl.multiple_of(jnp.where(j < 0, WIN_TILES, k) * NTK, NTK)))
    m_win = [jnp.full((rows, LANES), NEG_INF, F32) for g in groups]
    for k, (off, boff) in enumerate(win_tiles):
        key = kw_ref[0, pl.ds(off, NTK), :]
        raw = [_dot_nt(qgs[g], key) for g in groups]
        for g in groups:
            s = raw[g] + biasw_ref[g, :, pl.ds(boff, NTK)]
            s_win_ref[g, :, k * NTK:(k + 1) * NTK] = s
            m_win[g] = jnp.maximum(m_win[g], jnp.maximum(s[:, :LANES], s[:, LANES:]))

    ecs = []
    for g in groups:
        mc = jnp.maximum(jnp.max(lcs[g], axis=-1, keepdims=True), 0.1 * NEG_INF)
        ec = jnp.exp(lcs[g] - mc).astype(BF16)
        oc = _dot(ec, jnp.concatenate([vc_ref[0], ones_v[:LANES]], axis=1))
        ocmp_ref[g] = oc[:, :LANES] * (1.0 / jnp.maximum(oc[:, LANES:], TINY))
        ecs.append(ec)

    for g in groups:
        imp4 = _dot_nt(ovl_ref[...], ecs[g])
        imp4 = imp4[:n_selp] * (1.0 / jnp.maximum(imp4[n_selp:n_selp + 1], TINY))
        imp = imp4[:, 0:NTQ]
        for r in range(1, NSA_REP):
            imp = imp + imp4[:, r * NTQ:(r + 1) * NTQ]
        score = jnp.where(causal_blk, imp + jnp.where(forced, FORCED_BONUS, 0.0), NEG_INF)
        rank = jnp.zeros((n_selp, NTQ), F32)
        for j in range(n_selp):
            other = score[j:j + 1, :]
            ahead = (other > score) | ((other == score) & (blk > j))
            rank = rank + jnp.where(ahead, 1.0, 0.0)
        unsel_t = jnp.where((rank < SEL_TOP) & causal_blk, 0.0, 1.0)
        unsel_t = jnp.concatenate([unsel_t, jnp.zeros((LANES - n_selp, NTQ), F32)], axis=0)
        unsel = unsel_t.T.astype(BF16)

        qsel_ref[g] = jnp.concatenate([qgs[g], jnp.concatenate([unsel] * NSA_REP, axis=0)], axis=1)

    for g in groups:
        m_win[g] = jnp.broadcast_to(jnp.max(m_win[g], axis=-1, keepdims=True), (rows, LANES))
    acc_win = [jnp.zeros((rows, 2 * LANES), F32) for g in groups]
    for k, (off, _) in enumerate(win_tiles):
        v = jnp.concatenate([vw_ref[0, pl.ds(off, NTK), :], ones_v], axis=1)
        ps = [probabilities(s_win_ref[g, :, k * NTK:(k + 1) * NTK], m_win[g]) for g in groups]
        for g in groups:
            acc_win[g] = acc_win[g] + _dot(ps[g], v)

    def gate(g, branch):
        return jnp.concatenate(
            [gates[:, (g * NSA_REP + r) * 3 + branch:(g * NSA_REP + r) * 3 + branch + 1]
             for r in range(NSA_REP)], axis=0)

    for g in groups:
        ocmp_ref[g] = gate(g, 0) * ocmp_ref[g]
        gsel_ref[g] = jnp.broadcast_to(gate(g, 1), (rows, LANES))
        owin_ref[g] = gate(g, 2) * (acc_win[g][:, :LANES] * (1.0 / acc_win[g][:, LANES:]))

    m_sel_ref[...] = jnp.full(m_sel_ref.shape, NEG_INF, F32)

    n_double = (i + 1) // 2

    def sel_logits(j, n_tiles):
        off = pl.multiple_of(j * NTK, NTK)
        width = n_tiles * NTK
        key = jnp.concatenate([ks_ref[0, pl.ds(off, width), :], blockneg_ref[pl.ds(off, width), :]], axis=1)
        raw = [_dot_nt(qsel_ref[g], key) for g in groups]
        for g in groups:
            m = m_sel_ref[g]
            for k in range(n_tiles):
                boff = pl.multiple_of((2 - jnp.minimum(i - j - k, 2)) * NTK, NTK)
                s = raw[g][:, k * NTK:(k + 1) * NTK] + biass_ref[g, :, pl.ds(boff, NTK)]
                s_sel_ref[g, :, pl.ds(pl.multiple_of(off + k * NTK, NTK), NTK)] = s
                m = jnp.maximum(m, jnp.maximum(s[:, :LANES], s[:, LANES:]))
            m_sel_ref[g] = m

    def sel_weights(j, n_tiles):
        off = pl.multiple_of(j * NTK, NTK)
        width = n_tiles * NTK
        v = jnp.concatenate([vs_ref[0, pl.ds(off, width), :], jnp.ones((width, LANES), BF16)], axis=1)
        ps = []
        for g in groups:
            m = m_sel_ref[g]
            ps.append(jnp.exp(s_sel_ref[g, :, pl.ds(off, width)]
                              - jnp.concatenate([m] * (width // LANES), axis=1)).astype(BF16))
        for g in groups:
            acc_sel_ref[g] += _dot(ps[g], v)

    def sweep(tile_fn):
        def double(d, c):
            tile_fn(2 * d, 2)
            return c

        def single(_, c):
            tile_fn(i, 1)
            return c

        lax.fori_loop(0, n_double, double, 0)
        lax.fori_loop(0, (i + 1) % 2, single, 0)

    sweep(sel_logits)
    for g in groups:
        m_sel_ref[g] = jnp.broadcast_to(jnp.max(m_sel_ref[g], axis=-1, keepdims=True), (rows, LANES))
    acc_sel_ref[...] = jnp.zeros(acc_sel_ref.shape, F32)
    sweep(sel_weights)

    for g in groups:
        o_sel = acc_sel_ref[g, :, :LANES] * (1.0 / acc_sel_ref[g, :, LANES:])
        o = ocmp_ref[g] + gsel_ref[g] * o_sel + owin_ref[g]
        mine = (lane // HEAD_DIM) == g
        ssq = jnp.sum(jnp.where(mine, o * o, 0.0), axis=-1, keepdims=True)
        y = o * lax.rsqrt(ssq * (1.0 / HEAD_DIM) + EPS)
        left = lax.broadcasted_iota(jnp.int32, (NTQ, LANES), 1) < HEAD_DIM
        for pair in range(NSA_REP // 2):
            even = y[(2 * pair) * NTQ:(2 * pair + 1) * NTQ]
            odd = y[(2 * pair + 1) * NTQ:(2 * pair + 2) * NTQ]
            if g == 0:
                odd = pltpu.roll(odd, HEAD_DIM, axis=1)
            else:
                even = pltpu.roll(even, HEAD_DIM, axis=1)
            cb = g * (NSA_REP // 2) + pair
            packed = jnp.where(left, even, odd) * normw_ref[:, cb * LANES:(cb + 1) * LANES]
            o_ref[0, :, cb * LANES:(cb + 1) * LANES] = packed.astype(o_ref.dtype)


def _nsa(qn, kv, kcmp, vcmp, gates, biasc, biass, biasw, ovl, blockneg, normw, b, t):
    nq = t // NTQ
    rows = NSA_REP * NTQ
    qn3 = qn.reshape(b, t, qn.shape[-1])
    kv3 = kv.reshape(b, t, kv.shape[-1])
    gates3 = gates.reshape(b, t, LANES)
    kv_spec = lambda c: pl.BlockSpec((1, t, LANES), lambda bi, qi, c=c: (bi, 0, c))
    const = lambda a: pl.BlockSpec(a.shape, lambda bi, qi: (0,) * a.ndim, pipeline_mode=pl.Buffered(1))
    return pl.pallas_call(
        _nsa_kernel,
        grid=(b, nq),
        in_specs=[
            pl.BlockSpec((1, NTQ, qn3.shape[-1]), lambda bi, qi: (bi, qi, 0)),
            kv_spec(0), kv_spec(1), kv_spec(2), kv_spec(3),
            pl.BlockSpec((1,) + kcmp.shape[1:], lambda bi, qi: (bi, 0, 0)),
            pl.BlockSpec((1,) + vcmp.shape[1:], lambda bi, qi: (bi, 0, 0)),
            pl.BlockSpec((1, NTQ, LANES), lambda bi, qi: (bi, qi, 0)),
            pl.BlockSpec((NSA_GROUPS, 1, rows, LANES), lambda bi, qi: (0, qi, 0, 0)),
            const(biass), const(biasw), const(ovl), const(blockneg), const(normw),
        ],
        out_specs=pl.BlockSpec((1, NTQ, NSA_HEADS * HEAD_DIM), lambda bi, qi: (bi, qi, 0)),
        out_shape=jax.ShapeDtypeStruct((b, t, NSA_HEADS * HEAD_DIM), BF16),
        scratch_shapes=[
            pltpu.VMEM((NSA_GROUPS, rows, t), F32),
            pltpu.VMEM((NSA_GROUPS, rows, WIN_TILES * NTK), F32),
            pltpu.VMEM((NSA_GROUPS, rows, LANES), F32),
            pltpu.VMEM((NSA_GROUPS, rows, 2 * LANES), F32),
            pltpu.VMEM((NSA_GROUPS, rows, LANES), F32),
            pltpu.VMEM((NSA_GROUPS, rows, LANES), F32),
            pltpu.VMEM((NSA_GROUPS, rows, 2 * LANES), BF16),
            pltpu.VMEM((NSA_GROUPS, rows, LANES), F32),
        ],
        compiler_params=_params("parallel", "arbitrary"),
        name="nsa",
    )(qn3, kv3, kv3, kv3, kv3, kcmp, vcmp, gates3, biasc, biass, biasw, ovl, blockneg, normw)


def _sb_kernel(q_ref, k_ref, v_ref, tri_ref, normw_ref, o_ref, acc_ref, run_ref):
    i = pl.program_id(1)
    rows = 2 * TQ
    n_pairs = SB_HEADS // 2
    lane = lax.broadcasted_iota(jnp.int32, (rows, LANES), 1)
    row = lax.broadcasted_iota(jnp.int32, (rows, LANES), 0)
    strict = lax.broadcasted_iota(jnp.int32, (rows, TK), 1) < (
        lax.broadcasted_iota(jnp.int32, (rows, TK), 0) & (TQ - 1))
    mine = (lane // HEAD_DIM) == (row // TQ)
    left = lax.broadcasted_iota(jnp.int32, (TQ, LANES), 1) < HEAD_DIM

    def tile(j, diag):
        off = pl.multiple_of(j * TK, TK)
        cols = [slice(p * LANES, (p + 1) * LANES) for p in range(n_pairs)]
        zs, csums = {}, {}

        def scores(p):
            q2 = q_ref[0, :, cols[p]]
            zero = jnp.zeros_like(q2)
            qp = jnp.concatenate([jnp.where(left, q2, zero), jnp.where(left, zero, q2)],
                                 axis=0)
            zs[p] = _dot_nt(qp, k_ref[0, pl.ds(off, TK), cols[p]])

        def suffix_sums(p):
            z = zs[p]
            neg_abs = lax.bitcast_convert_type(
                lax.bitcast_convert_type(z, jnp.uint32) | jnp.uint32(0x80000000), F32)
            sp = jnp.maximum(z, 0.0) + jnp.log(1.0 + jnp.exp(neg_abs))
            if diag:
                sp = jnp.where(strict, sp, 0.0)
            csums[p] = _dot(sp.astype(BF16), tri_ref[...])

        def weights_times_v(p):
            v = v_ref[0, pl.ds(off, TK), cols[p]]
            z, csum = zs.pop(p), csums.pop(p)
            tile_sum = jnp.broadcast_to(csum[:, 0:1], (rows, LANES))
            if diag:
                a = jnp.where(strict, jnp.exp(z - csum), 0.0)
                acc_ref[p] = _dot(a.astype(BF16), v)
                run_ref[p] = tile_sum
            else:
                run = run_ref[p]
                a = jnp.exp(z - csum - jnp.concatenate([run] * (TK // LANES), axis=1))
                acc_ref[p] += _dot(a.astype(BF16), v)
                run_ref[p] = run + tile_sum

        stages = (scores, suffix_sums, weights_times_v)
        for tick in range(n_pairs + len(stages) - 1):
            for k, stage in enumerate(stages):
                if 0 <= tick - k < n_pairs:
                    stage(tick - k)

    tile(i, True)

    def body(step, carry):
        tile(i - 1 - step, False)
        return carry

    lax.fori_loop(0, i, body, 0)

    for pair in range(n_pairs):
        cols = slice(pair * LANES, (pair + 1) * LANES)
        acc = acc_ref[pair]
        ssq = jnp.sum(jnp.where(mine, acc * acc, 0.0), axis=-1, keepdims=True)
        y = acc * lax.rsqrt(ssq * (1.0 / HEAD_DIM) + EPS)
        packed = jnp.where(left, y[:TQ], y[TQ:]) * normw_ref[:, cols]
        o_ref[0, :, cols] = packed.astype(o_ref.dtype)


def _sb(q, k, v, tri, normw, b, t):
    q3 = q.reshape(b, t, q.shape[-1])
    k3 = k.reshape(b, t, k.shape[-1])
    v3 = v.reshape(b, t, v.shape[-1])
    width = SB_HEADS * HEAD_DIM
    return pl.pallas_call(
        _sb_kernel,
        grid=(b, t // TQ),
        in_specs=[
            pl.BlockSpec((1, TQ, q3.shape[-1]), lambda bi, qi: (bi, qi, 0)),
            pl.BlockSpec((1, t, width), lambda bi, qi: (bi, 0, 0)),
            pl.BlockSpec((1, t, width), lambda bi, qi: (bi, 0, 0)),
            pl.BlockSpec(tri.shape, lambda bi, qi: (0, 0)),
            pl.BlockSpec(normw.shape, lambda bi, qi: (0, 0)),
        ],
        out_specs=pl.BlockSpec((1, TQ, width), lambda bi, qi: (bi, qi, 0)),
        out_shape=jax.ShapeDtypeStruct((b, t, width), BF16),
        scratch_shapes=[pltpu.VMEM((SB_HEADS // 2, 2 * TQ, LANES), F32)] * 2,
        compiler_params=_params("parallel", "arbitrary"),
        name="stickbreak",
    )(q3, k3, v3, tri, normw)


def _mixffn_kernel(x_ref, on_ref, os_ref, xh_ref, onh_ref, osh_ref, wn_ref, ws_ref, n2_ref,
                   wup_ref, cw_ref, cb_ref, wdown_ref, fw_ref, o_ref, up_ref, act_ref, *, d_ff):
    i = pl.program_id(1)
    tile = x_ref.shape[1]

    def mix(x, o_nsa, o_sb):
        h = x + _dot(o_nsa, wn_ref[...]) + _dot(o_sb, ws_ref[...])
        u = h * lax.rsqrt(jnp.mean(h * h, axis=-1, keepdims=True) + EPS) * n2_ref[...]
        return h, u.astype(BF16)

    h, u = mix(x_ref[0], on_ref[0], os_ref[0])
    _, u_halo = mix(xh_ref[0], onh_ref[0], osh_ref[0])
    halo = jnp.where(i > 0, u_halo, jnp.zeros_like(u_halo))
    ue = jnp.concatenate([halo, u], axis=0)

    def conv(slot, c0):
        w = cw_ref[:, c0:c0 + FFN_CHUNK]
        taps = [up_ref[slot, HALO - k:HALO - k + tile] for k in range(CONV_WIDTH)]
        return (w[2:3] * taps[0] + w[1:2] * taps[1] + w[0:1] * taps[2]
                + cb_ref[:, c0:c0 + FFN_CHUNK])

    def up_project(c):
        slot = 2 * (c % 2)
        up_ref[slot] = _dot(ue, wup_ref[:, c * FFN_CHUNK:(c + 1) * FFN_CHUNK])
        up_ref[slot + 1] = _dot(ue, wup_ref[:, d_ff + c * FFN_CHUNK:d_ff + (c + 1) * FFN_CHUNK])

    n_chunks = d_ff // FFN_CHUNK
    up_project(0)
    for c in range(n_chunks):
        g0, v0 = c * FFN_CHUNK, d_ff + c * FFN_CHUNK
        slot = 2 * (c % 2)
        if c + 1 < n_chunks:
            up_project(c + 1)
        gate = conv(slot, g0)
        val = conv(slot + 1, v0)
        act = gate * (1.0 / (1.0 + jnp.exp(-gate))) * val
        act_ref[:, g0:g0 + FFN_CHUNK] = act.astype(BF16)
    acc = h + _dot(act_ref[...], wdown_ref[...])
    y = acc * lax.rsqrt(jnp.mean(acc * acc, axis=-1, keepdims=True) + EPS) * fw_ref[...]
    o_ref[0] = y.astype(o_ref.dtype)


def _mixffn(x, o_nsa, o_sb, w_n, w_s, norm2_w, w_up, conv_w, conv_b, w_down, final_w):
    b, t, d = x.shape
    d_ff = w_down.shape[0]
    per = ROW_TILE // HALO
    tile = lambda a: pl.BlockSpec((1, ROW_TILE, a.shape[-1]), lambda bi, ti: (bi, ti, 0))
    halo = lambda a: pl.BlockSpec((1, HALO, a.shape[-1]), lambda bi, ti: (bi, jnp.maximum(ti * per - 1, 0), 0))
    const = lambda a: pl.BlockSpec(a.shape, lambda bi, ti: (0, 0), pipeline_mode=pl.Buffered(1))
    weights = (w_n, w_s, norm2_w, w_up, conv_w, conv_b, w_down, final_w)
    return pl.pallas_call(
        functools.partial(_mixffn_kernel, d_ff=d_ff),
        grid=(b, t // ROW_TILE),
        in_specs=[tile(x), tile(o_nsa), tile(o_sb), halo(x), halo(o_nsa), halo(o_sb)]
                 + [const(w) for w in weights],
        out_specs=pl.BlockSpec((1, ROW_TILE, d), lambda bi, ti: (bi, ti, 0)),
        out_shape=jax.ShapeDtypeStruct((b, t, d), F32),
        scratch_shapes=[pltpu.VMEM((4, HALO + ROW_TILE, FFN_CHUNK), F32),
                        pltpu.VMEM((ROW_TILE, d_ff), BF16)],
        compiler_params=_params("parallel", "arbitrary"),
        name="mixffn",
    )(x, o_nsa, o_sb, x, o_nsa, o_sb, *weights)


def _pad_half(w, half):
    z = jnp.zeros_like(w)
    return jnp.concatenate([w, z] if half == 0 else [z, w], axis=-1)


def _input_weights(w):
    d = w.shape[0]
    nq, kvw = NSA_HEADS * HEAD_DIM, NSA_GROUPS * HEAD_DIM
    scale = HEAD_DIM ** -0.5
    o = 0
    q_n = w[:, o:o + nq] * scale; o += nq
    kc = w[:, o:o + kvw]; o += kvw
    vc = w[:, o:o + kvw]; o += kvw
    ks = w[:, o:o + kvw]; o += kvw
    vs = w[:, o:o + kvw]; o += kvw
    kw = w[:, o:o + kvw]; o += kvw
    vw = w[:, o:o + kvw]; o += kvw
    gl = w[:, o:o + NSA_HEADS * 3]; o += NSA_HEADS * 3
    sbw = SB_HEADS * HEAD_DIM
    q_s = w[:, o:o + sbw] * scale; o += sbw
    k_s = w[:, o:o + sbw]; o += sbw
    v_s = w[:, o:o + sbw]; o += sbw
    q_n = q_n.reshape(d, NSA_GROUPS, NSA_REP, HEAD_DIM)
    q_n = jnp.concatenate([_pad_half(q_n[:, g], g) for g in range(NSA_GROUPS)], axis=1).reshape(d, -1)
    gl = jnp.pad(gl, ((0, 0), (0, LANES - gl.shape[1])))
    pieces = (q_n, jnp.concatenate([kc, vc], axis=1), jnp.concatenate([ks, vs, kw, vw], axis=1),
              gl, q_s, k_s, v_s)
    return jnp.concatenate(pieces, axis=1).astype(BF16), tuple(p.shape[1] for p in pieces)


def _toeplitz_kernel(w_ref, o_ref, *, n_cols):
    n_rows = o_ref.shape[2]
    for k in range(w_ref.shape[1]):
        x = jnp.broadcast_to(w_ref[0, k], (n_rows, w_ref.shape[-1]))
        o_ref[0, k, :, :n_cols] = pltpu.roll(x, 0, 1, stride=1, stride_axis=0)[:, :n_cols]
        if n_cols < o_ref.shape[-1]:
            o_ref[0, k, :, n_cols:] = jnp.full((n_rows, o_ref.shape[-1] - n_cols), NEG_INF, F32)


def _toeplitz_rows(w, n_rows, n_cols, neg_cols):
    h, k, width = w.shape
    return pl.pallas_call(
        functools.partial(_toeplitz_kernel, n_cols=n_cols),
        grid=(h,),
        in_specs=[pl.BlockSpec((1, k, 1, width), lambda i: (i, 0, 0, 0))],
        out_specs=pl.BlockSpec((1, k, n_rows, n_cols + neg_cols), lambda i: (i, 0, 0, 0)),
        out_shape=jax.ShapeDtypeStruct((h, k, n_rows, n_cols + neg_cols), F32),
        compiler_params=_params("parallel"),
        name="toeplitz",
    )(w.reshape(h, k, 1, width))


def _bias_tables(rel_bias, t):
    nq = t // NTQ
    rb = rel_bias.T.astype(F32)

    def by_signed(d, ok):
        onehot = (_t5_bucket_np(d).reshape(1, -1) == np.arange(N_BUCKETS)[:, None]) & ok.reshape(1, -1)
        vals = jnp.dot(rb, jnp.asarray(onehot, F32), precision=lax.Precision.HIGHEST)
        vals = vals + jnp.asarray(np.where(ok.reshape(1, -1), 0.0, NEG_INF), F32)
        return vals.reshape((rb.shape[0],) + d.shape)

    def toeplitz(d, ok, n_rows, n_cols, neg_cols=0):
        p = d.shape[-1]
        assert p == n_rows + n_cols - 1
        width = -(-p // LANES) * LANES
        src = (n_cols - 1 - np.arange(width)) % width
        w = by_signed(d[..., np.minimum(src, p - 1)], ok[..., np.minimum(src, p - 1)] & (src < p))
        return _toeplitz_rows(w.reshape(rb.shape[0], -1, width), n_rows, n_cols, neg_cols).reshape(
            w.shape[:-1] + (n_rows, n_cols + neg_cols))

    def stacked(tb):
        return tb.reshape(NSA_GROUPS, NSA_REP * NTQ, tb.shape[-1])

    span = WINDOW + NTQ
    dw = np.arange(NTQ + span - 1) - (span - 1) + WINDOW
    biasw = stacked(toeplitz(dw, (dw >= 0) & (dw < WINDOW), NTQ, span, neg_cols=NTK))
    assert NTK > MAX_DISTANCE
    ds = np.arange(NTQ + 3 * NTK - 1) - (3 * NTK - 1) + 2 * NTK
    biass = stacked(toeplitz(ds, ds >= 0, NTQ, 3 * NTK))
    n_cmp = (t - CMP_BLOCK) // CMP_STRIDE + 1
    n_a = t // CMP_STRIDE
    dc = (CMP_STRIDE * (np.arange(n_a + LANES - 1)[None, :] - (LANES - 1))
          + np.arange(CMP_STRIDE)[:, None] - (CMP_BLOCK - 1))
    bc = toeplitz(dc, dc >= 0, n_a, LANES)
    bc = jnp.where(jnp.asarray(np.arange(LANES) < n_cmp), bc, NEG_INF)
    bc = bc.transpose(0, 2, 1, 3).reshape(NSA_GROUPS, NSA_REP, nq, NTQ, LANES)
    biasc = bc.transpose(0, 2, 1, 3, 4).reshape(NSA_GROUPS, nq, NSA_REP * NTQ, LANES)
    return biasc, biass, biasw


def _suffix_sum_matrix():
    return jnp.asarray(np.arange(TK)[:, None] >= np.arange(TK)[None, :], BF16)


def _selection_constants(t):
    n_cmp = (t - CMP_BLOCK) // CMP_STRIDE + 1
    n_sel = t // SEL_BLOCK
    n_selp = -(-n_sel // 8) * 8
    n = np.arange(LANES)[None, :]
    j = np.arange(n_selp)[:, None]
    ovl = ((CMP_STRIDE * n < SEL_BLOCK * (j + 1)) & (CMP_STRIDE * n + CMP_BLOCK > SEL_BLOCK * j)
           & (n < n_cmp) & (j < n_sel))
    ovl = np.concatenate([ovl, np.ones((1, LANES), bool), np.zeros((7, LANES), bool)], axis=0)
    own = (np.arange(t)[:, None] // SEL_BLOCK) == np.arange(LANES)[None, :]
    blockneg = jnp.where(jnp.asarray(own), NEG_INF, 0.0).astype(BF16)
    return jnp.asarray(ovl, BF16), blockneg


def kernel(x, norm1_w, w_in, cmp_pos_k, cmp_pos_v, cmp_k_w1, cmp_k_w2, cmp_v_w1, cmp_v_w2, gate_b,
           nsa_out_norm_w, sb_out_norm_w, w_out, norm2_w, w_up, conv_w, conv_b, w_down, rel_bias,
           final_norm_w):
    b, t, d = x.shape
    assert t % ROW_TILE == 0 and t % NTQ == 0 and t >= WINDOW + NTQ and (b * t) % ROW_TILE == 0
    assert (t - CMP_BLOCK) // CMP_STRIDE + 1 < LANES + 1 and t // CMP_STRIDE == LANES
    assert w_in.shape[0] == 1, "single-layer block: the closing norm is fused into the FFN kernel"
    nsa_w = NSA_HEADS * HEAD_DIM
    l = 0

    biasc, biass, biasw = _bias_tables(rel_bias, t)
    ovl, blockneg = _selection_constants(t)
    tri = _suffix_sum_matrix()

    h = x.reshape(b * t, d)
    w_all, splits = _input_weights(w_in[l])
    gb = jnp.pad(gate_b[l], (0, LANES - gate_b.shape[1])).reshape(1, LANES)
    qn, kcvc, kv, gates, sbq, sbk, sbv = _inproj(h, norm1_w[l].reshape(1, d), w_all, gb, splits, b, t)

    eye = jnp.eye(NSA_GROUPS, dtype=F32)

    def group_diag(w):
        w = jnp.einsum("...rc,gk->...grkc", w, eye)
        return w.reshape(-1, NSA_GROUPS * w.shape[-1])

    pos = jnp.stack([jnp.tile(p_.reshape(2, CMP_STRIDE, 1, HEAD_DIM), (1, 1, NSA_GROUPS, 1)).reshape(2, 1, -1)
                     for p_ in (cmp_pos_k[l], cmp_pos_v[l])])
    w1 = jnp.stack([jnp.stack([group_diag(w_.reshape(2, CMP_STRIDE, HEAD_DIM, -1)[hf]) for hf in range(2)])
                    for w_ in (cmp_k_w1[l], cmp_v_w1[l])]).astype(BF16)
    w2 = jnp.stack([group_diag(w_) for w_ in (cmp_k_w2[l], cmp_v_w2[l])]).astype(BF16)
    kcmp, vcmp = _compress(kcvc, pos, w1, w2)

    o_nsa = _nsa(qn, kv, kcmp, vcmp, gates, biasc, biass, biasw, ovl, blockneg,
                 nsa_out_norm_w[l].reshape(1, nsa_w), b, t)
    o_sb = _sb(sbq, sbk, sbv, tri, sb_out_norm_w[l].reshape(1, -1), b, t)

    wo = w_out[l].astype(BF16)
    return _mixffn(x, o_nsa, o_sb, wo[:nsa_w], wo[nsa_w:], norm2_w[l].reshape(1, d),
                   w_up[l].astype(BF16), conv_w[l], conv_b[l].reshape(1, -1),
                   w_down[l].astype(BF16), final_norm_w.reshape(1, d))
```

```python
import functools
import math

import jax
import jax.numpy as jnp
import numpy as np
from jax import lax
from jax.experimental import pallas as pl
from jax.experimental.pallas import tpu as pltpu

F32 = jnp.float32
BF16 = jnp.bfloat16

HEAD_DIM = 64
NSA_HEADS = 8
SB_HEADS = 8
NSA_GROUPS = 2
NSA_REP = NSA_HEADS // NSA_GROUPS
CMP_BLOCK = 32
CMP_STRIDE = 16
SEL_BLOCK = 64
SEL_TOP = 8
WINDOW = 512
N_BUCKETS = 32
MAX_DISTANCE = 128
CONV_WIDTH = 3
EPS = 1e-6
NEG_INF = -1e30
FORCED_BONUS = 1e6
TINY = 1e-30

LANES = 128
TQ = 256
TK = 256
NTQ = 256
NTK = 256
ROW_TILE = 512
FFN_CHUNK = 256
HALO = 16
VMEM_LIMIT = 56 * 1024 * 1024
WIN_TILES = WINDOW // NTK + 1


def _dot(a, b):
    return jnp.dot(a, b, preferred_element_type=F32)


def _dot_nt(a, b):
    return lax.dot_general(a, b, (((1,), (1,)), ((), ())), preferred_element_type=F32)


def _params(*sem):
    return pltpu.CompilerParams(dimension_semantics=sem, vmem_limit_bytes=VMEM_LIMIT)


def _t5_bucket_np(dist):
    n = np.maximum(dist, 0)
    max_exact = N_BUCKETS // 2
    nf = np.maximum(n, 1).astype(np.float32)
    log_b = max_exact + (np.log(nf / np.float32(max_exact)) / np.float32(math.log(MAX_DISTANCE / max_exact))
                         * np.float32(N_BUCKETS - max_exact)).astype(np.int32)
    log_b = np.minimum(log_b, N_BUCKETS - 1)
    return np.where(n < max_exact, n, log_b)


def _inproj_kernel(x_ref, nw_ref, w_ref, gb_ref, qn_ref, kcvc_ref, kv_ref, gates_ref,
                   sbq_ref, sbk_ref, sbv_ref, stage_ref, *, splits):
    x = x_ref[...]
    u = x * lax.rsqrt(jnp.mean(x * x, axis=-1, keepdims=True) + EPS) * nw_ref[...]
    u = u.astype(BF16)
    outs = (qn_ref, kcvc_ref, kv_ref, gates_ref, sbq_ref, sbk_ref, sbv_ref)
    start = 0
    for ref, size in zip(outs, splits):
        r = _dot(u, w_ref[:, start:start + size])
        if ref is gates_ref:
            r = jax.nn.sigmoid(r + gb_ref[...])
        if ref is kcvc_ref:
            for c in range(size // LANES):
                stage_ref[c] = r[:, c * LANES:(c + 1) * LANES]
                for p in range(CMP_STRIDE):
                    ref[0, p, :, c * LANES:(c + 1) * LANES] = (
                        stage_ref[c, pl.ds(p, ROW_TILE // CMP_STRIDE, stride=CMP_STRIDE), :])
        elif ref is qn_ref:
            half = lax.broadcasted_iota(jnp.int32, (r.shape[0], LANES), 1) // HEAD_DIM
            for h in range(NSA_HEADS):
                g = h // NSA_REP
                pair = r[:, (h // 2) * LANES:(h // 2 + 1) * LANES]
                if h % 2 != g:
                    pair = pltpu.roll(pair, HEAD_DIM, axis=1)
                ref[:, h * LANES:(h + 1) * LANES] = jnp.where(half == g, pair, 0.0).astype(ref.dtype)
        else:
            ref[...] = r.astype(ref.dtype)
        start += size


def _inproj(x2, norm_w, w_all, gate_b_pad, splits, b, t):
    n, d = x2.shape
    dts = (BF16, F32, BF16, F32, BF16, BF16, BF16)
    per_seq = t // ROW_TILE
    chunks = ROW_TILE // CMP_STRIDE
    widths = (NSA_HEADS * LANES,) + tuple(splits[1:])
    out_specs = [pl.BlockSpec((ROW_TILE, s), lambda i: (i, 0)) for s in widths]
    out_shape = [jax.ShapeDtypeStruct((n, s), dt) for s, dt in zip(widths, dts)]
    out_specs[1] = pl.BlockSpec((1, CMP_STRIDE, chunks, splits[1]), lambda i: (i // per_seq, 0, i % per_seq, 0))
    out_shape[1] = jax.ShapeDtypeStruct((b, CMP_STRIDE, t // CMP_STRIDE, splits[1]), F32)
    return pl.pallas_call(
        functools.partial(_inproj_kernel, splits=splits),
        grid=(n // ROW_TILE,),
        in_specs=[
            pl.BlockSpec((ROW_TILE, d), lambda i: (i, 0)),
            pl.BlockSpec((1, d), lambda i: (0, 0)),
            pl.BlockSpec(w_all.shape, lambda i: (0, 0)),
            pl.BlockSpec((1, LANES), lambda i: (0, 0)),
        ],
        out_specs=out_specs,
        out_shape=out_shape,
        scratch_shapes=[pltpu.VMEM((splits[1] // LANES, ROW_TILE, LANES), F32)],
        compiler_params=_params("parallel"),
        name="inproj",
    )(x2, norm_w, w_all, gate_b_pad)


def _gelu_tanh(x):
    return 0.5 * x * (1.0 + jnp.tanh(math.sqrt(2.0 / math.pi) * (x + 0.044715 * (x * x * x))))


def _compress_kernel(c_ref, pos_ref, w1_ref, w2_ref, kc_ref, vc_ref):
    n_chunk = c_ref.shape[2]
    row = lax.broadcasted_iota(jnp.int32, (n_chunk, LANES), 0)
    for kv, out_ref in ((0, kc_ref), (1, vc_ref)):
        def half(hf):
            x = jnp.concatenate(
                [(c_ref[0, p, :, kv * LANES:(kv + 1) * LANES]
                  + pos_ref[kv, hf, :, p * LANES:(p + 1) * LANES]).astype(BF16)
                 for p in range(CMP_STRIDE)], axis=1)
            return _dot(x, w1_ref[kv, hf])

        hidden = half(0) + pltpu.roll(half(1), n_chunk - 1, axis=0)
        out = _dot(_gelu_tanh(hidden).astype(BF16), w2_ref[kv])
        out_ref[0] = jnp.where(row < n_chunk - 1, out, 0.0).astype(out_ref.dtype)


def _compress(tokens, pos, w1, w2):
    b, _, n_chunk, _ = tokens.shape
    out = jax.ShapeDtypeStruct((b, n_chunk, LANES), BF16)
    const = lambda a: pl.BlockSpec(a.shape, lambda i: (0,) * a.ndim, pipeline_mode=pl.Buffered(1))
    return pl.pallas_call(
        _compress_kernel,
        grid=(b,),
        in_specs=[pl.BlockSpec((1,) + tokens.shape[1:], lambda i: (i, 0, 0, 0)), const(pos), const(w1), const(w2)],
        out_specs=[pl.BlockSpec((1, n_chunk, LANES), lambda i: (i, 0, 0))] * 2,
        out_shape=[out, out],
        compiler_params=_params("parallel"),
        name="compress",
    )(tokens, pos, w1, w2)


def _nsa_kernel(q_ref, ks_ref, vs_ref, kw_ref, vw_ref, kc_ref, vc_ref, gates_ref,
                biasc_ref, biass_ref, biasw_ref, ovl_ref, blockneg_ref, normw_ref,
                o_ref, s_sel_ref, s_win_ref, m_sel_ref, acc_sel_ref, owin_ref, gsel_ref, qsel_ref, ocmp_ref):
    i = pl.program_id(1)
    t0 = i * NTQ
    rows = NSA_REP * NTQ
    groups = range(NSA_GROUPS)
    lane = lax.broadcasted_iota(jnp.int32, (rows, LANES), 1)
    ones_v = jnp.ones((NTK, LANES), BF16)
    gates = gates_ref[0]

    n_selp = ovl_ref.shape[0] - 8
    blk = lax.broadcasted_iota(jnp.int32, (n_selp, NTQ), 0)
    cur = (t0 + lax.broadcasted_iota(jnp.int32, (n_selp, NTQ), 1)) // SEL_BLOCK
    causal_blk = blk <= cur
    forced = (blk == 0) | (blk == cur) | (blk == cur - 1)

    def probabilities(s, m):
        return jnp.exp(s - jnp.concatenate([m, m], axis=1)).astype(BF16)

    qgs = [jnp.concatenate(
        [q_ref[0, :, (g * NSA_REP + r) * LANES:(g * NSA_REP + r + 1) * LANES] for r in range(NSA_REP)],
        axis=0) for g in groups]

    lcs = [_dot_nt(qgs[g], kc_ref[0]) + biasc_ref[g, 0] for g in groups]

    win_tiles = []
    for k in range(WIN_TILES):
        j = i - (WIN_TILES - 1) + k
        win_tiles.append((pl.multiple_of(jnp.maximum(j, 0) * NTK, NTK),
                          pl.multiple_of(jnp.where(j < 0, WIN_TILES, k) * NTK, NTK)))
    m_win = [jnp.full((rows, LANES), NEG_INF, F32) for g in groups]
    for k, (off, boff) in enumerate(win_tiles):
        key = kw_ref[0, pl.ds(off, NTK), :]
        raw = [_dot_nt(qgs[g], key) for g in groups]
        for g in groups:
            s = raw[g] + biasw_ref[g, :, pl.ds(boff, NTK)]
            s_win_ref[g, :, k * NTK:(k + 1) * NTK] = s
            m_win[g] = jnp.maximum(m_win[g], jnp.maximum(s[:, :LANES], s[:, LANES:]))

    ecs = []
    for g in groups:
        mc = jnp.maximum(jnp.max(lcs[g], axis=-1, keepdims=True), 0.1 * NEG_INF)
        ec = jnp.exp(lcs[g] - mc).astype(BF16)
        oc = _dot(ec, jnp.concatenate([vc_ref[0], ones_v[:LANES]], axis=1))
        ocmp_ref[g] = oc[:, :LANES] * (1.0 / jnp.maximum(oc[:, LANES:], TINY))
        ecs.append(ec)

    for g in groups:
        imp4 = _dot_nt(ovl_ref[...], ecs[g])
        imp4 = imp4[:n_selp] * (1.0 / jnp.maximum(imp4[n_selp:n_selp + 1], TINY))
        imp = imp4[:, 0:NTQ]
        for r in range(1, NSA_REP):
            imp = imp + imp4[:, r * NTQ:(r + 1) * NTQ]
        score = jnp.where(causal_blk, imp + jnp.where(forced, FORCED_BONUS, 0.0), NEG_INF)
        rank = jnp.zeros((n_selp, NTQ), F32)
        for j in range(n_selp):
            other = score[j:j + 1, :]
            ahead = (other > score) | ((other == score) & (blk > j))
            rank = rank + jnp.where(ahead, 1.0, 0.0)
        unsel_t = jnp.where((rank < SEL_TOP) & causal_blk, 0.0, 1.0)
        unsel_t = jnp.concatenate([unsel_t, jnp.zeros((LANES - n_selp, NTQ), F32)], axis=0)
        unsel = unsel_t.T.astype(BF16)

        qsel_ref[g] = jnp.concatenate([qgs[g], jnp.concatenate([unsel] * NSA_REP, axis=0)], axis=1)

    for g in groups:
        m_win[g] = jnp.broadcast_to(jnp.max(m_win[g], axis=-1, keepdims=True), (rows, LANES))
    acc_win = [jnp.zeros((rows, 2 * LANES), F32) for g in groups]
    for k, (off, _) in enumerate(win_tiles):
        v = jnp.concatenate([vw_ref[0, pl.ds(off, NTK), :], ones_v], axis=1)
        ps = [probabilities(s_win_ref[g, :, k * NTK:(k + 1) * NTK], m_win[g]) for g in groups]
        for g in groups:
            acc_win[g] = acc_win[g] + _dot(ps[g], v)

    def gate(g, branch):
        return jnp.concatenate(
            [gates[:, (g * NSA_REP + r) * 3 + branch:(g * NSA_REP + r) * 3 + branch + 1]
             for r in range(NSA_REP)], axis=0)

    for g in groups:
        ocmp_ref[g] = gate(g, 0) * ocmp_ref[g]
        gsel_ref[g] = jnp.broadcast_to(gate(g, 1), (rows, LANES))
        owin_ref[g] = gate(g, 2) * (acc_win[g][:, :LANES] * (1.0 / acc_win[g][:, LANES:]))

    m_sel_ref[...] = jnp.full(m_sel_ref.shape, NEG_INF, F32)
    acc_sel_ref[...] = jnp.zeros(acc_sel_ref.shape, F32)

    n_double = (i + 1) // 2

    def sel_logits(j, n_tiles):
        off = pl.multiple_of(j * NTK, NTK)
        width = n_tiles * NTK
        key = jnp.concatenate([ks_ref[0, pl.ds(off, width), :], blockneg_ref[pl.ds(off, width), :]], axis=1)
        raw = [_dot_nt(qsel_ref[g], key) for g in groups]
        for g in groups:
            m = None
            for k in range(n_tiles):
                boff = pl.multiple_of((2 - jnp.minimum(i - j - k, 2)) * NTK, NTK)
                s = raw[g][:, k * NTK:(k + 1) * NTK] + biass_ref[g, :, pl.ds(boff, NTK)]
                s_sel_ref[g, :, pl.ds(pl.multiple_of(off + k * NTK, NTK), NTK)] = s
                tile_max = jnp.maximum(s[:, :LANES], s[:, LANES:])
                m = tile_max if m is None else jnp.maximum(m, tile_max)
            m = jnp.broadcast_to(jnp.max(m, axis=-1, keepdims=True), (rows, LANES))
            m_sel_ref[g] = jnp.maximum(m_sel_ref[g], m)

    def sel_weights(j, n_tiles):
        off = pl.multiple_of(j * NTK, NTK)
        width = n_tiles * NTK
        v = jnp.concatenate([vs_ref[0, pl.ds(off, width), :], jnp.ones((width, LANES), BF16)], axis=1)
        ps = []
        for g in groups:
            m = m_sel_ref[g]
            ps.append(jnp.exp(s_sel_ref[g, :, pl.ds(off, width)]
                              - jnp.concatenate([m] * (width // LANES), axis=1)).astype(BF16))
        for g in groups:
            acc_sel_ref[g] += _dot(ps[g], v)

    def sweep(tile_fn):
        def double(d, c):
            tile_fn(2 * d, 2)
            return c

        def single(_, c):
            tile_fn(i, 1)
            return c

        lax.fori_loop(0, n_double, double, 0)
        lax.fori_loop(0, (i + 1) % 2, single, 0)

    sweep(sel_logits)
    sweep(sel_weights)

    for g in groups:
        o_sel = acc_sel_ref[g, :, :LANES] * (1.0 / acc_sel_ref[g, :, LANES:])
        o = ocmp_ref[g] + gsel_ref[g] * o_sel + owin_ref[g]
        mine = (lane // HEAD_DIM) == g
        ssq = jnp.sum(jnp.where(mine, o * o, 0.0), axis=-1, keepdims=True)
        y = o * lax.rsqrt(ssq * (1.0 / HEAD_DIM) + EPS)
        left = lax.broadcasted_iota(jnp.int32, (NTQ, LANES), 1) < HEAD_DIM
        for pair in range(NSA_REP // 2):
            even = y[(2 * pair) * NTQ:(2 * pair + 1) * NTQ]
            odd = y[(2 * pair + 1) * NTQ:(2 * pair + 2) * NTQ]
            if g == 0:
                odd = pltpu.roll(odd, HEAD_DIM, axis=1)
            else:
                even = pltpu.roll(even, HEAD_DIM, axis=1)
            cb = g * (NSA_REP // 2) + pair
            packed = jnp.where(left, even, odd) * normw_ref[:, cb * LANES:(cb + 1) * LANES]
            o_ref[0, :, cb * LANES:(cb + 1) * LANES] = packed.astype(o_ref.dtype)


def _nsa(qn, kv, kcmp, vcmp, gates, biasc, biass, biasw, ovl, blockneg, normw, b, t):
    nq = t // NTQ
    rows = NSA_REP * NTQ
    qn3 = qn.reshape(b, t, qn.shape[-1])
    kv3 = kv.reshape(b, t, kv.shape[-1])
    gates3 = gates.reshape(b, t, LANES)
    kv_spec = lambda c: pl.BlockSpec((1, t, LANES), lambda bi, qi, c=c: (bi, 0, c))
    const = lambda a: pl.BlockSpec(a.shape, lambda bi, qi: (0,) * a.ndim, pipeline_mode=pl.Buffered(1))
    return pl.pallas_call(
        _nsa_kernel,
        grid=(b, nq),
        in_specs=[
            pl.BlockSpec((1, NTQ, qn3.shape[-1]), lambda bi, qi: (bi, qi, 0)),
            kv_spec(0), kv_spec(1), kv_spec(2), kv_spec(3),
            pl.BlockSpec((1,) + kcmp.shape[1:], lambda bi, qi: (bi, 0, 0)),
            pl.BlockSpec((1,) + vcmp.shape[1:], lambda bi, qi: (bi, 0, 0)),
            pl.BlockSpec((1, NTQ, LANES), lambda bi, qi: (bi, qi, 0)),
            pl.BlockSpec((NSA_GROUPS, 1, rows, LANES), lambda bi, qi: (0, qi, 0, 0)),
            const(biass), const(biasw), const(ovl), const(blockneg), const(normw),
        ],
        out_specs=pl.BlockSpec((1, NTQ, NSA_HEADS * HEAD_DIM), lambda bi, qi: (bi, qi, 0)),
        out_shape=jax.ShapeDtypeStruct((b, t, NSA_HEADS * HEAD_DIM), BF16),
        scratch_shapes=[
            pltpu.VMEM((NSA_GROUPS, rows, t), F32),
            pltpu.VMEM((NSA_GROUPS, rows, WIN_TILES * NTK), F32),
            pltpu.VMEM((NSA_GROUPS, rows, LANES), F32),
            pltpu.VMEM((NSA_GROUPS, rows, 2 * LANES), F32),
            pltpu.VMEM((NSA_GROUPS, rows, LANES), F32),
            pltpu.VMEM((NSA_GROUPS, rows, LANES), F32),
            pltpu.VMEM((NSA_GROUPS, rows, 2 * LANES), BF16),
            pltpu.VMEM((NSA_GROUPS, rows, LANES), F32),
        ],
        compiler_params=_params("parallel", "arbitrary"),
        name="nsa",
    )(qn3, kv3, kv3, kv3, kv3, kcmp, vcmp, gates3, biasc, biass, biasw, ovl, blockneg, normw)


def _sb_kernel(q_ref, k_ref, v_ref, tri_ref, normw_ref, o_ref, acc_ref, run_ref):
    i = pl.program_id(1)
    rows = 2 * TQ
    n_pairs = SB_HEADS // 2
    lane = lax.broadcasted_iota(jnp.int32, (rows, LANES), 1)
    row = lax.broadcasted_iota(jnp.int32, (rows, LANES), 0)
    strict = lax.broadcasted_iota(jnp.int32, (rows, TK), 1) < (
        lax.broadcasted_iota(jnp.int32, (rows, TK), 0) & (TQ - 1))
    mine = (lane // HEAD_DIM) == (row // TQ)
    left = lax.broadcasted_iota(jnp.int32, (TQ, LANES), 1) < HEAD_DIM

    def tile(j, diag):
        off = pl.multiple_of(j * TK, TK)
        cols = [slice(p * LANES, (p + 1) * LANES) for p in range(n_pairs)]
        zs, csums = {}, {}

        def scores(p):
            q2 = q_ref[0, :, cols[p]]
            zero = jnp.zeros_like(q2)
            qp = jnp.concatenate([jnp.where(left, q2, zero), jnp.where(left, zero, q2)],
                                 axis=0)
            zs[p] = _dot_nt(qp, k_ref[0, pl.ds(off, TK), cols[p]])

        def suffix_sums(p):
            z = zs[p]
            neg_abs = lax.bitcast_convert_type(
                lax.bitcast_convert_type(z, jnp.uint32) | jnp.uint32(0x80000000), F32)
            sp = jnp.maximum(z, 0.0) + jnp.log(1.0 + jnp.exp(neg_abs))
            if diag:
                sp = jnp.where(strict, sp, 0.0)
            csums[p] = _dot(sp.astype(BF16), tri_ref[...])

        def weights_times_v(p):
            v = v_ref[0, pl.ds(off, TK), cols[p]]
            z, csum = zs.pop(p), csums.pop(p)
            tile_sum = jnp.broadcast_to(csum[:, 0:1], (rows, LANES))
            if diag:
                a = jnp.where(strict, jnp.exp(z - csum), 0.0)
                acc_ref[p] = _dot(a.astype(BF16), v)
                run_ref[p] = tile_sum
            else:
                run = run_ref[p]
                a = jnp.exp(z - csum - jnp.concatenate([run] * (TK // LANES), axis=1))
                acc_ref[p] += _dot(a.astype(BF16), v)
                run_ref[p] = run + tile_sum

        stages = (scores, suffix_sums, weights_times_v)
        for tick in range(n_pairs + len(stages) - 1):
            for k, stage in enumerate(stages):
                if 0 <= tick - k < n_pairs:
                    stage(tick - k)

    tile(i, True)

    def body(step, carry):
        tile(i - 1 - step, False)
        return carry

    lax.fori_loop(0, i, body, 0)

    for pair in range(n_pairs):
        cols = slice(pair * LANES, (pair + 1) * LANES)
        acc = acc_ref[pair]
        ssq = jnp.sum(jnp.where(mine, acc * acc, 0.0), axis=-1, keepdims=True)
        y = acc * lax.rsqrt(ssq * (1.0 / HEAD_DIM) + EPS)
        packed = jnp.where(left, y[:TQ], y[TQ:]) * normw_ref[:, cols]
        o_ref[0, :, cols] = packed.astype(o_ref.dtype)


def _sb(q, k, v, tri, normw, b, t):
    q3 = q.reshape(b, t, q.shape[-1])
    k3 = k.reshape(b, t, k.shape[-1])
    v3 = v.reshape(b, t, v.shape[-1])
    width = SB_HEADS * HEAD_DIM
    return pl.pallas_call(
        _sb_kernel,
        grid=(b, t // TQ),
        in_specs=[
            pl.BlockSpec((1, TQ, q3.shape[-1]), lambda bi, qi: (bi, qi, 0)),
            pl.BlockSpec((1, t, width), lambda bi, qi: (bi, 0, 0)),
            pl.BlockSpec((1, t, width), lambda bi, qi: (bi, 0, 0)),
            pl.BlockSpec(tri.shape, lambda bi, qi: (0, 0)),
            pl.BlockSpec(normw.shape, lambda bi, qi: (0, 0)),
        ],
        out_specs=pl.BlockSpec((1, TQ, width), lambda bi, qi: (bi, qi, 0)),
        out_shape=jax.ShapeDtypeStruct((b, t, width), BF16),
        scratch_shapes=[pltpu.VMEM((SB_HEADS // 2, 2 * TQ, LANES), F32)] * 2,
        compiler_params=_params("parallel", "arbitrary"),
        name="stickbreak",
    )(q3, k3, v3, tri, normw)


def _mixffn_kernel(x_ref, on_ref, os_ref, xh_ref, onh_ref, osh_ref, wn_ref, ws_ref, n2_ref,
                   wup_ref, cw_ref, cb_ref, wdown_ref, fw_ref, o_ref, up_ref, act_ref, *, d_ff):
    i = pl.program_id(1)
    tile = x_ref.shape[1]

    def mix(x, o_nsa, o_sb):
        h = x + _dot(o_nsa, wn_ref[...]) + _dot(o_sb, ws_ref[...])
        u = h * lax.rsqrt(jnp.mean(h * h, axis=-1, keepdims=True) + EPS) * n2_ref[...]
        return h, u.astype(BF16)

    h, u = mix(x_ref[0], on_ref[0], os_ref[0])
    _, u_halo = mix(xh_ref[0], onh_ref[0], osh_ref[0])
    halo = jnp.where(i > 0, u_halo, jnp.zeros_like(u_halo))
    ue = jnp.concatenate([halo, u], axis=0)

    def conv(slot, c0):
        w = cw_ref[:, c0:c0 + FFN_CHUNK]
        taps = [up_ref[slot, HALO - k:HALO - k + tile] for k in range(CONV_WIDTH)]
        return (w[2:3] * taps[0] + w[1:2] * taps[1] + w[0:1] * taps[2]
                + cb_ref[:, c0:c0 + FFN_CHUNK])

    def up_project(c):
        slot = 2 * (c % 2)
        up_ref[slot] = _dot(ue, wup_ref[:, c * FFN_CHUNK:(c + 1) * FFN_CHUNK])
        up_ref[slot + 1] = _dot(ue, wup_ref[:, d_ff + c * FFN_CHUNK:d_ff + (c + 1) * FFN_CHUNK])

    n_chunks = d_ff // FFN_CHUNK
    up_project(0)
    for c in range(n_chunks):
        g0, v0 = c * FFN_CHUNK, d_ff + c * FFN_CHUNK
        slot = 2 * (c % 2)
        if c + 1 < n_chunks:
            up_project(c + 1)
        gate = conv(slot, g0)
        val = conv(slot + 1, v0)
        act = gate * (1.0 / (1.0 + jnp.exp(-gate))) * val
        act_ref[:, g0:g0 + FFN_CHUNK] = act.astype(BF16)
    acc = h + _dot(act_ref[...], wdown_ref[...])
    y = acc * lax.rsqrt(jnp.mean(acc * acc, axis=-1, keepdims=True) + EPS) * fw_ref[...]
    o_ref[0] = y.astype(o_ref.dtype)


def _mixffn(x, o_nsa, o_sb, w_n, w_s, norm2_w, w_up, conv_w, conv_b, w_down, final_w):
    b, t, d = x.shape
    d_ff = w_down.shape[0]
    per = ROW_TILE // HALO
    tile = lambda a: pl.BlockSpec((1, ROW_TILE, a.shape[-1]), lambda bi, ti: (bi, ti, 0))
    halo = lambda a: pl.BlockSpec((1, HALO, a.shape[-1]), lambda bi, ti: (bi, jnp.maximum(ti * per - 1, 0), 0))
    const = lambda a: pl.BlockSpec(a.shape, lambda bi, ti: (0, 0), pipeline_mode=pl.Buffered(1))
    weights = (w_n, w_s, norm2_w, w_up, conv_w, conv_b, w_down, final_w)
    return pl.pallas_call(
        functools.partial(_mixffn_kernel, d_ff=d_ff),
        grid=(b, t // ROW_TILE),
        in_specs=[tile(x), tile(o_nsa), tile(o_sb), halo(x), halo(o_nsa), halo(o_sb)]
                 + [const(w) for w in weights],
        out_specs=pl.BlockSpec((1, ROW_TILE, d), lambda bi, ti: (bi, ti, 0)),
        out_shape=jax.ShapeDtypeStruct((b, t, d), F32),
        scratch_shapes=[pltpu.VMEM((4, HALO + ROW_TILE, FFN_CHUNK), F32),
                        pltpu.VMEM((ROW_TILE, d_ff), BF16)],
        compiler_params=_params("parallel", "arbitrary"),
        name="mixffn",
    )(x, o_nsa, o_sb, x, o_nsa, o_sb, *weights)


def _input_weights(w):
    d = w.shape[0]
    nq, kvw = NSA_HEADS * HEAD_DIM, NSA_GROUPS * HEAD_DIM
    scale = HEAD_DIM ** -0.5
    o = 0
    q_n = w[:, o:o + nq] * scale; o += nq
    kc = w[:, o:o + kvw]; o += kvw
    vc = w[:, o:o + kvw]; o += kvw
    ks = w[:, o:o + kvw]; o += kvw
    vs = w[:, o:o + kvw]; o += kvw
    kw = w[:, o:o + kvw]; o += kvw
    vw = w[:, o:o + kvw]; o += kvw
    gl = w[:, o:o + NSA_HEADS * 3]; o += NSA_HEADS * 3
    sbw = SB_HEADS * HEAD_DIM
    q_s = w[:, o:o + sbw] * scale; o += sbw
    k_s = w[:, o:o + sbw]; o += sbw
    v_s = w[:, o:o + sbw]; o += sbw
    gl = jnp.pad(gl, ((0, 0), (0, LANES - gl.shape[1])))
    pieces = (q_n, jnp.concatenate([kc, vc], axis=1), jnp.concatenate([ks, vs, kw, vw], axis=1),
              gl, q_s, k_s, v_s)
    return jnp.concatenate(pieces, axis=1).astype(BF16), tuple(p.shape[1] for p in pieces)


def _toeplitz_kernel(w_ref, o_ref, *, n_cols):
    n_rows = o_ref.shape[2]
    for k in range(w_ref.shape[1]):
        x = jnp.broadcast_to(w_ref[0, k], (n_rows, w_ref.shape[-1]))
        o_ref[0, k, :, :n_cols] = pltpu.roll(x, 0, 1, stride=1, stride_axis=0)[:, :n_cols]
        if n_cols < o_ref.shape[-1]:
            o_ref[0, k, :, n_cols:] = jnp.full((n_rows, o_ref.shape[-1] - n_cols), NEG_INF, F32)


def _toeplitz_rows(w, n_rows, n_cols, neg_cols):
    h, k, width = w.shape
    return pl.pallas_call(
        functools.partial(_toeplitz_kernel, n_cols=n_cols),
        grid=(h,),
        in_specs=[pl.BlockSpec((1, k, 1, width), lambda i: (i, 0, 0, 0))],
        out_specs=pl.BlockSpec((1, k, n_rows, n_cols + neg_cols), lambda i: (i, 0, 0, 0)),
        out_shape=jax.ShapeDtypeStruct((h, k, n_rows, n_cols + neg_cols), F32),
        compiler_params=_params("parallel"),
        name="toeplitz",
    )(w.reshape(h, k, 1, width))


def _bias_tables(rel_bias, t):
    nq = t // NTQ
    rb = rel_bias.T.astype(F32)

    def by_signed(d, ok):
        onehot = (_t5_bucket_np(d).reshape(1, -1) == np.arange(N_BUCKETS)[:, None]) & ok.reshape(1, -1)
        vals = jnp.dot(rb, jnp.asarray(onehot, F32), precision=lax.Precision.HIGHEST)
        vals = vals + jnp.asarray(np.where(ok.reshape(1, -1), 0.0, NEG_INF), F32)
        return vals.reshape((rb.shape[0],) + d.shape)

    def toeplitz(d, ok, n_rows, n_cols, neg_cols=0):
        p = d.shape[-1]
        assert p == n_rows + n_cols - 1
        width = -(-p // LANES) * LANES
        src = (n_cols - 1 - np.arange(width)) % width
        w = by_signed(d[..., np.minimum(src, p - 1)], ok[..., np.minimum(src, p - 1)] & (src < p))
        return _toeplitz_rows(w.reshape(rb.shape[0], -1, width), n_rows, n_cols, neg_cols).reshape(
            w.shape[:-1] + (n_rows, n_cols + neg_cols))

    def stacked(tb):
        return tb.reshape(NSA_GROUPS, NSA_REP * NTQ, tb.shape[-1])

    span = WINDOW + NTQ
    dw = np.arange(NTQ + span - 1) - (span - 1) + WINDOW
    biasw = stacked(toeplitz(dw, (dw >= 0) & (dw < WINDOW), NTQ, span, neg_cols=NTK))
    assert NTK > MAX_DISTANCE
    ds = np.arange(NTQ + 3 * NTK - 1) - (3 * NTK - 1) + 2 * NTK
    biass = stacked(toeplitz(ds, ds >= 0, NTQ, 3 * NTK))
    n_cmp = (t - CMP_BLOCK) // CMP_STRIDE + 1
    n_a = t // CMP_STRIDE
    dc = (CMP_STRIDE * (np.arange(n_a + LANES - 1)[None, :] - (LANES - 1))
          + np.arange(CMP_STRIDE)[:, None] - (CMP_BLOCK - 1))
    bc = toeplitz(dc, dc >= 0, n_a, LANES)
    bc = jnp.where(jnp.asarray(np.arange(LANES) < n_cmp), bc, NEG_INF)
    bc = bc.transpose(0, 2, 1, 3).reshape(NSA_GROUPS, NSA_REP, nq, NTQ, LANES)
    biasc = bc.transpose(0, 2, 1, 3, 4).reshape(NSA_GROUPS, nq, NSA_REP * NTQ, LANES)
    return biasc, biass, biasw


def _suffix_sum_matrix():
    return jnp.asarray(np.arange(TK)[:, None] >= np.arange(TK)[None, :], BF16)


def _selection_constants(t):
    n_cmp = (t - CMP_BLOCK) // CMP_STRIDE + 1
    n_sel = t // SEL_BLOCK
    n_selp = -(-n_sel // 8) * 8
    n = np.arange(LANES)[None, :]
    j = np.arange(n_selp)[:, None]
    ovl = ((CMP_STRIDE * n < SEL_BLOCK * (j + 1)) & (CMP_STRIDE * n + CMP_BLOCK > SEL_BLOCK * j)
           & (n < n_cmp) & (j < n_sel))
    ovl = np.concatenate([ovl, np.ones((1, LANES), bool), np.zeros((7, LANES), bool)], axis=0)
    own = (np.arange(t)[:, None] // SEL_BLOCK) == np.arange(LANES)[None, :]
    blockneg = jnp.where(jnp.asarray(own), NEG_INF, 0.0).astype(BF16)
    return jnp.asarray(ovl, BF16), blockneg


def kernel(x, norm1_w, w_in, cmp_pos_k, cmp_pos_v, cmp_k_w1, cmp_k_w2, cmp_v_w1, cmp_v_w2, gate_b,
           nsa_out_norm_w, sb_out_norm_w, w_out, norm2_w, w_up, conv_w, conv_b, w_down, rel_bias,
           final_norm_w):
    b, t, d = x.shape
    assert t % ROW_TILE == 0 and t % NTQ == 0 and t >= WINDOW + NTQ and (b * t) % ROW_TILE == 0
    assert (t - CMP_BLOCK) // CMP_STRIDE + 1 < LANES + 1 and t // CMP_STRIDE == LANES
    assert w_in.shape[0] == 1, "single-layer block: the closing norm is fused into the FFN kernel"
    nsa_w = NSA_HEADS * HEAD_DIM
    l = 0

    biasc, biass, biasw = _bias_tables(rel_bias, t)
    ovl, blockneg = _selection_constants(t)
    tri = _suffix_sum_matrix()

    h = x.reshape(b * t, d)
    w_all, splits = _input_weights(w_in[l])
    gb = jnp.pad(gate_b[l], (0, LANES - gate_b.shape[1])).reshape(1, LANES)
    qn, kcvc, kv, gates, sbq, sbk, sbv = _inproj(h, norm1_w[l].reshape(1, d), w_all, gb, splits, b, t)

    eye = jnp.eye(NSA_GROUPS, dtype=F32)

    def group_diag(w):
        w = jnp.einsum("...rc,gk->...grkc", w, eye)
        return w.reshape(-1, NSA_GROUPS * w.shape[-1])

    pos = jnp.stack([jnp.tile(p_.reshape(2, CMP_STRIDE, 1, HEAD_DIM), (1, 1, NSA_GROUPS, 1)).reshape(2, 1, -1)
                     for p_ in (cmp_pos_k[l], cmp_pos_v[l])])
    w1 = jnp.stack([jnp.stack([group_diag(w_.reshape(2, CMP_STRIDE, HEAD_DIM, -1)[hf]) for hf in range(2)])
                    for w_ in (cmp_k_w1[l], cmp_v_w1[l])]).astype(BF16)
    w2 = jnp.stack([group_diag(w_) for w_ in (cmp_k_w2[l], cmp_v_w2[l])]).astype(BF16)
    kcmp, vcmp = _compress(kcvc, pos, w1, w2)

    o_nsa = _nsa(qn, kv, kcmp, vcmp, gates, biasc, biass, biasw, ovl, blockneg,
                 nsa_out_norm_w[l].reshape(1, nsa_w), b, t)
    o_sb = _sb(sbq, sbk, sbv, tri, sb_out_norm_w[l].reshape(1, -1), b, t)

    wo = w_out[l].astype(BF16)
    return _mixffn(x, o_nsa, o_sb, wo[:nsa_w], wo[nsa_w:], norm2_w[l].reshape(1, d),
                   w_up[l].astype(BF16), conv_w[l], conv_b[l].reshape(1, -1),
                   w_down[l].astype(BF16), final_norm_w.reshape(1, d))
```

```python
import functools
import math

import jax
import jax.numpy as jnp
import numpy as np
from jax import lax
from jax.experimental import pallas as pl
from jax.experimental.pallas import tpu as pltpu

F32 = jnp.float32
BF16 = jnp.bfloat16

HEAD_DIM = 64
NSA_HEADS = 8
SB_HEADS = 8
NSA_GROUPS = 2
NSA_REP = NSA_HEADS // NSA_GROUPS
CMP_BLOCK = 32
CMP_STRIDE = 16
SEL_BLOCK = 64
SEL_TOP = 8
WINDOW = 512
N_BUCKETS = 32
MAX_DISTANCE = 128
CONV_WIDTH = 3
EPS = 1e-6
NEG_INF = -1e30
FORCED_BONUS = 1e6
TINY = 1e-30

LANES = 128
SUBLANES = 8
TQ = 256
TK = 256
NTQ = 256
NTK = 256
ROW_TILE = 512
FFN_CHUNK = 256
HALO = 16
VMEM_LIMIT = 56 * 1024 * 1024
WIN_TILES = WINDOW // NTK + 1


def _dot(a, b):
    return jnp.dot(a, b, preferred_element_type=F32)


def _dot_nt(a, b):
    return lax.dot_general(a, b, (((1,), (1,)), ((), ())), preferred_element_type=F32)


def _params(*sem):
    return pltpu.CompilerParams(dimension_semantics=sem, vmem_limit_bytes=VMEM_LIMIT)


def _t5_bucket_np(dist):
    n = np.maximum(dist, 0)
    max_exact = N_BUCKETS // 2
    nf = np.maximum(n, 1).astype(np.float32)
    log_b = max_exact + (np.log(nf / np.float32(max_exact)) / np.float32(math.log(MAX_DISTANCE / max_exact))
                         * np.float32(N_BUCKETS - max_exact)).astype(np.int32)
    log_b = np.minimum(log_b, N_BUCKETS - 1)
    return np.where(n < max_exact, n, log_b)


def _inproj_kernel(x_ref, nw_ref, w_ref, gb_ref, qn_ref, kcvc_ref, kv_ref, gates_ref,
                   sbq_ref, sbk_ref, sbv_ref, stage_ref, *, splits):
    x = x_ref[...]
    u = x * lax.rsqrt(jnp.mean(x * x, axis=-1, keepdims=True) + EPS) * nw_ref[...]
    u = u.astype(BF16)
    outs = (qn_ref, kcvc_ref, kv_ref, gates_ref, sbq_ref, sbk_ref, sbv_ref)
    start = 0
    for ref, size in zip(outs, splits):
        r = _dot(u, w_ref[:, start:start + size])
        if ref is gates_ref:
            r = jax.nn.sigmoid(r + gb_ref[...])
        if ref is kcvc_ref:
            for c in range(size // LANES):
                stage_ref[c] = r[:, c * LANES:(c + 1) * LANES]
                for p in range(CMP_STRIDE):
                    ref[0, p, :, c * LANES:(c + 1) * LANES] = (
                        stage_ref[c, pl.ds(p, ROW_TILE // CMP_STRIDE, stride=CMP_STRIDE), :])
        elif ref is qn_ref:
            half = lax.broadcasted_iota(jnp.int32, (r.shape[0], LANES), 1) // HEAD_DIM
            for h in range(NSA_HEADS):
                g = h // NSA_REP
                pair = r[:, (h // 2) * LANES:(h // 2 + 1) * LANES]
                if h % 2 != g:
                    pair = pltpu.roll(pair, HEAD_DIM, axis=1)
                ref[:, h * LANES:(h + 1) * LANES] = jnp.where(half == g, pair, 0.0).astype(ref.dtype)
        else:
            ref[...] = r.astype(ref.dtype)
        start += size


def _inproj(x2, norm_w, w_all, gate_b_pad, splits, b, t):
    n, d = x2.shape
    dts = (BF16, F32, BF16, F32, BF16, BF16, BF16)
    per_seq = t // ROW_TILE
    chunks = ROW_TILE // CMP_STRIDE
    widths = (NSA_HEADS * LANES,) + tuple(splits[1:])
    out_specs = [pl.BlockSpec((ROW_TILE, s), lambda i: (i, 0)) for s in widths]
    out_shape = [jax.ShapeDtypeStruct((n, s), dt) for s, dt in zip(widths, dts)]
    out_specs[1] = pl.BlockSpec((1, CMP_STRIDE, chunks, splits[1]), lambda i: (i // per_seq, 0, i % per_seq, 0))
    out_shape[1] = jax.ShapeDtypeStruct((b, CMP_STRIDE, t // CMP_STRIDE, splits[1]), F32)
    return pl.pallas_call(
        functools.partial(_inproj_kernel, splits=splits),
        grid=(n // ROW_TILE,),
        in_specs=[
            pl.BlockSpec((ROW_TILE, d), lambda i: (i, 0)),
            pl.BlockSpec((1, d), lambda i: (0, 0)),
            pl.BlockSpec(w_all.shape, lambda i: (0, 0)),
            pl.BlockSpec((1, LANES), lambda i: (0, 0)),
        ],
        out_specs=out_specs,
        out_shape=out_shape,
        scratch_shapes=[pltpu.VMEM((splits[1] // LANES, ROW_TILE, LANES), F32)],
        compiler_params=_params("parallel"),
        name="inproj",
    )(x2, norm_w, w_all, gate_b_pad)


def _gelu_tanh(x):
    return 0.5 * x * (1.0 + jnp.tanh(math.sqrt(2.0 / math.pi) * (x + 0.044715 * (x * x * x))))


def _compress_kernel(c_ref, pos_ref, w1_ref, w2_ref, kc_ref, vc_ref):
    n_chunk = c_ref.shape[2]
    row = lax.broadcasted_iota(jnp.int32, (n_chunk, LANES), 0)
    for kv, out_ref in ((0, kc_ref), (1, vc_ref)):
        def half(hf):
            x = jnp.concatenate(
                [(c_ref[0, p, :, kv * LANES:(kv + 1) * LANES]
                  + pos_ref[kv, hf, :, p * LANES:(p + 1) * LANES]).astype(BF16)
                 for p in range(CMP_STRIDE)], axis=1)
            return _dot(x, w1_ref[kv, hf])

        hidden = half(0) + pltpu.roll(half(1), n_chunk - 1, axis=0)
        out = _dot(_gelu_tanh(hidden).astype(BF16), w2_ref[kv])
        out_ref[0] = jnp.where(row < n_chunk - 1, out, 0.0).astype(out_ref.dtype)


def _compress(tokens, pos, w1, w2):
    b, _, n_chunk, _ = tokens.shape
    out = jax.ShapeDtypeStruct((b, n_chunk, LANES), BF16)
    const = lambda a: pl.BlockSpec(a.shape, lambda i: (0,) * a.ndim, pipeline_mode=pl.Buffered(1))
    return pl.pallas_call(
        _compress_kernel,
        grid=(b,),
        in_specs=[pl.BlockSpec((1,) + tokens.shape[1:], lambda i: (i, 0, 0, 0)), const(pos), const(w1), const(w2)],
        out_specs=[pl.BlockSpec((1, n_chunk, LANES), lambda i: (i, 0, 0))] * 2,
        out_shape=[out, out],
        compiler_params=_params("parallel"),
        name="compress",
    )(tokens, pos, w1, w2)


def _nsa_kernel(q_ref, ks_ref, vs_ref, kw_ref, vw_ref, kc_ref, vc_ref, gates_ref,
                biasc_ref, biass_ref, biasw_ref, ovl_ref, blockneg_ref, normw_ref,
                o_ref, s_sel_ref, s_win_ref, m_sel_ref, acc_sel_ref, owin_ref, gsel_ref, qsel_ref, ocmp_ref):
    i = pl.program_id(1)
    t0 = i * NTQ
    rows = NSA_REP * NTQ
    groups = range(NSA_GROUPS)
    lane = lax.broadcasted_iota(jnp.int32, (rows, LANES), 1)
    ones_v = jnp.ones((NTK, LANES), BF16)
    gates = gates_ref[0]

    n_selp = ovl_ref.shape[0] - SUBLANES
    blk = lax.broadcasted_iota(jnp.int32, (n_selp, NTQ), 0)
    cur = (t0 + lax.broadcasted_iota(jnp.int32, (n_selp, NTQ), 1)) // SEL_BLOCK
    causal_blk = blk <= cur
    forced = (blk == 0) | (blk == cur) | (blk == cur - 1)

    def probabilities(s, m):
        return jnp.exp(s - jnp.concatenate([m, m], axis=1)).astype(BF16)

    qgs = [jnp.concatenate(
        [q_ref[0, :, (g * NSA_REP + r) * LANES:(g * NSA_REP + r + 1) * LANES] for r in range(NSA_REP)],
        axis=0) for g in groups]

    lcs = [_dot_nt(qgs[g], kc_ref[0]) + biasc_ref[g, 0] for g in groups]

    win_tiles = []
    for k in range(WIN_TILES):
        j = i - (WIN_TILES - 1) + k
        win_tiles.append((pl.multiple_of(jnp.maximum(j, 0) * NTK, NTK),
                          pl.multiple_of(jnp.where(j < 0, WIN_TILES, k) * NTK, NTK)))
    m_win = [jnp.full((rows, LANES), NEG_INF, F32) for g in groups]
    for k, (off, boff) in enumerate(win_tiles):
        key = kw_ref[0, pl.ds(off, NTK), :]
        raw = [_dot_nt(qgs[g], key) for g in groups]
        for g in groups:
            s = raw[g] + biasw_ref[g, :, pl.ds(boff, NTK)]
            s_win_ref[g, :, k * NTK:(k + 1) * NTK] = s
            m_win[g] = jnp.maximum(m_win[g], jnp.maximum(s[:, :LANES], s[:, LANES:]))

    ecs = []
    for g in groups:
        mc = jnp.maximum(jnp.max(lcs[g], axis=-1, keepdims=True), 0.1 * NEG_INF)
        ec = jnp.exp(lcs[g] - mc).astype(BF16)
        oc = _dot(ec, jnp.concatenate([vc_ref[0], ones_v[:LANES]], axis=1))
        ocmp_ref[g] = oc[:, :LANES] * (1.0 / jnp.maximum(oc[:, LANES:], TINY))
        ecs.append(ec)

    for g in groups:
        imp4 = _dot_nt(ovl_ref[...], ecs[g])
        imp4 = imp4[:n_selp] * (1.0 / jnp.maximum(imp4[n_selp:n_selp + 1], TINY))
        imp = imp4[:, 0:NTQ]
        for r in range(1, NSA_REP):
            imp = imp + imp4[:, r * NTQ:(r + 1) * NTQ]
        score = jnp.where(causal_blk, imp + jnp.where(forced, FORCED_BONUS, 0.0), NEG_INF)
        rank = jnp.zeros((n_selp, NTQ), F32)
        for j in range(n_selp):
            other = score[j:j + 1, :]
            ahead = (other > score) | ((other == score) & (blk > j))
            rank = rank + jnp.where(ahead, 1.0, 0.0)
        unsel_t = jnp.where((rank < SEL_TOP) & causal_blk, 0.0, 1.0)
        unsel_t = jnp.concatenate([unsel_t, jnp.zeros((LANES - n_selp, NTQ), F32)], axis=0)
        unsel = unsel_t.T.astype(BF16)

        qsel_ref[g] = jnp.concatenate([qgs[g], jnp.concatenate([unsel] * NSA_REP, axis=0)], axis=1)

    for g in groups:
        m_win[g] = jnp.broadcast_to(jnp.max(m_win[g], axis=-1, keepdims=True), (rows, LANES))
    acc_win = [jnp.zeros((rows, 2 * LANES), F32) for g in groups]
    for k, (off, _) in enumerate(win_tiles):
        v = jnp.concatenate([vw_ref[0, pl.ds(off, NTK), :], ones_v], axis=1)
        ps = [probabilities(s_win_ref[g, :, k * NTK:(k + 1) * NTK], m_win[g]) for g in groups]
        for g in groups:
            acc_win[g] = acc_win[g] + _dot(ps[g], v)

    def gate(g, branch):
        return jnp.concatenate(
            [gates[:, (g * NSA_REP + r) * 3 + branch:(g * NSA_REP + r) * 3 + branch + 1]
             for r in range(NSA_REP)], axis=0)

    for g in groups:
        ocmp_ref[g] = gate(g, 0) * ocmp_ref[g]
        gsel_ref[g] = jnp.broadcast_to(gate(g, 1), (rows, LANES))
        owin_ref[g] = gate(g, 2) * (acc_win[g][:, :LANES] * (1.0 / acc_win[g][:, LANES:]))

    m_sel_ref[...] = jnp.full(m_sel_ref.shape, NEG_INF, F32)
    acc_sel_ref[...] = jnp.zeros(acc_sel_ref.shape, F32)

    n_double = (i + 1) // 2

    def sel_logits(j, n_tiles):
        off = pl.multiple_of(j * NTK, NTK)
        width = n_tiles * NTK
        key = jnp.concatenate([ks_ref[0, pl.ds(off, width), :], blockneg_ref[pl.ds(off, width), :]], axis=1)
        raw = [_dot_nt(qsel_ref[g], key) for g in groups]
        for g in groups:
            m = None
            for k in range(n_tiles):
                boff = pl.multiple_of((2 - jnp.minimum(i - j - k, 2)) * NTK, NTK)
                s = raw[g][:, k * NTK:(k + 1) * NTK] + biass_ref[g, :, pl.ds(boff, NTK)]
                s_sel_ref[g, :, pl.ds(pl.multiple_of(off + k * NTK, NTK), NTK)] = s
                tile_max = jnp.maximum(s[:, :LANES], s[:, LANES:])
                m = tile_max if m is None else jnp.maximum(m, tile_max)
            m = jnp.broadcast_to(jnp.max(m, axis=-1, keepdims=True), (rows, LANES))
            m_sel_ref[g] = jnp.maximum(m_sel_ref[g], m)

    def sel_weights(j, n_tiles):
        off = pl.multiple_of(j * NTK, NTK)
        width = n_tiles * NTK
        v = jnp.concatenate([vs_ref[0, pl.ds(off, width), :], jnp.ones((width, LANES), BF16)], axis=1)
        ps = []
        for g in groups:
            m = m_sel_ref[g]
            ps.append(jnp.exp(s_sel_ref[g, :, pl.ds(off, width)]
                              - jnp.concatenate([m] * (width // LANES), axis=1)).astype(BF16))
        for g in groups:
            acc_sel_ref[g] += _dot(ps[g], v)

    def sweep(tile_fn):
        def double(d, c):
            tile_fn(2 * d, 2)
            return c

        def single(_, c):
            tile_fn(i, 1)
            return c

        lax.fori_loop(0, n_double, double, 0)
        lax.fori_loop(0, (i + 1) % 2, single, 0)

    sweep(sel_logits)
    sweep(sel_weights)

    for g in groups:
        o_sel = acc_sel_ref[g, :, :LANES] * (1.0 / acc_sel_ref[g, :, LANES:])
        o = ocmp_ref[g] + gsel_ref[g] * o_sel + owin_ref[g]
        mine = (lane // HEAD_DIM) == g
        ssq = jnp.sum(jnp.where(mine, o * o, 0.0), axis=-1, keepdims=True)
        y = o * lax.rsqrt(ssq * (1.0 / HEAD_DIM) + EPS)
        left = lax.broadcasted_iota(jnp.int32, (NTQ, LANES), 1) < HEAD_DIM
        for pair in range(NSA_REP // 2):
            even = y[(2 * pair) * NTQ:(2 * pair + 1) * NTQ]
            odd = y[(2 * pair + 1) * NTQ:(2 * pair + 2) * NTQ]
            if g == 0:
                odd = pltpu.roll(odd, HEAD_DIM, axis=1)
            else:
                even = pltpu.roll(even, HEAD_DIM, axis=1)
            cb = g * (NSA_REP // 2) + pair
            packed = jnp.where(left, even, odd) * normw_ref[:, cb * LANES:(cb + 1) * LANES]
            o_ref[0, :, cb * LANES:(cb + 1) * LANES] = packed.astype(o_ref.dtype)


def _nsa(qn, kv, kcmp, vcmp, gates, biasc, biass, biasw, ovl, blockneg, normw, b, t):
    nq = t // NTQ
    rows = NSA_REP * NTQ
    qn3 = qn.reshape(b, t, qn.shape[-1])
    kv3 = kv.reshape(b, t, kv.shape[-1])
    gates3 = gates.reshape(b, t, LANES)
    kv_spec = lambda c: pl.BlockSpec((1, t, LANES), lambda bi, qi, c=c: (bi, 0, c))
    const = lambda a: pl.BlockSpec(a.shape, lambda bi, qi: (0,) * a.ndim, pipeline_mode=pl.Buffered(1))
    return pl.pallas_call(
        _nsa_kernel,
        grid=(b, nq),
        in_specs=[
            pl.BlockSpec((1, NTQ, qn3.shape[-1]), lambda bi, qi: (bi, qi, 0)),
            kv_spec(0), kv_spec(1), kv_spec(2), kv_spec(3),
            pl.BlockSpec((1,) + kcmp.shape[1:], lambda bi, qi: (bi, 0, 0)),
            pl.BlockSpec((1,) + vcmp.shape[1:], lambda bi, qi: (bi, 0, 0)),
            pl.BlockSpec((1, NTQ, LANES), lambda bi, qi: (bi, qi, 0)),
            pl.BlockSpec((NSA_GROUPS, 1, rows, LANES), lambda bi, qi: (0, qi, 0, 0)),
            const(biass), const(biasw), const(ovl), const(blockneg), const(normw),
        ],
        out_specs=pl.BlockSpec((1, NTQ, NSA_HEADS * HEAD_DIM), lambda bi, qi: (bi, qi, 0)),
        out_shape=jax.ShapeDtypeStruct((b, t, NSA_HEADS * HEAD_DIM), BF16),
        scratch_shapes=[
            pltpu.VMEM((NSA_GROUPS, rows, t), F32),
            pltpu.VMEM((NSA_GROUPS, rows, WIN_TILES * NTK), F32),
            pltpu.VMEM((NSA_GROUPS, rows, LANES), F32),
            pltpu.VMEM((NSA_GROUPS, rows, 2 * LANES), F32),
            pltpu.VMEM((NSA_GROUPS, rows, LANES), F32),
            pltpu.VMEM((NSA_GROUPS, rows, LANES), F32),
            pltpu.VMEM((NSA_GROUPS, rows, 2 * LANES), BF16),
            pltpu.VMEM((NSA_GROUPS, rows, LANES), F32),
        ],
        compiler_params=_params("parallel", "arbitrary"),
        name="nsa",
    )(qn3, kv3, kv3, kv3, kv3, kcmp, vcmp, gates3, biasc, biass, biasw, ovl, blockneg, normw)


def _sb_kernel(q_ref, k_ref, v_ref, tri_ref, normw_ref, o_ref, acc_ref, run_ref):
    i = pl.program_id(1)
    rows = 2 * TQ
    n_pairs = SB_HEADS // 2
    lane = lax.broadcasted_iota(jnp.int32, (rows, LANES), 1)
    row = lax.broadcasted_iota(jnp.int32, (rows, LANES), 0)
    strict = lax.broadcasted_iota(jnp.int32, (rows, TK), 1) < (
        lax.broadcasted_iota(jnp.int32, (rows, TK), 0) & (TQ - 1))
    mine = (lane // HEAD_DIM) == (row // TQ)
    left = lax.broadcasted_iota(jnp.int32, (TQ, LANES), 1) < HEAD_DIM

    def tile(j, diag):
        off = pl.multiple_of(j * TK, TK)
        cols = [slice(p * LANES, (p + 1) * LANES) for p in range(n_pairs)]
        zs, csums = {}, {}

        def scores(p):
            q2 = q_ref[0, :, cols[p]]
            zero = jnp.zeros_like(q2)
            qp = jnp.concatenate([jnp.where(left, q2, zero), jnp.where(left, zero, q2)],
                                 axis=0)
            zs[p] = _dot_nt(qp, k_ref[0, pl.ds(off, TK), cols[p]])

        def suffix_sums(p):
            z = zs[p]
            zb = z.astype(BF16)
            sp = jnp.maximum(zb, 0) + jnp.log(1 + jnp.exp(-jnp.abs(zb)))
            if diag:
                sp = jnp.where(strict, sp, jnp.zeros_like(sp))
            csums[p] = _dot(sp, tri_ref[...])

        def weights_times_v(p):
            v = v_ref[0, pl.ds(off, TK), cols[p]]
            z, csum = zs.pop(p), csums.pop(p)
            tile_sum = jnp.broadcast_to(csum[:, 0:1], (rows, LANES))
            if diag:
                a = jnp.where(strict, jnp.exp(z - csum), 0.0)
                acc_ref[p] = _dot(a.astype(BF16), v)
                run_ref[p] = tile_sum
            else:
                run = run_ref[p]
                a = jnp.exp(z - csum - jnp.concatenate([run] * (TK // LANES), axis=1))
                acc_ref[p] += _dot(a.astype(BF16), v)
                run_ref[p] = run + tile_sum

        stages = (scores, suffix_sums, weights_times_v)
        for tick in range(n_pairs + len(stages) - 1):
            for k, stage in enumerate(stages):
                if 0 <= tick - k < n_pairs:
                    stage(tick - k)

    tile(i, True)

    def body(step, carry):
        tile(i - 1 - step, False)
        return carry

    lax.fori_loop(0, i, body, 0)

    for pair in range(n_pairs):
        cols = slice(pair * LANES, (pair + 1) * LANES)
        acc = acc_ref[pair]
        ssq = jnp.sum(jnp.where(mine, acc * acc, 0.0), axis=-1, keepdims=True)
        y = acc * lax.rsqrt(ssq * (1.0 / HEAD_DIM) + EPS)
        packed = jnp.where(left, y[:TQ], y[TQ:]) * normw_ref[:, cols]
        o_ref[0, :, cols] = packed.astype(o_ref.dtype)


def _sb(q, k, v, tri, normw, b, t):
    q3 = q.reshape(b, t, q.shape[-1])
    k3 = k.reshape(b, t, k.shape[-1])
    v3 = v.reshape(b, t, v.shape[-1])
    width = SB_HEADS * HEAD_DIM
    return pl.pallas_call(
        _sb_kernel,
        grid=(b, t // TQ),
        in_specs=[
            pl.BlockSpec((1, TQ, q3.shape[-1]), lambda bi, qi: (bi, qi, 0)),
            pl.BlockSpec((1, t, width), lambda bi, qi: (bi, 0, 0)),
            pl.BlockSpec((1, t, width), lambda bi, qi: (bi, 0, 0)),
            pl.BlockSpec(tri.shape, lambda bi, qi: (0, 0)),
            pl.BlockSpec(normw.shape, lambda bi, qi: (0, 0)),
        ],
        out_specs=pl.BlockSpec((1, TQ, width), lambda bi, qi: (bi, qi, 0)),
        out_shape=jax.ShapeDtypeStruct((b, t, width), BF16),
        scratch_shapes=[pltpu.VMEM((SB_HEADS // 2, 2 * TQ, LANES), F32)] * 2,
        compiler_params=_params("parallel", "arbitrary"),
        name="stickbreak",
    )(q3, k3, v3, tri, normw)


def _mixffn_kernel(x_ref, on_ref, os_ref, xh_ref, onh_ref, osh_ref, wn_ref, ws_ref, n2_ref,
                   wup_ref, cw_ref, cb_ref, wdown_ref, fw_ref, o_ref, up_ref, act_ref, *, d_ff):
    i = pl.program_id(1)
    tile = x_ref.shape[1]

    def mix(x, o_nsa, o_sb):
        h = x + _dot(o_nsa, wn_ref[...]) + _dot(o_sb, ws_ref[...])
        u = h * lax.rsqrt(jnp.mean(h * h, axis=-1, keepdims=True) + EPS) * n2_ref[...]
        return h, u.astype(BF16)

    h, u = mix(x_ref[0], on_ref[0], os_ref[0])
    _, u_halo = mix(xh_ref[0], onh_ref[0], osh_ref[0])
    halo = jnp.where(i > 0, u_halo, jnp.zeros_like(u_halo))
    ue = jnp.concatenate([halo, u], axis=0)

    def conv(slot, c0):
        w = cw_ref[:, c0:c0 + FFN_CHUNK]
        taps = [up_ref[slot, HALO - k:HALO - k + tile] for k in range(CONV_WIDTH)]
        return (w[2:3] * taps[0] + w[1:2] * taps[1] + w[0:1] * taps[2]
                + cb_ref[:, c0:c0 + FFN_CHUNK])

    def up_project(c):
        slot = 2 * (c % 2)
        up_ref[slot] = _dot(ue, wup_ref[:, c * FFN_CHUNK:(c + 1) * FFN_CHUNK])
        up_ref[slot + 1] = _dot(ue, wup_ref[:, d_ff + c * FFN_CHUNK:d_ff + (c + 1) * FFN_CHUNK])

    n_chunks = d_ff // FFN_CHUNK
    up_project(0)
    for c in range(n_chunks):
        g0, v0 = c * FFN_CHUNK, d_ff + c * FFN_CHUNK
        slot = 2 * (c % 2)
        if c + 1 < n_chunks:
            up_project(c + 1)
        gate = conv(slot, g0)
        val = conv(slot + 1, v0)
        act = gate * (1.0 / (1.0 + jnp.exp(-gate))) * val
        act_ref[:, g0:g0 + FFN_CHUNK] = act.astype(BF16)
    acc = h + _dot(act_ref[...], wdown_ref[...])
    y = acc * lax.rsqrt(jnp.mean(acc * acc, axis=-1, keepdims=True) + EPS) * fw_ref[...]
    o_ref[0] = y.astype(o_ref.dtype)


def _mixffn(x, o_nsa, o_sb, w_n, w_s, norm2_w, w_up, conv_w, conv_b, w_down, final_w):
    b, t, d = x.shape
    d_ff = w_down.shape[0]
    per = ROW_TILE // HALO
    tile = lambda a: pl.BlockSpec((1, ROW_TILE, a.shape[-1]), lambda bi, ti: (bi, ti, 0))
    halo = lambda a: pl.BlockSpec((1, HALO, a.shape[-1]), lambda bi, ti: (bi, jnp.maximum(ti * per - 1, 0), 0))
    const = lambda a: pl.BlockSpec(a.shape, lambda bi, ti: (0, 0), pipeline_mode=pl.Buffered(1))
    weights = (w_n, w_s, norm2_w, w_up, conv_w, conv_b, w_down, final_w)
    return pl.pallas_call(
        functools.partial(_mixffn_kernel, d_ff=d_ff),
        grid=(b, t // ROW_TILE),
        in_specs=[tile(x), tile(o_nsa), tile(o_sb), halo(x), halo(o_nsa), halo(o_sb)]
                 + [const(w) for w in weights],
        out_specs=pl.BlockSpec((1, ROW_TILE, d), lambda bi, ti: (bi, ti, 0)),
        out_shape=jax.ShapeDtypeStruct((b, t, d), F32),
        scratch_shapes=[pltpu.VMEM((4, HALO + ROW_TILE, FFN_CHUNK), F32),
                        pltpu.VMEM((ROW_TILE, d_ff), BF16)],
        compiler_params=_params("parallel", "arbitrary"),
        name="mixffn",
    )(x, o_nsa, o_sb, x, o_nsa, o_sb, *weights)


def _input_weights(w):
    d = w.shape[0]
    nq, kvw = NSA_HEADS * HEAD_DIM, NSA_GROUPS * HEAD_DIM
    scale = HEAD_DIM ** -0.5
    o = 0
    q_n = w[:, o:o + nq] * scale; o += nq
    kc = w[:, o:o + kvw]; o += kvw
    vc = w[:, o:o + kvw]; o += kvw
    ks = w[:, o:o + kvw]; o += kvw
    vs = w[:, o:o + kvw]; o += kvw
    kw = w[:, o:o + kvw]; o += kvw
    vw = w[:, o:o + kvw]; o += kvw
    gl = w[:, o:o + NSA_HEADS * 3]; o += NSA_HEADS * 3
    sbw = SB_HEADS * HEAD_DIM
    q_s = w[:, o:o + sbw] * scale; o += sbw
    k_s = w[:, o:o + sbw]; o += sbw
    v_s = w[:, o:o + sbw]; o += sbw
    gl = jnp.pad(gl, ((0, 0), (0, LANES - gl.shape[1])))
    pieces = (q_n, jnp.concatenate([kc, vc], axis=1), jnp.concatenate([ks, vs, kw, vw], axis=1),
              gl, q_s, k_s, v_s)
    return jnp.concatenate(pieces, axis=1).astype(BF16), tuple(p.shape[1] for p in pieces)


def _toeplitz_kernel(w_ref, o_ref, *, n_cols):
    n_rows = o_ref.shape[2]
    for k in range(w_ref.shape[1]):
        x = jnp.broadcast_to(w_ref[0, k], (n_rows, w_ref.shape[-1]))
        o_ref[0, k, :, :n_cols] = pltpu.roll(x, 0, 1, stride=1, stride_axis=0)[:, :n_cols]
        if n_cols < o_ref.shape[-1]:
            o_ref[0, k, :, n_cols:] = jnp.full((n_rows, o_ref.shape[-1] - n_cols), NEG_INF, F32)


def _toeplitz_rows(w, n_rows, n_cols, neg_cols):
    h, k, width = w.shape
    return pl.pallas_call(
        functools.partial(_toeplitz_kernel, n_cols=n_cols),
        grid=(h,),
        in_specs=[pl.BlockSpec((1, k, 1, width), lambda i: (i, 0, 0, 0))],
        out_specs=pl.BlockSpec((1, k, n_rows, n_cols + neg_cols), lambda i: (i, 0, 0, 0)),
        out_shape=jax.ShapeDtypeStruct((h, k, n_rows, n_cols + neg_cols), F32),
        compiler_params=_params("parallel"),
        name="toeplitz",
    )(w.reshape(h, k, 1, width))


def _bias_tables(rel_bias, t):
    nq = t // NTQ
    rb = rel_bias.T.astype(F32)

    def by_signed(d, ok):
        onehot = (_t5_bucket_np(d).reshape(1, -1) == np.arange(N_BUCKETS)[:, None]) & ok.reshape(1, -1)
        vals = jnp.dot(rb, jnp.asarray(onehot, F32), precision=lax.Precision.HIGHEST)
        vals = vals + jnp.asarray(np.where(ok.reshape(1, -1), 0.0, NEG_INF), F32)
        return vals.reshape((rb.shape[0],) + d.shape)

    def toeplitz(d, ok, n_rows, n_cols, neg_cols=0):
        p = d.shape[-1]
        assert p == n_rows + n_cols - 1
        width = -(-p // LANES) * LANES
        src = (n_cols - 1 - np.arange(width)) % width
        w = by_signed(d[..., np.minimum(src, p - 1)], ok[..., np.minimum(src, p - 1)] & (src < p))
        return _toeplitz_rows(w.reshape(rb.shape[0], -1, width), n_rows, n_cols, neg_cols).reshape(
            w.shape[:-1] + (n_rows, n_cols + neg_cols))

    def stacked(tb):
        return tb.reshape(NSA_GROUPS, NSA_REP * NTQ, tb.shape[-1])

    span = WINDOW + NTQ
    dw = np.arange(NTQ + span - 1) - (span - 1) + WINDOW
    biasw = stacked(toeplitz(dw, (dw >= 0) & (dw < WINDOW), NTQ, span, neg_cols=NTK))
    assert NTK > MAX_DISTANCE
    ds = np.arange(NTQ + 3 * NTK - 1) - (3 * NTK - 1) + 2 * NTK
    biass = stacked(toeplitz(ds, ds >= 0, NTQ, 3 * NTK))
    n_cmp = (t - CMP_BLOCK) // CMP_STRIDE + 1
    n_a = t // CMP_STRIDE
    dc = (CMP_STRIDE * (np.arange(n_a + LANES - 1)[None, :] - (LANES - 1))
          + np.arange(CMP_STRIDE)[:, None] - (CMP_BLOCK - 1))
    bc = toeplitz(dc, dc >= 0, n_a, LANES)
    bc = jnp.where(jnp.asarray(np.arange(LANES) < n_cmp), bc, NEG_INF)
    bc = bc.transpose(0, 2, 1, 3).reshape(NSA_GROUPS, NSA_REP, nq, NTQ, LANES)
    biasc = bc.transpose(0, 2, 1, 3, 4).reshape(NSA_GROUPS, nq, NSA_REP * NTQ, LANES)
    return biasc, biass, biasw


def _suffix_sum_matrix():
    return jnp.asarray(np.arange(TK)[:, None] >= np.arange(TK)[None, :], BF16)


def _selection_constants(t):
    n_cmp = (t - CMP_BLOCK) // CMP_STRIDE + 1
    n_sel = t // SEL_BLOCK
    n_selp = -(-n_sel // SUBLANES) * SUBLANES
    n = np.arange(LANES)[None, :]
    j = np.arange(n_selp)[:, None]
    ovl = ((CMP_STRIDE * n < SEL_BLOCK * (j + 1)) & (CMP_STRIDE * n + CMP_BLOCK > SEL_BLOCK * j)
           & (n < n_cmp) & (j < n_sel))
    ovl = np.concatenate([ovl, np.ones((1, LANES), bool), np.zeros((SUBLANES - 1, LANES), bool)], axis=0)
    own = (np.arange(t)[:, None] // SEL_BLOCK) == np.arange(LANES)[None, :]
    blockneg = jnp.where(jnp.asarray(own), NEG_INF, 0.0).astype(BF16)
    return jnp.asarray(ovl, BF16), blockneg


def kernel(x, norm1_w, w_in, cmp_pos_k, cmp_pos_v, cmp_k_w1, cmp_k_w2, cmp_v_w1, cmp_v_w2, gate_b,
           nsa_out_norm_w, sb_out_norm_w, w_out, norm2_w, w_up, conv_w, conv_b, w_down, rel_bias,
           final_norm_w):
    b, t, d = x.shape
    assert t % ROW_TILE == 0 and t % NTQ == 0 and t >= WINDOW + NTQ and (b * t) % ROW_TILE == 0
    assert (t - CMP_BLOCK) // CMP_STRIDE + 1 < LANES + 1 and t // CMP_STRIDE == LANES
    assert w_in.shape[0] == 1, "single-layer block: the closing norm is fused into the FFN kernel"
    nsa_w = NSA_HEADS * HEAD_DIM
    l = 0

    biasc, biass, biasw = _bias_tables(rel_bias, t)
    ovl, blockneg = _selection_constants(t)
    tri = _suffix_sum_matrix()

    h = x.reshape(b * t, d)
    w_all, splits = _input_weights(w_in[l])
    gb = jnp.pad(gate_b[l], (0, LANES - gate_b.shape[1])).reshape(1, LANES)
    qn, kcvc, kv, gates, sbq, sbk, sbv = _inproj(h, norm1_w[l].reshape(1, d), w_all, gb, splits, b, t)

    eye = jnp.eye(NSA_GROUPS, dtype=F32)

    def group_diag(w):
        w = jnp.einsum("...rc,gk->...grkc", w, eye)
        return w.reshape(-1, NSA_GROUPS * w.shape[-1])

    pos = jnp.stack([jnp.tile(p_.reshape(2, CMP_STRIDE, 1, HEAD_DIM), (1, 1, NSA_GROUPS, 1)).reshape(2, 1, -1)
                     for p_ in (cmp_pos_k[l], cmp_pos_v[l])])
    w1 = jnp.stack([jnp.stack([group_diag(w_.reshape(2, CMP_STRIDE, HEAD_DIM, -1)[hf]) for hf in range(2)])
                    for w_ in (cmp_k_w1[l], cmp_v_w1[l])]).astype(BF16)
    w2 = jnp.stack([group_diag(w_) for w_ in (cmp_k_w2[l], cmp_v_w2[l])]).astype(BF16)
    kcmp, vcmp = _compress(kcvc, pos, w1, w2)

    o_nsa = _nsa(qn, kv, kcmp, vcmp, gates, biasc, biass, biasw, ovl, blockneg,
                 nsa_out_norm_w[l].reshape(1, nsa_w), b, t)
    o_sb = _sb(sbq, sbk, sbv, tri, sb_out_norm_w[l].reshape(1, -1), b, t)

    wo = w_out[l].astype(BF16)
    return _mixffn(x, o_nsa, o_sb, wo[:nsa_w], wo[nsa_w:], norm2_w[l].reshape(1, d),
                   w_up[l].astype(BF16), conv_w[l], conv_b[l].reshape(1, -1),
                   w_down[l].astype(BF16), final_norm_w.reshape(1, d))
```

```python
import functools
import math

import jax
import jax.numpy as jnp
import numpy as np
from jax import lax
from jax.experimental import pallas as pl
from jax.experimental.pallas import tpu as pltpu

F32 = jnp.float32
BF16 = jnp.bfloat16

HEAD_DIM = 64
NSA_HEADS = 8
SB_HEADS = 8
NSA_GROUPS = 2
NSA_REP = NSA_HEADS // NSA_GROUPS
CMP_BLOCK = 32
CMP_STRIDE = 16
SEL_BLOCK = 64
SEL_TOP = 8
WINDOW = 512
N_BUCKETS = 32
MAX_DISTANCE = 128
CONV_WIDTH = 3
EPS = 1e-6
NEG_INF = -1e30
FORCED_BONUS = 1e6
TINY = 1e-30

LANES = 128
SUBLANES = 8
TQ = 256
TK = 256
NTQ = 256
NTK = 256
ROW_TILE = 512
FFN_CHUNK = 256
HALO = 16
VMEM_LIMIT = 56 * 1024 * 1024
WIN_TILES = WINDOW // NTK + 1


def _dot(a, b):
    return jnp.dot(a, b, preferred_element_type=F32)


def _dot_nt(a, b):
    return lax.dot_general(a, b, (((1,), (1,)), ((), ())), preferred_element_type=F32)


def _params(*sem):
    return pltpu.CompilerParams(dimension_semantics=sem, vmem_limit_bytes=VMEM_LIMIT)


def _t5_bucket_np(dist):
    n = np.maximum(dist, 0)
    max_exact = N_BUCKETS // 2
    nf = np.maximum(n, 1).astype(np.float32)
    log_b = max_exact + (np.log(nf / np.float32(max_exact)) / np.float32(math.log(MAX_DISTANCE / max_exact))
                         * np.float32(N_BUCKETS - max_exact)).astype(np.int32)
    log_b = np.minimum(log_b, N_BUCKETS - 1)
    return np.where(n < max_exact, n, log_b)


def _inproj_kernel(x_ref, nw_ref, w_ref, gb_ref, qn_ref, kcvc_ref, kv_ref, gates_ref,
                   sbq_ref, sbk_ref, sbv_ref, stage_ref, *, splits):
    x = x_ref[...]
    u = x * lax.rsqrt(jnp.mean(x * x, axis=-1, keepdims=True) + EPS) * nw_ref[...]
    u = u.astype(BF16)
    outs = (qn_ref, kcvc_ref, kv_ref, gates_ref, sbq_ref, sbk_ref, sbv_ref)
    start = 0
    for ref, size in zip(outs, splits):
        r = _dot(u, w_ref[:, start:start + size])
        if ref is gates_ref:
            r = jax.nn.sigmoid(r + gb_ref[...])
        if ref is kcvc_ref:
            for c in range(size // LANES):
                stage_ref[c] = r[:, c * LANES:(c + 1) * LANES]
                for p in range(CMP_STRIDE):
                    ref[0, p, :, c * LANES:(c + 1) * LANES] = (
                        stage_ref[c, pl.ds(p, ROW_TILE // CMP_STRIDE, stride=CMP_STRIDE), :])
        elif ref is qn_ref:
            half = lax.broadcasted_iota(jnp.int32, (r.shape[0], LANES), 1) // HEAD_DIM
            for h in range(NSA_HEADS):
                g = h // NSA_REP
                pair = r[:, (h // 2) * LANES:(h // 2 + 1) * LANES]
                if h % 2 != g:
                    pair = pltpu.roll(pair, HEAD_DIM, axis=1)
                ref[:, h * LANES:(h + 1) * LANES] = jnp.where(half == g, pair, 0.0).astype(ref.dtype)
        else:
            ref[...] = r.astype(ref.dtype)
        start += size


def _inproj(x2, norm_w, w_all, gate_b_pad, splits, b, t):
    n, d = x2.shape
    dts = (BF16, F32, BF16, F32, BF16, BF16, BF16)
    per_seq = t // ROW_TILE
    chunks = ROW_TILE // CMP_STRIDE
    widths = (NSA_HEADS * LANES,) + tuple(splits[1:])
    out_specs = [pl.BlockSpec((ROW_TILE, s), lambda i: (i, 0)) for s in widths]
    out_shape = [jax.ShapeDtypeStruct((n, s), dt) for s, dt in zip(widths, dts)]
    out_specs[1] = pl.BlockSpec((1, CMP_STRIDE, chunks, splits[1]), lambda i: (i // per_seq, 0, i % per_seq, 0))
    out_shape[1] = jax.ShapeDtypeStruct((b, CMP_STRIDE, t // CMP_STRIDE, splits[1]), F32)
    return pl.pallas_call(
        functools.partial(_inproj_kernel, splits=splits),
        grid=(n // ROW_TILE,),
        in_specs=[
            pl.BlockSpec((ROW_TILE, d), lambda i: (i, 0)),
            pl.BlockSpec((1, d), lambda i: (0, 0)),
            pl.BlockSpec(w_all.shape, lambda i: (0, 0)),
            pl.BlockSpec((1, LANES), lambda i: (0, 0)),
        ],
        out_specs=out_specs,
        out_shape=out_shape,
        scratch_shapes=[pltpu.VMEM((splits[1] // LANES, ROW_TILE, LANES), F32)],
        compiler_params=_params("parallel"),
        name="inproj",
    )(x2, norm_w, w_all, gate_b_pad)


def _gelu_tanh(x):
    return 0.5 * x * (1.0 + jnp.tanh(math.sqrt(2.0 / math.pi) * (x + 0.044715 * (x * x * x))))


def _compress_kernel(c_ref, pos_ref, w1_ref, w2_ref, kc_ref, vc_ref):
    n_chunk = c_ref.shape[2]
    row = lax.broadcasted_iota(jnp.int32, (n_chunk, LANES), 0)
    for kv, out_ref in ((0, kc_ref), (1, vc_ref)):
        def half(hf):
            x = jnp.concatenate(
                [(c_ref[0, p, :, kv * LANES:(kv + 1) * LANES]
                  + pos_ref[kv, hf, :, p * LANES:(p + 1) * LANES]).astype(BF16)
                 for p in range(CMP_STRIDE)], axis=1)
            return _dot(x, w1_ref[kv, hf])

        hidden = half(0) + pltpu.roll(half(1), n_chunk - 1, axis=0)
        out = _dot(_gelu_tanh(hidden).astype(BF16), w2_ref[kv])
        out_ref[0] = jnp.where(row < n_chunk - 1, out, 0.0).astype(out_ref.dtype)


def _compress(tokens, pos, w1, w2):
    b, _, n_chunk, _ = tokens.shape
    out = jax.ShapeDtypeStruct((b, n_chunk, LANES), BF16)
    const = lambda a: pl.BlockSpec(a.shape, lambda i: (0,) * a.ndim, pipeline_mode=pl.Buffered(1))
    return pl.pallas_call(
        _compress_kernel,
        grid=(b,),
        in_specs=[pl.BlockSpec((1,) + tokens.shape[1:], lambda i: (i, 0, 0, 0)), const(pos), const(w1), const(w2)],
        out_specs=[pl.BlockSpec((1, n_chunk, LANES), lambda i: (i, 0, 0))] * 2,
        out_shape=[out, out],
        compiler_params=_params("parallel"),
        name="compress",
    )(tokens, pos, w1, w2)


def _nsa_kernel(q_ref, ks_ref, vs_ref, kw_ref, vw_ref, kc_ref, vc_ref, gates_ref,
                biasc_ref, biass_ref, biasw_ref, ovl_ref, blockneg_ref, normw_ref,
                o_ref, s_sel_ref, s_win_ref, m_sel_ref, acc_sel_ref, owin_ref, gsel_ref, qsel_ref, ocmp_ref):
    i = pl.program_id(1)
    t0 = i * NTQ
    rows = NSA_REP * NTQ
    groups = range(NSA_GROUPS)
    lane = lax.broadcasted_iota(jnp.int32, (rows, LANES), 1)
    ones_v = jnp.ones((NTK, LANES), BF16)
    gates = gates_ref[0]

    n_selp = ovl_ref.shape[0] - SUBLANES
    blk = lax.broadcasted_iota(jnp.int32, (n_selp, NTQ), 0)
    cur = (t0 + lax.broadcasted_iota(jnp.int32, (n_selp, NTQ), 1)) // SEL_BLOCK
    causal_blk = blk <= cur
    forced = (blk == 0) | (blk == cur) | (blk == cur - 1)

    def probabilities(s, m):
        return jnp.exp((s - jnp.concatenate([m, m], axis=1)).astype(BF16))

    qgs = [jnp.concatenate(
        [q_ref[0, :, (g * NSA_REP + r) * LANES:(g * NSA_REP + r + 1) * LANES] for r in range(NSA_REP)],
        axis=0) for g in groups]

    lcs = [_dot_nt(qgs[g], kc_ref[0]) + biasc_ref[g, 0] for g in groups]

    win_tiles = []
    for k in range(WIN_TILES):
        j = i - (WIN_TILES - 1) + k
        win_tiles.append((pl.multiple_of(jnp.maximum(j, 0) * NTK, NTK),
                          pl.multiple_of(jnp.where(j < 0, WIN_TILES, k) * NTK, NTK)))
    m_win = [jnp.full((rows, LANES), NEG_INF, F32) for g in groups]
    for k, (off, boff) in enumerate(win_tiles):
        key = kw_ref[0, pl.ds(off, NTK), :]
        raw = [_dot_nt(qgs[g], key) for g in groups]
        for g in groups:
            s = raw[g] + biasw_ref[g, :, pl.ds(boff, NTK)]
            s_win_ref[g, :, k * NTK:(k + 1) * NTK] = s
            m_win[g] = jnp.maximum(m_win[g], jnp.maximum(s[:, :LANES], s[:, LANES:]))

    ecs = []
    for g in groups:
        mc = jnp.maximum(jnp.max(lcs[g], axis=-1, keepdims=True), 0.1 * NEG_INF)
        ec = jnp.exp(lcs[g] - mc).astype(BF16)
        oc = _dot(ec, jnp.concatenate([vc_ref[0], ones_v[:LANES]], axis=1))
        ocmp_ref[g] = oc[:, :LANES] * (1.0 / jnp.maximum(oc[:, LANES:], TINY))
        ecs.append(ec)

    for g in groups:
        imp4 = _dot_nt(ovl_ref[...], ecs[g])
        imp4 = imp4[:n_selp] * (1.0 / jnp.maximum(imp4[n_selp:n_selp + 1], TINY))
        imp = imp4[:, 0:NTQ]
        for r in range(1, NSA_REP):
            imp = imp + imp4[:, r * NTQ:(r + 1) * NTQ]
        score = jnp.where(causal_blk, imp + jnp.where(forced, FORCED_BONUS, 0.0), NEG_INF)
        rank = jnp.zeros((n_selp, NTQ), F32)
        for j in range(n_selp):
            other = score[j:j + 1, :]
            ahead = (other > score) | ((other == score) & (blk > j))
            rank = rank + jnp.where(ahead, 1.0, 0.0)
        unsel_t = jnp.where((rank < SEL_TOP) & causal_blk, 0.0, 1.0)
        unsel_t = jnp.concatenate([unsel_t, jnp.zeros((LANES - n_selp, NTQ), F32)], axis=0)
        unsel = unsel_t.T.astype(BF16)

        qsel_ref[g] = jnp.concatenate([qgs[g], jnp.concatenate([unsel] * NSA_REP, axis=0)], axis=1)

    for g in groups:
        m_win[g] = jnp.broadcast_to(jnp.max(m_win[g], axis=-1, keepdims=True), (rows, LANES))
    acc_win = [jnp.zeros((rows, 2 * LANES), F32) for g in groups]
    for k, (off, _) in enumerate(win_tiles):
        v = jnp.concatenate([vw_ref[0, pl.ds(off, NTK), :], ones_v], axis=1)
        ps = [probabilities(s_win_ref[g, :, k * NTK:(k + 1) * NTK], m_win[g]) for g in groups]
        for g in groups:
            acc_win[g] = acc_win[g] + _dot(ps[g], v)

    def gate(g, branch):
        return jnp.concatenate(
            [gates[:, (g * NSA_REP + r) * 3 + branch:(g * NSA_REP + r) * 3 + branch + 1]
             for r in range(NSA_REP)], axis=0)

    for g in groups:
        ocmp_ref[g] = gate(g, 0) * ocmp_ref[g]
        gsel_ref[g] = jnp.broadcast_to(gate(g, 1), (rows, LANES))
        owin_ref[g] = gate(g, 2) * (acc_win[g][:, :LANES] * (1.0 / acc_win[g][:, LANES:]))

    m_sel_ref[...] = jnp.full(m_sel_ref.shape, NEG_INF, F32)
    acc_sel_ref[...] = jnp.zeros(acc_sel_ref.shape, F32)

    n_double = (i + 1) // 2

    def sel_logits(j, n_tiles):
        off = pl.multiple_of(j * NTK, NTK)
        width = n_tiles * NTK
        key = jnp.concatenate([ks_ref[0, pl.ds(off, width), :], blockneg_ref[pl.ds(off, width), :]], axis=1)
        raw = [_dot_nt(qsel_ref[g], key) for g in groups]
        for g in groups:
            m = None
            for k in range(n_tiles):
                boff = pl.multiple_of((2 - jnp.minimum(i - j - k, 2)) * NTK, NTK)
                s = raw[g][:, k * NTK:(k + 1) * NTK] + biass_ref[g, :, pl.ds(boff, NTK)]
                s_sel_ref[g, :, pl.ds(pl.multiple_of(off + k * NTK, NTK), NTK)] = s
                tile_max = jnp.maximum(s[:, :LANES], s[:, LANES:])
                m = tile_max if m is None else jnp.maximum(m, tile_max)
            m = jnp.broadcast_to(jnp.max(m, axis=-1, keepdims=True), (rows, LANES))
            m_sel_ref[g] = jnp.maximum(m_sel_ref[g], m)

    def sel_weights(j, n_tiles):
        off = pl.multiple_of(j * NTK, NTK)
        width = n_tiles * NTK
        v = jnp.concatenate([vs_ref[0, pl.ds(off, width), :], jnp.ones((width, LANES), BF16)], axis=1)
        ps = []
        for g in groups:
            m = m_sel_ref[g]
            ps.append(jnp.exp((s_sel_ref[g, :, pl.ds(off, width)]
                               - jnp.concatenate([m] * (width // LANES), axis=1)).astype(BF16)))
        for g in groups:
            acc_sel_ref[g] += _dot(ps[g], v)

    def sweep(tile_fn):
        def double(d, c):
            tile_fn(2 * d, 2)
            return c

        def single(_, c):
            tile_fn(i, 1)
            return c

        lax.fori_loop(0, n_double, double, 0)
        lax.fori_loop(0, (i + 1) % 2, single, 0)

    sweep(sel_logits)
    sweep(sel_weights)

    for g in groups:
        o_sel = acc_sel_ref[g, :, :LANES] * (1.0 / acc_sel_ref[g, :, LANES:])
        o = ocmp_ref[g] + gsel_ref[g] * o_sel + owin_ref[g]
        mine = (lane // HEAD_DIM) == g
        ssq = jnp.sum(jnp.where(mine, o * o, 0.0), axis=-1, keepdims=True)
        y = o * lax.rsqrt(ssq * (1.0 / HEAD_DIM) + EPS)
        left = lax.broadcasted_iota(jnp.int32, (NTQ, LANES), 1) < HEAD_DIM
        for pair in range(NSA_REP // 2):
            even = y[(2 * pair) * NTQ:(2 * pair + 1) * NTQ]
            odd = y[(2 * pair + 1) * NTQ:(2 * pair + 2) * NTQ]
            if g == 0:
                odd = pltpu.roll(odd, HEAD_DIM, axis=1)
            else:
                even = pltpu.roll(even, HEAD_DIM, axis=1)
            cb = g * (NSA_REP // 2) + pair
            packed = jnp.where(left, even, odd) * normw_ref[:, cb * LANES:(cb + 1) * LANES]
            o_ref[0, :, cb * LANES:(cb + 1) * LANES] = packed.astype(o_ref.dtype)


def _nsa(qn, kv, kcmp, vcmp, gates, biasc, biass, biasw, ovl, blockneg, normw, b, t):
    nq = t // NTQ
    rows = NSA_REP * NTQ
    qn3 = qn.reshape(b, t, qn.shape[-1])
    kv3 = kv.reshape(b, t, kv.shape[-1])
    gates3 = gates.reshape(b, t, LANES)
    kv_spec = lambda c: pl.BlockSpec((1, t, LANES), lambda bi, qi, c=c: (bi, 0, c))
    const = lambda a: pl.BlockSpec(a.shape, lambda bi, qi: (0,) * a.ndim, pipeline_mode=pl.Buffered(1))
    return pl.pallas_call(
        _nsa_kernel,
        grid=(b, nq),
        in_specs=[
            pl.BlockSpec((1, NTQ, qn3.shape[-1]), lambda bi, qi: (bi, qi, 0)),
            kv_spec(0), kv_spec(1), kv_spec(2), kv_spec(3),
            pl.BlockSpec((1,) + kcmp.shape[1:], lambda bi, qi: (bi, 0, 0)),
            pl.BlockSpec((1,) + vcmp.shape[1:], lambda bi, qi: (bi, 0, 0)),
            pl.BlockSpec((1, NTQ, LANES), lambda bi, qi: (bi, qi, 0)),
            pl.BlockSpec((NSA_GROUPS, 1, rows, LANES), lambda bi, qi: (0, qi, 0, 0)),
            const(biass), const(biasw), const(ovl), const(blockneg), const(normw),
        ],
        out_specs=pl.BlockSpec((1, NTQ, NSA_HEADS * HEAD_DIM), lambda bi, qi: (bi, qi, 0)),
        out_shape=jax.ShapeDtypeStruct((b, t, NSA_HEADS * HEAD_DIM), BF16),
        scratch_shapes=[
            pltpu.VMEM((NSA_GROUPS, rows, t), F32),
            pltpu.VMEM((NSA_GROUPS, rows, WIN_TILES * NTK), F32),
            pltpu.VMEM((NSA_GROUPS, rows, LANES), F32),
            pltpu.VMEM((NSA_GROUPS, rows, 2 * LANES), F32),
            pltpu.VMEM((NSA_GROUPS, rows, LANES), F32),
            pltpu.VMEM((NSA_GROUPS, rows, LANES), F32),
            pltpu.VMEM((NSA_GROUPS, rows, 2 * LANES), BF16),
            pltpu.VMEM((NSA_GROUPS, rows, LANES), F32),
        ],
        compiler_params=_params("parallel", "arbitrary"),
        name="nsa",
    )(qn3, kv3, kv3, kv3, kv3, kcmp, vcmp, gates3, biasc, biass, biasw, ovl, blockneg, normw)


def _sb_kernel(q_ref, k_ref, v_ref, tri_ref, normw_ref, o_ref, acc_ref, run_ref):
    i = pl.program_id(1)
    rows = 2 * TQ
    n_pairs = SB_HEADS // 2
    lane = lax.broadcasted_iota(jnp.int32, (rows, LANES), 1)
    row = lax.broadcasted_iota(jnp.int32, (rows, LANES), 0)
    strict = lax.broadcasted_iota(jnp.int32, (rows, TK), 1) < (
        lax.broadcasted_iota(jnp.int32, (rows, TK), 0) & (TQ - 1))
    mine = (lane // HEAD_DIM) == (row // TQ)
    left = lax.broadcasted_iota(jnp.int32, (TQ, LANES), 1) < HEAD_DIM

    def tile(j, diag):
        off = pl.multiple_of(j * TK, TK)
        cols = [slice(p * LANES, (p + 1) * LANES) for p in range(n_pairs)]
        zs, csums = {}, {}

        def scores(p):
            q2 = q_ref[0, :, cols[p]]
            zero = jnp.zeros_like(q2)
            qp = jnp.concatenate([jnp.where(left, q2, zero), jnp.where(left, zero, q2)],
                                 axis=0)
            zs[p] = _dot_nt(qp, k_ref[0, pl.ds(off, TK), cols[p]])

        def suffix_sums(p):
            z = zs[p]
            zb = z.astype(BF16)
            sp = jnp.maximum(zb, 0) + jnp.log(1 + jnp.exp(-jnp.abs(zb)))
            if diag:
                sp = jnp.where(strict, sp, jnp.zeros_like(sp))
            csums[p] = _dot(sp, tri_ref[...])

        def weights_times_v(p):
            v = v_ref[0, pl.ds(off, TK), cols[p]]
            z, csum = zs.pop(p), csums.pop(p)
            tile_sum = jnp.broadcast_to(csum[:, 0:1], (rows, LANES))
            if diag:
                a = jnp.where(strict, jnp.exp(z - csum), 0.0)
                acc_ref[p] = _dot(a.astype(BF16), v)
                run_ref[p] = tile_sum
            else:
                run = run_ref[p]
                a = jnp.exp(z - csum - jnp.concatenate([run] * (TK // LANES), axis=1))
                acc_ref[p] += _dot(a.astype(BF16), v)
                run_ref[p] = run + tile_sum

        stages = (scores, suffix_sums, weights_times_v)
        for tick in range(n_pairs + len(stages) - 1):
            for k, stage in enumerate(stages):
                if 0 <= tick - k < n_pairs:
                    stage(tick - k)

    tile(i, True)

    def body(step, carry):
        tile(i - 1 - step, False)
        return carry

    lax.fori_loop(0, i, body, 0)

    for pair in range(n_pairs):
        cols = slice(pair * LANES, (pair + 1) * LANES)
        acc = acc_ref[pair]
        ssq = jnp.sum(jnp.where(mine, acc * acc, 0.0), axis=-1, keepdims=True)
        y = acc * lax.rsqrt(ssq * (1.0 / HEAD_DIM) + EPS)
        packed = jnp.where(left, y[:TQ], y[TQ:]) * normw_ref[:, cols]
        o_ref[0, :, cols] = packed.astype(o_ref.dtype)


def _sb(q, k, v, tri, normw, b, t):
    q3 = q.reshape(b, t, q.shape[-1])
    k3 = k.reshape(b, t, k.shape[-1])
    v3 = v.reshape(b, t, v.shape[-1])
    width = SB_HEADS * HEAD_DIM
    return pl.pallas_call(
        _sb_kernel,
        grid=(b, t // TQ),
        in_specs=[
            pl.BlockSpec((1, TQ, q3.shape[-1]), lambda bi, qi: (bi, qi, 0)),
            pl.BlockSpec((1, t, width), lambda bi, qi: (bi, 0, 0)),
            pl.BlockSpec((1, t, width), lambda bi, qi: (bi, 0, 0)),
            pl.BlockSpec(tri.shape, lambda bi, qi: (0, 0)),
            pl.BlockSpec(normw.shape, lambda bi, qi: (0, 0)),
        ],
        out_specs=pl.BlockSpec((1, TQ, width), lambda bi, qi: (bi, qi, 0)),
        out_shape=jax.ShapeDtypeStruct((b, t, width), BF16),
        scratch_shapes=[pltpu.VMEM((SB_HEADS // 2, 2 * TQ, LANES), F32)] * 2,
        compiler_params=_params("parallel", "arbitrary"),
        name="stickbreak",
    )(q3, k3, v3, tri, normw)


def _mixffn_kernel(x_ref, on_ref, os_ref, xh_ref, onh_ref, osh_ref, wn_ref, ws_ref, n2_ref,
                   wup_ref, cw_ref, cb_ref, wdown_ref, fw_ref, o_ref, up_ref, act_ref, *, d_ff):
    i = pl.program_id(1)
    tile = x_ref.shape[1]

    def mix(x, o_nsa, o_sb):
        h = x + _dot(o_nsa, wn_ref[...]) + _dot(o_sb, ws_ref[...])
        u = h * lax.rsqrt(jnp.mean(h * h, axis=-1, keepdims=True) + EPS) * n2_ref[...]
        return h, u.astype(BF16)

    h, u = mix(x_ref[0], on_ref[0], os_ref[0])
    _, u_halo = mix(xh_ref[0], onh_ref[0], osh_ref[0])
    halo = jnp.where(i > 0, u_halo, jnp.zeros_like(u_halo))
    ue = jnp.concatenate([halo, u], axis=0)

    def conv(slot, c0):
        w = cw_ref[:, c0:c0 + FFN_CHUNK]
        taps = [up_ref[slot, HALO - k:HALO - k + tile] for k in range(CONV_WIDTH)]
        return (w[2:3] * taps[0] + w[1:2] * taps[1] + w[0:1] * taps[2]
                + cb_ref[:, c0:c0 + FFN_CHUNK])

    def up_project(c):
        slot = 2 * (c % 2)
        up_ref[slot] = _dot(ue, wup_ref[:, c * FFN_CHUNK:(c + 1) * FFN_CHUNK])
        up_ref[slot + 1] = _dot(ue, wup_ref[:, d_ff + c * FFN_CHUNK:d_ff + (c + 1) * FFN_CHUNK])

    n_chunks = d_ff // FFN_CHUNK
    up_project(0)
    for c in range(n_chunks):
        g0, v0 = c * FFN_CHUNK, d_ff + c * FFN_CHUNK
        slot = 2 * (c % 2)
        if c + 1 < n_chunks:
            up_project(c + 1)
        gate = conv(slot, g0)
        val = conv(slot + 1, v0)
        act = gate * (1.0 / (1.0 + jnp.exp(-gate))) * val
        act_ref[:, g0:g0 + FFN_CHUNK] = act.astype(BF16)
    acc = h + _dot(act_ref[...], wdown_ref[...])
    y = acc * lax.rsqrt(jnp.mean(acc * acc, axis=-1, keepdims=True) + EPS) * fw_ref[...]
    o_ref[0] = y.astype(o_ref.dtype)


def _mixffn(x, o_nsa, o_sb, w_n, w_s, norm2_w, w_up, conv_w, conv_b, w_down, final_w):
    b, t, d = x.shape
    d_ff = w_down.shape[0]
    per = ROW_TILE // HALO
    tile = lambda a: pl.BlockSpec((1, ROW_TILE, a.shape[-1]), lambda bi, ti: (bi, ti, 0))
    halo = lambda a: pl.BlockSpec((1, HALO, a.shape[-1]), lambda bi, ti: (bi, jnp.maximum(ti * per - 1, 0), 0))
    const = lambda a: pl.BlockSpec(a.shape, lambda bi, ti: (0, 0), pipeline_mode=pl.Buffered(1))
    weights = (w_n, w_s, norm2_w, w_up, conv_w, conv_b, w_down, final_w)
    return pl.pallas_call(
        functools.partial(_mixffn_kernel, d_ff=d_ff),
        grid=(b, t // ROW_TILE),
        in_specs=[tile(x), tile(o_nsa), tile(o_sb), halo(x), halo(o_nsa), halo(o_sb)]
                 + [const(w) for w in weights],
        out_specs=pl.BlockSpec((1, ROW_TILE, d), lambda bi, ti: (bi, ti, 0)),
        out_shape=jax.ShapeDtypeStruct((b, t, d), F32),
        scratch_shapes=[pltpu.VMEM((4, HALO + ROW_TILE, FFN_CHUNK), F32),
                        pltpu.VMEM((ROW_TILE, d_ff), BF16)],
        compiler_params=_params("parallel", "arbitrary"),
        name="mixffn",
    )(x, o_nsa, o_sb, x, o_nsa, o_sb, *weights)


def _input_weights(w):
    d = w.shape[0]
    nq, kvw = NSA_HEADS * HEAD_DIM, NSA_GROUPS * HEAD_DIM
    scale = HEAD_DIM ** -0.5
    o = 0
    q_n = w[:, o:o + nq] * scale; o += nq
    kc = w[:, o:o + kvw]; o += kvw
    vc = w[:, o:o + kvw]; o += kvw
    ks = w[:, o:o + kvw]; o += kvw
    vs = w[:, o:o + kvw]; o += kvw
    kw = w[:, o:o + kvw]; o += kvw
    vw = w[:, o:o + kvw]; o += kvw
    gl = w[:, o:o + NSA_HEADS * 3]; o += NSA_HEADS * 3
    sbw = SB_HEADS * HEAD_DIM
    q_s = w[:, o:o + sbw] * scale; o += sbw
    k_s = w[:, o:o + sbw]; o += sbw
    v_s = w[:, o:o + sbw]; o += sbw
    gl = jnp.pad(gl, ((0, 0), (0, LANES - gl.shape[1])))
    pieces = (q_n, jnp.concatenate([kc, vc], axis=1), jnp.concatenate([ks, vs, kw, vw], axis=1),
              gl, q_s, k_s, v_s)
    return jnp.concatenate(pieces, axis=1).astype(BF16), tuple(p.shape[1] for p in pieces)


def _toeplitz_kernel(w_ref, o_ref, *, n_cols):
    n_rows = o_ref.shape[2]
    for k in range(w_ref.shape[1]):
        x = jnp.broadcast_to(w_ref[0, k], (n_rows, w_ref.shape[-1]))
        o_ref[0, k, :, :n_cols] = pltpu.roll(x, 0, 1, stride=1, stride_axis=0)[:, :n_cols]
        if n_cols < o_ref.shape[-1]:
            o_ref[0, k, :, n_cols:] = jnp.full((n_rows, o_ref.shape[-1] - n_cols), NEG_INF, F32)


def _toeplitz_rows(w, n_rows, n_cols, neg_cols):
    h, k, width = w.shape
    return pl.pallas_call(
        functools.partial(_toeplitz_kernel, n_cols=n_cols),
        grid=(h,),
        in_specs=[pl.BlockSpec((1, k, 1, width), lambda i: (i, 0, 0, 0))],
        out_specs=pl.BlockSpec((1, k, n_rows, n_cols + neg_cols), lambda i: (i, 0, 0, 0)),
        out_shape=jax.ShapeDtypeStruct((h, k, n_rows, n_cols + neg_cols), F32),
        compiler_params=_params("parallel"),
        name="toeplitz",
    )(w.reshape(h, k, 1, width))


def _bias_tables(rel_bias, t):
    nq = t // NTQ
    rb = rel_bias.T.astype(F32)

    def by_signed(d, ok):
        onehot = (_t5_bucket_np(d).reshape(1, -1) == np.arange(N_BUCKETS)[:, None]) & ok.reshape(1, -1)
        vals = jnp.dot(rb, jnp.asarray(onehot, F32), precision=lax.Precision.HIGHEST)
        vals = vals + jnp.asarray(np.where(ok.reshape(1, -1), 0.0, NEG_INF), F32)
        return vals.reshape((rb.shape[0],) + d.shape)

    def toeplitz(d, ok, n_rows, n_cols, neg_cols=0):
        p = d.shape[-1]
        assert p == n_rows + n_cols - 1
        width = -(-p // LANES) * LANES
        src = (n_cols - 1 - np.arange(width)) % width
        w = by_signed(d[..., np.minimum(src, p - 1)], ok[..., np.minimum(src, p - 1)] & (src < p))
        return _toeplitz_rows(w.reshape(rb.shape[0], -1, width), n_rows, n_cols, neg_cols).reshape(
            w.shape[:-1] + (n_rows, n_cols + neg_cols))

    def stacked(tb):
        return tb.reshape(NSA_GROUPS, NSA_REP * NTQ, tb.shape[-1])

    span = WINDOW + NTQ
    dw = np.arange(NTQ + span - 1) - (span - 1) + WINDOW
    biasw = stacked(toeplitz(dw, (dw >= 0) & (dw < WINDOW), NTQ, span, neg_cols=NTK))
    assert NTK > MAX_DISTANCE
    ds = np.arange(NTQ + 3 * NTK - 1) - (3 * NTK - 1) + 2 * NTK
    biass = stacked(toeplitz(ds, ds >= 0, NTQ, 3 * NTK))
    n_cmp = (t - CMP_BLOCK) // CMP_STRIDE + 1
    n_a = t // CMP_STRIDE
    dc = (CMP_STRIDE * (np.arange(n_a + LANES - 1)[None, :] - (LANES - 1))
          + np.arange(CMP_STRIDE)[:, None] - (CMP_BLOCK - 1))
    bc = toeplitz(dc, dc >= 0, n_a, LANES)
    bc = jnp.where(jnp.asarray(np.arange(LANES) < n_cmp), bc, NEG_INF)
    bc = bc.transpose(0, 2, 1, 3).reshape(NSA_GROUPS, NSA_REP, nq, NTQ, LANES)
    biasc = bc.transpose(0, 2, 1, 3, 4).reshape(NSA_GROUPS, nq, NSA_REP * NTQ, LANES)
    return biasc, biass, biasw


def _suffix_sum_matrix():
    return jnp.asarray(np.arange(TK)[:, None] >= np.arange(TK)[None, :], BF16)


def _selection_constants(t):
    n_cmp = (t - CMP_BLOCK) // CMP_STRIDE + 1
    n_sel = t // SEL_BLOCK
    n_selp = -(-n_sel // SUBLANES) * SUBLANES
    n = np.arange(LANES)[None, :]
    j = np.arange(n_selp)[:, None]
    ovl = ((CMP_STRIDE * n < SEL_BLOCK * (j + 1)) & (CMP_STRIDE * n + CMP_BLOCK > SEL_BLOCK * j)
           & (n < n_cmp) & (j < n_sel))
    ovl = np.concatenate([ovl, np.ones((1, LANES), bool), np.zeros((SUBLANES - 1, LANES), bool)], axis=0)
    own = (np.arange(t)[:, None] // SEL_BLOCK) == np.arange(LANES)[None, :]
    blockneg = jnp.where(jnp.asarray(own), NEG_INF, 0.0).astype(BF16)
    return jnp.asarray(ovl, BF16), blockneg


def kernel(x, norm1_w, w_in, cmp_pos_k, cmp_pos_v, cmp_k_w1, cmp_k_w2, cmp_v_w1, cmp_v_w2, gate_b,
           nsa_out_norm_w, sb_out_norm_w, w_out, norm2_w, w_up, conv_w, conv_b, w_down, rel_bias,
           final_norm_w):
    b, t, d = x.shape
    assert t % ROW_TILE == 0 and t % NTQ == 0 and t >= WINDOW + NTQ and (b * t) % ROW_TILE == 0
    assert (t - CMP_BLOCK) // CMP_STRIDE + 1 < LANES + 1 and t // CMP_STRIDE == LANES
    assert w_in.shape[0] == 1, "single-layer block: the closing norm is fused into the FFN kernel"
    nsa_w = NSA_HEADS * HEAD_DIM
    l = 0

    biasc, biass, biasw = _bias_tables(rel_bias, t)
    ovl, blockneg = _selection_constants(t)
    tri = _suffix_sum_matrix()

    h = x.reshape(b * t, d)
    w_all, splits = _input_weights(w_in[l])
    gb = jnp.pad(gate_b[l], (0, LANES - gate_b.shape[1])).reshape(1, LANES)
    qn, kcvc, kv, gates, sbq, sbk, sbv = _inproj(h, norm1_w[l].reshape(1, d), w_all, gb, splits, b, t)

    eye = jnp.eye(NSA_GROUPS, dtype=F32)

    def group_diag(w):
        w = jnp.einsum("...rc,gk->...grkc", w, eye)
        return w.reshape(-1, NSA_GROUPS * w.shape[-1])

    pos = jnp.stack([jnp.tile(p_.reshape(2, CMP_STRIDE, 1, HEAD_DIM), (1, 1, NSA_GROUPS, 1)).reshape(2, 1, -1)
                     for p_ in (cmp_pos_k[l], cmp_pos_v[l])])
    w1 = jnp.stack([jnp.stack([group_diag(w_.reshape(2, CMP_STRIDE, HEAD_DIM, -1)[hf]) for hf in range(2)])
                    for w_ in (cmp_k_w1[l], cmp_v_w1[l])]).astype(BF16)
    w2 = jnp.stack([group_diag(w_) for w_ in (cmp_k_w2[l], cmp_v_w2[l])]).astype(BF16)
    kcmp, vcmp = _compress(kcvc, pos, w1, w2)

    o_nsa = _nsa(qn, kv, kcmp, vcmp, gates, biasc, biass, biasw, ovl, blockneg,
                 nsa_out_norm_w[l].reshape(1, nsa_w), b, t)
    o_sb = _sb(sbq, sbk, sbv, tri, sb_out_norm_w[l].reshape(1, -1), b, t)

    wo = w_out[l].astype(BF16)
    return _mixffn(x, o_nsa, o_sb, wo[:nsa_w], wo[nsa_w:], norm2_w[l].reshape(1, d),
                   w_up[l].astype(BF16), conv_w[l], conv_b[l].reshape(1, -1),
                   w_down[l].astype(BF16), final_norm_w.reshape(1, d))
```
